```python
import jax, jax.numpy as jnp
from jax import lax
import numpy as np

D_MODEL = 2048
BATCH = 2
SEQ = 8192
DEPTH = 4

GRID_W = 64
CTX_LEN = 256
N_MIXERS = 4
NORM_EPS = 1e-6
ROPE_THETA = 10000.0
Q_BLOCK = 128
NEG_INF = -1e30

CONV_WIDTH = 3
GQA_HEADS = 16
GQA_KV_HEADS = 4
GQA_HEAD_DIM = D_MODEL // GQA_HEADS
POOL_WINDOWS = (2, 4, 8, 16)
POOL_GROUP = D_MODEL // len(POOL_WINDOWS)
SWA_HEADS = 32
SWA_KV_HEADS = 8
SWA_HEAD_DIM = D_MODEL // SWA_HEADS
SWA_WINDOW = 128
N_GROUPS = 8
EXPERTS_PER_GROUP = 4
N_EXPERTS = N_GROUPS * EXPERTS_PER_GROUP
TOP_K = 2
D_EXPERT = D_MODEL // 4
MOE_BLOCK = 128

kernel_name = 'hybrid_interleaved_diffusion_trunk'


def rmsnorm(x, gain):
    xf = x.astype(jnp.float32)
    y = xf * lax.rsqrt(jnp.mean(xf * xf, axis=-1, keepdims=True) + NORM_EPS)
    return (y * gain.astype(jnp.float32)).astype(x.dtype)


def axial_rope_tables(rows, head_dim):
    quarter = head_dim // 4
    row = jnp.repeat(jnp.arange(rows), GRID_W).astype(jnp.float32)
    col = jnp.tile(jnp.arange(GRID_W), rows).astype(jnp.float32)
    inv = ROPE_THETA ** (-jnp.arange(quarter, dtype=jnp.float32) / quarter)
    ang = jnp.concatenate([row[:, None] * inv, col[:, None] * inv], axis=-1)
    return jnp.cos(ang), jnp.sin(ang)


def apply_rope(x, cos, sin):
    half = x.shape[-1] // 2
    x1, x2 = x[..., :half], x[..., half:]
    cs = cos[:, None, :].astype(x.dtype)
    sn = sin[:, None, :].astype(x.dtype)
    return jnp.concatenate([x1 * cs - x2 * sn, x1 * sn + x2 * cs], axis=-1)


def attend(q, k, v, mask=None, sink=None):
    scale = q.shape[-1] ** -0.5
    s = jnp.einsum('bqhgd,bkhd->bhgqk', q, k).astype(jnp.float32) * scale
    if mask is not None:
        s = jnp.where(mask, s, NEG_INF)
    if sink is not None:
        sk = jnp.broadcast_to(sink.astype(jnp.float32)[None, :, :, None, None], s.shape[:-1] + (1,))
        p = jax.nn.softmax(jnp.concatenate([s, sk], axis=-1), axis=-1)[..., :-1]
    else:
        p = jax.nn.softmax(s, axis=-1)
    return jnp.einsum('bhgqk,bkhd->bqhgd', p.astype(v.dtype), v)


def project_qkv(u, w_qkv, n_heads, n_kv, head_dim):
    qkv = u @ w_qkv
    q, k, v = jnp.split(qkv, [n_heads * head_dim, (n_heads + n_kv) * head_dim], axis=-1)
    lead = u.shape[:-1]
    return (q.reshape(lead + (n_heads, head_dim)),
            k.reshape(lead + (n_kv, head_dim)),
            v.reshape(lead + (n_kv, head_dim)))


def to_query_blocks(q, n_kv):
    B, S, H, d = q.shape
    return jnp.moveaxis(q.reshape(B, S // Q_BLOCK, Q_BLOCK, n_kv, H // n_kv, d), 1, 0)


def from_query_blocks(ob):
    nb, B, qb, hkv, g, d = ob.shape
    return jnp.moveaxis(ob, 0, 1).reshape(B, nb * qb, hkv * g * d)


def short_gated_conv(u, w_in, conv_k, w_out):
    S = u.shape[1]
    b_gate, c_gate, xv = jnp.split(u @ w_in, 3, axis=-1)
    half = CONV_WIDTH // 2
    z = jnp.pad(c_gate * xv, ((0, 0), (half, half), (0, 0)))
    y = sum(z[:, kk:kk + S] * conv_k[kk] for kk in range(CONV_WIDTH))
    return (b_gate * y) @ w_out


def multiscale_pool(u, pool_w, pool_scale):
    B, S, D = u.shape
    cs = jnp.pad(jnp.cumsum(u.astype(jnp.float32), axis=1), ((0, 0), (1, 0), (0, 0)))
    t = jnp.arange(S)
    outs = []
    for g, w in enumerate(POOL_WINDOWS):
        left = w // 2
        right = w - 1 - left
        lo = jnp.clip(t - left, 0, S)
        hi = jnp.clip(t + right + 1, 0, S)
        sl = slice(g * POOL_GROUP, (g + 1) * POOL_GROUP)
        csg = cs[..., sl]
        mean = (csg[:, hi] - csg[:, lo]) / (hi - lo).astype(jnp.float32)[None, :, None]
        outs.append((mean.astype(u.dtype) - u[..., sl]) @ pool_w[g])
    return jnp.concatenate(outs, axis=-1) * pool_scale


def global_axial_gqa(u_lat, u_ctx, w_qkv, q_gain, k_gain, w_out, cos, sin, with_ctx_out):
    B, S, _ = u_lat.shape
    q, k, v = project_qkv(u_lat, w_qkv, GQA_HEADS, GQA_KV_HEADS, GQA_HEAD_DIM)
    q = apply_rope(rmsnorm(q, q_gain), cos, sin)
    k = apply_rope(rmsnorm(k, k_gain), cos, sin)
    qc, kc, vc = project_qkv(u_ctx, w_qkv, GQA_HEADS, GQA_KV_HEADS, GQA_HEAD_DIM)
    kc = rmsnorm(kc, k_gain)
    k_all = jnp.concatenate([k, kc], axis=1)
    v_all = jnp.concatenate([v, vc], axis=1)
    ob = lax.map(lambda qq: attend(qq, k_all, v_all), to_query_blocks(q, GQA_KV_HEADS))
    y_lat = from_query_blocks(ob) @ w_out
    y_ctx = None
    if with_ctx_out:
        qg = rmsnorm(qc, q_gain)
        n_ctx = qg.shape[1]
        qg = qg.reshape(B, n_ctx, GQA_KV_HEADS, GQA_HEADS // GQA_KV_HEADS, GQA_HEAD_DIM)
        y_ctx = attend(qg, kc, vc).reshape(B, n_ctx, GQA_HEADS * GQA_HEAD_DIM) @ w_out
    return y_lat, y_ctx


def windowed_sink_gqa(u_lat, u_ctx, w_qkv, sink, w_out, cos, sin, with_ctx_out):
    B, S, _ = u_lat.shape
    q, k, v = project_qkv(u_lat, w_qkv, SWA_HEADS, SWA_KV_HEADS, SWA_HEAD_DIM)
    q = apply_rope(q, cos, sin)
    k = apply_rope(k, cos, sin)
    qc, kc, vc = project_qkv(u_ctx, w_qkv, SWA_HEADS, SWA_KV_HEADS, SWA_HEAD_DIM)
    n_ctx = kc.shape[1]
    sink_g = sink.reshape(SWA_KV_HEADS, SWA_HEADS // SWA_KV_HEADS)
    pad = ((0, 0), (SWA_WINDOW, SWA_WINDOW), (0, 0), (0, 0))
    k_pad = jnp.pad(k, pad)
    v_pad = jnp.pad(v, pad)
    span = Q_BLOCK + 2 * SWA_WINDOW
    nb = S // Q_BLOCK
    ctx_mask = jnp.ones((Q_BLOCK, n_ctx), dtype=bool)

    def one_block(args):
        qq, b = args
        start = b * Q_BLOCK
        kw = lax.dynamic_slice_in_dim(k_pad, start, span, axis=1)
        vw = lax.dynamic_slice_in_dim(v_pad, start, span, axis=1)
        qpos = start + jnp.arange(Q_BLOCK)
        kpos = start - SWA_WINDOW + jnp.arange(span)
        win = (jnp.abs(qpos[:, None] - kpos[None, :]) <= SWA_WINDOW) & ((kpos >= 0) & (kpos < S))[None, :]
        mask = jnp.concatenate([win, ctx_mask], axis=1)
        return attend(qq, jnp.concatenate([kw, kc], axis=1), jnp.concatenate([vw, vc], axis=1), mask, sink_g)

    ob = lax.map(one_block, (to_query_blocks(q, SWA_KV_HEADS), jnp.arange(nb)))
    y_lat = from_query_blocks(ob) @ w_out
    y_ctx = None
    if with_ctx_out:
        qg = qc.reshape(B, n_ctx, SWA_KV_HEADS, SWA_HEADS // SWA_KV_HEADS, SWA_HEAD_DIM)
        y_ctx = attend(qg, kc, vc, None, sink_g).reshape(B, n_ctx, SWA_HEADS * SWA_HEAD_DIM) @ w_out
    return y_lat, y_ctx


def hierarchical_moe(h, rg_w, rg_b, re_w, re_b, w_gate, w_up, w_down):
    T, D = h.shape
    grp_logits = (h @ rg_w).astype(jnp.float32) + rg_b
    grp_prob = jax.nn.softmax(grp_logits, axis=-1)
    grp = jnp.argmax(grp_logits, axis=-1)
    p_grp = jnp.max(grp_prob, axis=-1)
    exp_logits = ((h @ re_w).astype(jnp.float32) + re_b).reshape(T, N_GROUPS, EXPERTS_PER_GROUP)
    in_grp = exp_logits[jnp.arange(T), grp]
    top_val, top_idx = lax.top_k(in_grp, TOP_K)
    gate = jax.nn.softmax(top_val, axis=-1) * p_grp[:, None]
    expert = grp[:, None] * EXPERTS_PER_GROUP + top_idx

    A = T * TOP_K
    e_flat = expert.reshape(A)
    tok = jnp.repeat(jnp.arange(T), TOP_K)
    order = jnp.argsort(e_flat)
    e_s, tok_s, g_s = e_flat[order], tok[order], gate.reshape(A)[order]
    counts = jnp.bincount(e_flat, length=N_EXPERTS)
    start = jnp.cumsum(counts) - counts
    padded = ((counts + MOE_BLOCK - 1) // MOE_BLOCK) * MOE_BLOCK
    pad_end = jnp.cumsum(padded)
    pad_start = pad_end - padded
    dest = pad_start[e_s] + (jnp.arange(A) - start[e_s])
    n_blocks = -(-(A + N_EXPERTS * (MOE_BLOCK - 1)) // MOE_BLOCK)
    P = n_blocks * MOE_BLOCK
    buf_tok = jnp.zeros((P,), jnp.int32).at[dest].set(tok_s.astype(jnp.int32))
    buf_gate = jnp.zeros((P,), h.dtype).at[dest].set(g_s.astype(h.dtype))
    blk_expert = jnp.minimum(
        jnp.searchsorted(pad_end, jnp.arange(n_blocks) * MOE_BLOCK, side='right'), N_EXPERTS - 1)
    xb = h[buf_tok].reshape(n_blocks, MOE_BLOCK, D)

    def expert_block(args):
        xblk, e = args
        return (jax.nn.silu(xblk @ w_gate[e]) * (xblk @ w_up[e])) @ w_down[e]

    yb = lax.map(expert_block, (xb, blk_expert)).reshape(P, D) * buf_gate[:, None]
    return jnp.zeros_like(h).at[buf_tok].add(yb)


def setup_inputs(seed: int = 0) -> dict:
    key = jax.random.key(seed)
    ks = iter(jax.random.split(key, 28))

    def nrm(shape, scale):
        return jax.random.normal(next(ks), shape, jnp.float32) * scale

    D, L = D_MODEL, DEPTH
    n_a, n_b, n_c, n_d = [len(range(m, DEPTH, N_MIXERS)) for m in range(N_MIXERS)]
    qkv_b = (GQA_HEADS + 2 * GQA_KV_HEADS) * GQA_HEAD_DIM
    qkv_d = (SWA_HEADS + 2 * SWA_KV_HEADS) * SWA_HEAD_DIM
    return {
        'x': nrm((BATCH, SEQ, D), 1.0),
        'c': nrm((BATCH, D), 1.0),
        'ctx': nrm((BATCH, CTX_LEN, D), 1.0),
        'c_ctx': nrm((D,), 1.0),
        'ada_w': nrm((L, D, 6 * D), 0.5 * D ** -0.5),
        'ada_b': nrm((L, 6 * D), 0.02),
        'norm_mix': 1.0 + nrm((L, D), 0.1),
        'norm_ffn': 1.0 + nrm((L, D), 0.1),
        'norm_final': 1.0 + nrm((D,), 0.1),
        'conv_in': nrm((n_a, D, 3 * D), D ** -0.5),
        'conv_k': nrm((n_a, CONV_WIDTH, D), CONV_WIDTH ** -0.5),
        'conv_out': nrm((n_a, D, D), D ** -0.5),
        'gqa_qkv': nrm((n_b, D, qkv_b), D ** -0.5),
        'gqa_q_gain': 1.0 + nrm((n_b, GQA_HEAD_DIM), 0.1),
        'gqa_k_gain': 1.0 + nrm((n_b, GQA_HEAD_DIM), 0.1),
        'gqa_out': nrm((n_b, GQA_HEADS * GQA_HEAD_DIM, D), D ** -0.5),
        'pool_w': nrm((n_c, len(POOL_WINDOWS), POOL_GROUP, POOL_GROUP), POOL_GROUP ** -0.5),
        'pool_scale': 1.0 + nrm((n_c, D), 0.1),
        'swa_qkv': nrm((n_d, D, qkv_d), D ** -0.5),
        'swa_sink': nrm((n_d, SWA_HEADS), 0.5),
        'swa_out': nrm((n_d, SWA_HEADS * SWA_HEAD_DIM, D), D ** -0.5),
        'router_grp_w': nrm((L, D, N_GROUPS), D ** -0.5),
        'router_grp_b': nrm((L, N_GROUPS), 0.01),
        'router_exp_w': nrm((L, D, N_EXPERTS), D ** -0.5),
        'router_exp_b': nrm((L, N_EXPERTS), 0.01),
        'exp_gate': nrm((L, N_EXPERTS, D, D_EXPERT), D ** -0.5),
        'exp_up': nrm((L, N_EXPERTS, D, D_EXPERT), D ** -0.5),
        'exp_down': nrm((L, N_EXPERTS, D_EXPERT, D), D_EXPERT ** -0.5),
    }


def reference(x, c, ctx, c_ctx, ada_w, ada_b, norm_mix, norm_ffn, norm_final,
              conv_in, conv_k, conv_out, gqa_qkv, gqa_q_gain, gqa_k_gain, gqa_out,
              pool_w, pool_scale, swa_qkv, swa_sink, swa_out,
              router_grp_w, router_grp_b, router_exp_w, router_exp_b,
              exp_gate, exp_up, exp_down):
    B, S, D = x.shape
    n_ctx = ctx.shape[1]
    rows = S // GRID_W
    cos_b, sin_b = axial_rope_tables(rows, GQA_HEAD_DIM)
    cos_d, sin_d = axial_rope_tables(rows, SWA_HEAD_DIM)
    h, hc = x, ctx
    for i in range(DEPTH):
        ctx_out = i < DEPTH - 1
        m, j = i % N_MIXERS, i // N_MIXERS
        mod = jax.nn.silu(c) @ ada_w[i] + ada_b[i]
        mod_c = jax.nn.silu(c_ctx) @ ada_w[i] + ada_b[i]
        sh1, sc1, g1, sh2, sc2, g2 = jnp.split(mod[:, None, :], 6, axis=-1)
        shc1, scc1, gc1, shc2, scc2, gc2 = jnp.split(mod_c, 6)
        u = rmsnorm(h, norm_mix[i]) * (1 + sc1) + sh1
        uc = rmsnorm(hc, norm_mix[i]) * (1 + scc1) + shc1
        if m == 0:
            y = short_gated_conv(u, conv_in[j], conv_k[j], conv_out[j])
            yc = short_gated_conv(uc, conv_in[j], conv_k[j], conv_out[j]) if ctx_out else None
        elif m == 1:
            y, yc = global_axial_gqa(u, uc, gqa_qkv[j], gqa_q_gain[j], gqa_k_gain[j], gqa_out[j],
                                     cos_b, sin_b, ctx_out)
        elif m == 2:
            y = multiscale_pool(u, pool_w[j], pool_scale[j])
            yc = multiscale_pool(uc, pool_w[j], pool_scale[j]) if ctx_out else None
        else:
            y, yc = windowed_sink_gqa(u, uc, swa_qkv[j], swa_sink[j], swa_out[j], cos_d, sin_d, ctx_out)
        h = h + g1 * y
        v_lat = rmsnorm(h, norm_ffn[i]) * (1 + sc2) + sh2
        moe_args = (router_grp_w[i], router_grp_b[i], router_exp_w[i], router_exp_b[i],
                    exp_gate[i], exp_up[i], exp_down[i])
        if ctx_out:
            hc = hc + gc1 * yc
            v_ctx = rmsnorm(hc, norm_ffn[i]) * (1 + scc2) + shc2
            tokens = jnp.concatenate([v_lat.reshape(B * S, D), v_ctx.reshape(B * n_ctx, D)], axis=0)
            f = hierarchical_moe(tokens, *moe_args)
            h = h + g2 * f[:B * S].reshape(B, S, D)
            hc = hc + gc2 * f[B * S:].reshape(B, n_ctx, D)
        else:
            f = hierarchical_moe(v_lat.reshape(B * S, D), *moe_args)
            h = h + g2 * f.reshape(B, S, D)
    return rmsnorm(h, norm_final)
```

```python
import functools

import jax
import jax.numpy as jnp
from jax import lax
from jax.experimental import pallas as pl
from jax.experimental.pallas import tpu as pltpu

GRID_W = 64
NORM_EPS = 1e-6
ROPE_THETA = 10000.0
NEG_INF = -1e30
GQA_HEADS, GQA_KV_HEADS = 16, 4
SWA_HEADS, SWA_KV_HEADS, SWA_WINDOW = 32, 8, 128
POOL_WINDOWS = (2, 4, 8, 16)
N_GROUPS, EXPERTS_PER_GROUP, TOP_K = 8, 4, 2
N_EXPERTS = N_GROUPS * EXPERTS_PER_GROUP
MOE_BLOCK = 128
LOG2E = 1.4426950408889634

LANES = 128
SUBLANES_F32 = 8
SUBLANES_BF16 = 16
VMEM_LIMIT = 56 * 1024 * 1024
TM = 256
HALO = SUBLANES_BF16


def _cparams(sem):
    return pltpu.CompilerParams(dimension_semantics=sem, vmem_limit_bytes=VMEM_LIMIT)


class Layout:
    def __init__(self, B, S, C, D):
        assert S % TM == 0 and C % TM == 0
        self.B, self.S, self.C, self.D = B, S, C, D
        self.SB = S + C
        self.T = B * self.SB
        self.LT = S // TM
        self.TPB = self.SB // TM
        self.NT = B * self.TPB

    def split(self, i):
        return i // self.TPB, i % self.TPB

    def mod_row(self, i):
        b, w = self.split(i)
        return jnp.where(w >= self.LT, self.B, b)


def _mod_spec(lay, chunk, tile_of):
    return pl.BlockSpec((None, None, 1, lay.D),
                        lambda *g: (chunk, lay.mod_row(tile_of(*g)), 0, 0))


def _row_spec(D):
    return pl.BlockSpec((1, D), lambda *g: (0, 0))


def _rms(x, gain):
    ms = jnp.mean(x * x, axis=-1, keepdims=True)
    return (x * lax.rsqrt(ms + NORM_EPS)) * gain


def _dot(a, b):
    return jnp.dot(a, b, preferred_element_type=jnp.float32)


def _mod_kernel(c_ref, w_ref, b_ref, o_ref):
    c = c_ref[...]
    a = (c * (1.0 / (1.0 + jnp.exp(-c)))).astype(jnp.bfloat16)
    o_ref[...] = _dot(a, w_ref[...].astype(jnp.bfloat16)) + b_ref[...]


def _modulation(cvec, ada_w, ada_b):
    L, D, N = ada_w.shape
    R = cvec.shape[0]
    tn = 512
    return pl.pallas_call(
        _mod_kernel,
        out_shape=jax.ShapeDtypeStruct((L, R, N), jnp.float32),
        grid=(L, N // tn),
        in_specs=[pl.BlockSpec((R, D), lambda l, j: (0, 0)),
                  pl.BlockSpec((None, D, tn), lambda l, j: (l, 0, j)),
                  pl.BlockSpec((None, 1, tn), lambda l, j: (l, 0, j))],
        out_specs=pl.BlockSpec((None, R, tn), lambda l, j: (l, 0, j)),
        compiler_params=_cparams(("arbitrary", "arbitrary")),
        name="adaln_mod",
    )(cvec, ada_w, ada_b.reshape(L, 1, N))


def _prenorm_kernel(h_ref, gain_ref, sc_ref, sh_ref, u_ref):
    u_ref[...] = (_rms(h_ref[...], gain_ref[...]) * (1.0 + sc_ref[...]) + sh_ref[...]).astype(u_ref.dtype)


def _prenorm(lay, h, gain, mod):
    D = lay.D
    tile = lambda i: i
    return pl.pallas_call(
        _prenorm_kernel,
        out_shape=jax.ShapeDtypeStruct((lay.T, D), jnp.bfloat16),
        grid=(lay.NT,),
        in_specs=[pl.BlockSpec((TM, D), lambda i: (i, 0)), _row_spec(D),
                  _mod_spec(lay, 1, tile), _mod_spec(lay, 0, tile)],
        out_specs=pl.BlockSpec((TM, D), lambda i: (i, 0)),
        compiler_params=_cparams(("arbitrary",)),
        name="prenorm",
    )(h, gain.reshape(1, D), mod, mod)


def _mixer_tail(y, h_ref, g1_ref, gain_ref, sc_ref, sh_ref, wr_ref, br_ref, h1_ref, v_ref, lg_ref):
    h1 = h_ref[...] + g1_ref[...] * y
    h1_ref[...] = h1
    v = _rms(h1, gain_ref[...]) * (1.0 + sc_ref[...]) + sh_ref[...]
    v_ref[...] = v
    lg_ref[...] = _dot(v.astype(jnp.bfloat16), wr_ref[...]) + br_ref[...]


def _tail_in_specs(lay, tile):
    D = lay.D
    return [pl.BlockSpec((TM, D), lambda *g: (tile(*g), 0)),
            _mod_spec(lay, 2, tile),
            _row_spec(D),
            _mod_spec(lay, 4, tile), _mod_spec(lay, 3, tile),
            pl.BlockSpec((D, LANES), lambda *g: (0, 0)),
            pl.BlockSpec((1, LANES), lambda *g: (0, 0))]


def _tail_out(lay, tile):
    D = lay.D
    shapes = (jax.ShapeDtypeStruct((lay.T, D), jnp.float32),
              jax.ShapeDtypeStruct((lay.T, D), jnp.float32),
              jax.ShapeDtypeStruct((lay.T, LANES), jnp.float32))
    specs = (pl.BlockSpec((TM, D), lambda *g: (tile(*g), 0)),
             pl.BlockSpec((TM, D), lambda *g: (tile(*g), 0)),
             pl.BlockSpec((TM, LANES), lambda *g: (tile(*g), 0)))
    return shapes, specs


def _seq_flags(lay, i):
    _, w = lay.split(i)
    has_prev = jnp.logical_and(w != 0, w != lay.LT)
    has_next = jnp.logical_and(w != lay.LT - 1, w != lay.TPB - 1)
    return has_prev, has_next


def _halo_specs(lay, width, col=lambda *g: 0, tile=lambda i: i):
    per = TM // HALO
    last = lay.T // HALO - 1
    prev = pl.BlockSpec((HALO, width), lambda *g: (jnp.maximum(tile(*g) * per - 1, 0), col(*g)))
    nxt = pl.BlockSpec((HALO, width), lambda *g: (jnp.minimum((tile(*g) + 1) * per, last), col(*g)))
    return prev, nxt


def _convin_kernel(u_ref, wb_ref, wc_ref, wx_ref, bg_ref, z_ref):
    x = u_ref[...]
    bg_ref[...] = _dot(x, wb_ref[...]).astype(bg_ref.dtype)
    z_ref[...] = (_dot(x, wc_ref[...]) * _dot(x, wx_ref[...])).astype(z_ref.dtype)


def _conv_in(lay, u, w_in):
    D = lay.D
    tn = 512
    nb = D // tn
    out = jax.ShapeDtypeStruct((lay.T, D), jnp.bfloat16)
    return pl.pallas_call(
        _convin_kernel,
        out_shape=(out, out),
        grid=(nb, lay.NT),
        in_specs=[pl.BlockSpec((TM, D), lambda j, i: (i, 0)),
                  pl.BlockSpec((D, tn), lambda j, i: (0, j)),
                  pl.BlockSpec((D, tn), lambda j, i: (0, nb + j)),
                  pl.BlockSpec((D, tn), lambda j, i: (0, 2 * nb + j))],
        out_specs=(pl.BlockSpec((TM, tn), lambda j, i: (i, j)),
                   pl.BlockSpec((TM, tn), lambda j, i: (i, j))),
        compiler_params=_cparams(("arbitrary", "arbitrary")),
        name="conv_in",
    )(u, w_in, w_in, w_in)


def _convout_kernel(lay, z_ref, zp_ref, zn_ref, bg_ref, ck_ref, wo_ref, *rest):
    i = pl.program_id(0)
    has_prev, has_next = _seq_flags(lay, i)
    z = z_ref[...].astype(jnp.float32)
    row = lax.broadcasted_iota(jnp.int32, z.shape, 0)
    prev_row = jnp.where(has_prev, zp_ref[HALO - 1:HALO, :].astype(jnp.float32), 0.0)
    next_row = jnp.where(has_next, zn_ref[0:1, :].astype(jnp.float32), 0.0)
    z_m1 = jnp.where(row == 0, prev_row, pltpu.roll(z, 1, 0))
    z_p1 = jnp.where(row == TM - 1, next_row, pltpu.roll(z, TM - 1, 0))
    y = z_m1 * ck_ref[0:1, :] + z * ck_ref[1:2, :] + z_p1 * ck_ref[2:3, :]
    g = (bg_ref[...].astype(jnp.float32) * y).astype(jnp.bfloat16)
    _mixer_tail(_dot(g, wo_ref[...]), *rest)


def _conv_out(lay, z, bg, conv_k, w_out, h, mod, gain, wr, br):
    D = lay.D
    tile = lambda i: i
    prev, nxt = _halo_specs(lay, D)
    shapes, specs = _tail_out(lay, tile)
    return pl.pallas_call(
        functools.partial(_convout_kernel, lay),
        out_shape=shapes,
        grid=(lay.NT,),
        in_specs=[pl.BlockSpec((TM, D), lambda i: (i, 0)), prev, nxt,
                  pl.BlockSpec((TM, D), lambda i: (i, 0)),
                  pl.BlockSpec(conv_k.shape, lambda i: (0, 0)),
                  pl.BlockSpec((D, D), lambda i: (0, 0))] + _tail_in_specs(lay, tile),
        out_specs=specs,
        compiler_params=_cparams(("arbitrary",)),
        name="conv_out",
    )(z, z, z, bg, conv_k, w_out, h, mod, gain.reshape(1, D), mod, mod, wr, br)


def _proj_kernel(head_dim, use_norm, use_rope, scale, u_ref, w_ref, gain_ref, cos_ref, sin_ref, o_ref):
    y = _dot(u_ref[...], w_ref[...])
    tn = y.shape[1]
    lane = lax.broadcasted_iota(jnp.int32, (TM, LANES), 1)
    for g in range(tn // LANES):
        yg = y[:, g * LANES:(g + 1) * LANES]
        if use_norm:
            yg = _rms(yg, gain_ref[...])
        if use_rope:
            if head_dim == LANES:
                rot = pltpu.roll(yg, LANES // 2, 1)
            else:
                q = head_dim // 2
                rot = jnp.where(lane % head_dim < q, pltpu.roll(yg, LANES - q, 1), pltpu.roll(yg, q, 1))
            yg = yg * cos_ref[...] + rot * sin_ref[...]
        if scale != 1.0:
            yg = yg * scale
        o_ref[:, g * LANES:(g + 1) * LANES] = yg.astype(o_ref.dtype)


def _project(lay, u, w, col0, ncols, head_dim, gain, cos_t, sin_t, use_norm, use_rope, scale, name):
    D = lay.D
    tn = 512
    nb = ncols // tn
    cb0 = col0 // tn
    kern = functools.partial(_proj_kernel, head_dim, use_norm, use_rope, scale)
    return pl.pallas_call(
        kern,
        out_shape=jax.ShapeDtypeStruct((lay.T, ncols), jnp.bfloat16),
        grid=(nb, lay.NT),
        in_specs=[pl.BlockSpec((TM, D), lambda j, i: (i, 0)),
                  pl.BlockSpec((D, tn), lambda j, i: (0, cb0 + j)),
                  pl.BlockSpec((1, LANES), lambda j, i: (0, 0)),
                  pl.BlockSpec((TM, LANES), lambda j, i: (i % lay.TPB, 0)),
                  pl.BlockSpec((TM, LANES), lambda j, i: (i % lay.TPB, 0))],
        out_specs=pl.BlockSpec((TM, tn), lambda j, i: (i, j)),
        compiler_params=_cparams(("arbitrary", "arbitrary")),
        name=name,
    )(u, w, gain, cos_t, sin_t)


def _rope_tables(lay, head_dim):
    quarter = head_dim // 4
    rows = lay.S // GRID_W
    row = jnp.repeat(jnp.arange(rows), GRID_W).astype(jnp.float32)
    col = jnp.tile(jnp.arange(GRID_W), rows).astype(jnp.float32)
    inv = ROPE_THETA ** (-jnp.arange(quarter, dtype=jnp.float32) / quarter)
    ang = jnp.concatenate([row[:, None] * inv, col[:, None] * inv], axis=-1)
    cos, sin = jnp.cos(ang), jnp.sin(ang)
    reps = LANES // head_dim
    cos_t = jnp.tile(jnp.concatenate([cos, cos], axis=-1), (1, reps))
    sin_t = jnp.tile(jnp.concatenate([-sin, sin], axis=-1), (1, reps))
    cos_t = jnp.concatenate([cos_t, jnp.ones((lay.C, LANES), jnp.float32)], axis=0)
    sin_t = jnp.concatenate([sin_t, jnp.zeros((lay.C, LANES), jnp.float32)], axis=0)
    return cos_t, sin_t


def _flash_kernel(n_rep, q_ref, k_ref, v_ref, o_ref, m_ref, l_ref, acc_ref):
    kv = pl.program_id(3)
    hd = LANES

    @pl.when(kv == 0)
    def _():
        m_ref[...] = jnp.full(m_ref.shape, NEG_INF, jnp.float32)
        l_ref[...] = jnp.zeros(l_ref.shape, jnp.float32)
        acc_ref[...] = jnp.zeros(acc_ref.shape, jnp.float32)

    k = k_ref[...]
    v = v_ref[...]
    tk = k.shape[0]
    for h in range(n_rep):
        q = q_ref[:, h * hd:(h + 1) * hd]
        s = lax.dot_general(q, k, (((1,), (1,)), ((), ())), preferred_element_type=jnp.float32)
        m_prev = m_ref[h]
        m_next = jnp.maximum(m_prev, jnp.max(s, axis=1, keepdims=True))
        alpha = jnp.exp2(m_prev - m_next)
        p = jnp.exp2(s - jnp.tile(m_next, (1, tk // LANES)))
        l_ref[h] = alpha * l_ref[h] + jnp.sum(p, axis=1, keepdims=True)
        m_ref[h] = m_next
        acc_ref[h] = acc_ref[h] * alpha + _dot(p.astype(jnp.bfloat16), v)

    @pl.when(kv == pl.num_programs(3) - 1)
    def _():
        for h in range(n_rep):
            o_ref[:, h * hd:(h + 1) * hd] = (acc_ref[h] / l_ref[h]).astype(o_ref.dtype)


def _flash(lay, q, k, v, n_kv, n_rep, tq, tk, q_blk0, nq, k_blk0, nk, name):
    B = lay.B
    qw = n_rep * LANES
    return pl.pallas_call(
        functools.partial(_flash_kernel, n_rep),
        out_shape=jax.ShapeDtypeStruct((B, nq * tq, n_kv * qw), jnp.bfloat16),
        grid=(B, n_kv, nq, nk),
        in_specs=[pl.BlockSpec((None, tq, qw), lambda b, g, i, j: (b, q_blk0 + i, g)),
                  pl.BlockSpec((None, tk, LANES), lambda b, g, i, j: (b, k_blk0 + j, g)),
                  pl.BlockSpec((None, tk, LANES), lambda b, g, i, j: (b, k_blk0 + j, g))],
        out_specs=pl.BlockSpec((None, tq, qw), lambda b, g, i, j: (b, i, g)),
        scratch_shapes=[pltpu.VMEM((n_rep, tq, LANES), jnp.float32),
                        pltpu.VMEM((n_rep, tq, LANES), jnp.float32),
                        pltpu.VMEM((n_rep, tq, LANES), jnp.float32)],
        compiler_params=_cparams(("arbitrary", "arbitrary", "arbitrary", "arbitrary")),
        name=name,
    )(q, k, v)


def _swa_kernel(lay, tq, sink_ref, q_ref, kp_ref, kc_ref, kn_ref, kx_ref, vp_ref, vc_ref, vn_ref, vx_ref, o_ref):
    pair = pl.program_id(1)
    qb = pl.program_id(2)
    hd = LANES // 2
    n_rep = SWA_HEADS // SWA_KV_HEADS
    W = SWA_WINDOW
    kk = jnp.concatenate([kp_ref[...], kc_ref[...], kn_ref[...], kx_ref[...]], axis=0)
    vv = jnp.concatenate([vp_ref[...], vc_ref[...], vn_ref[...], vx_ref[...]], axis=0)
    nkeys = kk.shape[0]
    nwin = tq + 2 * W
    lane = lax.broadcasted_iota(jnp.int32, (nkeys, LANES), 1)
    low = lane < hd
    kk_sw = pltpu.roll(kk.astype(jnp.float32), hd, 1).astype(kk.dtype)
    vv_sw = pltpu.roll(vv.astype(jnp.float32), hd, 1).astype(vv.dtype)
    k_dup = (jnp.where(low, kk, kk_sw), jnp.where(low, kk_sw, kk))
    v_dup = (jnp.where(low, vv, vv_sw), jnp.where(low, vv_sw, vv))

    start = qb * tq
    qpos = start + lax.broadcasted_iota(jnp.int32, (tq, nkeys), 0)
    col = lax.broadcasted_iota(jnp.int32, (tq, nkeys), 1)
    kpos = start - W + col
    in_win = jnp.logical_and(jnp.abs(qpos - kpos) <= W, jnp.logical_and(kpos >= 0, kpos < lay.S))
    valid = jnp.logical_or(col >= nwin, in_win)

    qlane = lax.broadcasted_iota(jnp.int32, (tq, LANES), 1)
    qlow = qlane < hd
    for g in range(2 * n_rep * hd // LANES):
        qg = q_ref[:, g * LANES:(g + 1) * LANES]
        kvh = (2 * g) // n_rep
        out = None
        for half in range(2):
            head = pair * 2 * n_rep + 2 * g + half
            qm = jnp.where(qlow if half == 0 else jnp.logical_not(qlow), qg, jnp.zeros_like(qg))
            s = lax.dot_general(qm, k_dup[kvh], (((1,), (1,)), ((), ())), preferred_element_type=jnp.float32)
            s = jnp.where(valid, s, NEG_INF)
            sink = sink_ref[head]
            m = jnp.maximum(jnp.max(s, axis=1, keepdims=True), sink)
            p = jnp.exp2(s - m)
            l = jnp.sum(p, axis=1, keepdims=True) + jnp.exp2(sink - m)
            o = _dot(p.astype(jnp.bfloat16), v_dup[kvh]) / l
            out = o if half == 0 else jnp.where(qlow, out, o)
        o_ref[:, g * LANES:(g + 1) * LANES] = out.astype(o_ref.dtype)


def _swa(lay, q, k, v, sink2):
    B, S = lay.B, lay.S
    tq = TM
    W = SWA_WINDOW
    per = tq // W
    nq = S // tq
    qw = 2 * (SWA_HEADS // SWA_KV_HEADS) * (LANES // 2)
    ctx_blk = S // lay.C
    kv_specs = [pl.BlockSpec((None, W, LANES), lambda b, p, i: (b, jnp.maximum(i * per - 1, 0), p)),
                pl.BlockSpec((None, tq, LANES), lambda b, p, i: (b, i, p)),
                pl.BlockSpec((None, W, LANES), lambda b, p, i: (b, (i + 1) * per, p)),
                pl.BlockSpec((None, lay.C, LANES), lambda b, p, i: (b, ctx_blk, p))]
    return pl.pallas_call(
        functools.partial(_swa_kernel, lay, tq),
        out_shape=jax.ShapeDtypeStruct((B, S, lay.D), jnp.bfloat16),
        grid=(B, SWA_KV_HEADS // 2, nq),
        in_specs=[pl.BlockSpec(memory_space=pltpu.SMEM),
                  pl.BlockSpec((None, tq, qw), lambda b, p, i: (b, i, p))] + kv_specs + kv_specs,
        out_specs=pl.BlockSpec((None, tq, qw), lambda b, p, i: (b, i, p)),
        compiler_params=_cparams(("arbitrary", "arbitrary", "arbitrary")),
        name="swa",
    )(sink2, q, k, k, k, k, v, v, v, v)


def _outproj_kernel(lay, has_ctx, ol_ref, oc_ref, wo_ref, *rest):
    x = ol_ref[...]
    if has_ctx:
        _, w = lay.split(pl.program_id(0))
        x = jnp.where(w >= lay.LT, oc_ref[...], x)
    _mixer_tail(_dot(x, wo_ref[...]), *rest)


def _out_proj(lay, o_lat, o_ctx, w_out, h, mod, gain, wr, br):
    D = lay.D
    tile = lambda i: i
    has_ctx = o_ctx is not None
    if not has_ctx:
        o_ctx = o_lat
    shapes, specs = _tail_out(lay, tile)

    def lat_idx(i):
        b, w = lay.split(i)
        return b, jnp.minimum(w, lay.LT - 1), 0

    def ctx_idx(i):
        b, w = lay.split(i)
        return b, jnp.clip(w - lay.LT, 0, lay.TPB - lay.LT - 1), 0

    return pl.pallas_call(
        functools.partial(_outproj_kernel, lay, has_ctx),
        out_shape=shapes,
        grid=(lay.NT,),
        in_specs=[pl.BlockSpec((None, TM, D), lat_idx),
                  pl.BlockSpec((None, TM, D), ctx_idx if has_ctx else lat_idx),
                  pl.BlockSpec((D, D), lambda i: (0, 0))] + _tail_in_specs(lay, tile),
        out_specs=specs,
        compiler_params=_cparams(("arbitrary",)),
        name="attn_out",
    )(o_lat, o_ctx, w_out, h, mod, gain.reshape(1, D), mod, mod, wr, br)


def _pool_kernel(lay, u_ref, up_ref, un_ref, pw_ref, ps_ref, *rest):
    i = pl.program_id(0)
    _, w = lay.split(i)
    has_prev, has_next = _seq_flags(lay, i)
    in_ctx = w >= lay.LT
    seq_len = jnp.where(in_ctx, lay.C, lay.S)
    pos0 = jnp.where(in_ctx, w - lay.LT, w) * TM
    G = len(POOL_WINDOWS)
    gw = lay.D // G
    E_ROWS = TM + 2 * SUBLANES_F32
    pos = pos0 + lax.broadcasted_iota(jnp.int32, (TM, gw), 0)
    ys = []
    for g, win in enumerate(POOL_WINDOWS):
        sl = slice(g * gw, (g + 1) * gw)
        u = u_ref[:, sl].astype(jnp.float32)
        before = jnp.where(has_prev, up_ref[HALO - SUBLANES_F32:HALO, sl].astype(jnp.float32), 0.0)
        after = jnp.where(has_next, un_ref[0:SUBLANES_F32, sl].astype(jnp.float32), 0.0)
        e = jnp.concatenate([before, u, after], axis=0)
        left = win // 2
        right = win - 1 - left
        assert left == right + 1 and left & (left - 1) == 0
        acc = e
        span = 1
        while span < left:
            acc = acc + pltpu.roll(acc, E_ROWS - span, 0)
            span *= 2
        tot = pltpu.roll(acc, left, 0) + acc
        total = tot[SUBLANES_F32:SUBLANES_F32 + TM]
        cnt = jnp.minimum(pos + right, seq_len - 1) - jnp.maximum(pos - left, 0) + 1
        mean = total / cnt.astype(jnp.float32)
        ys.append(_dot((mean - u).astype(jnp.bfloat16), pw_ref[g]))
    y = jnp.concatenate(ys, axis=1) * ps_ref[...]
    _mixer_tail(y, *rest)


def _pool(lay, u, pool_w, pool_scale, h, mod, gain, wr, br):
    D = lay.D
    tile = lambda i: i
    prev, nxt = _halo_specs(lay, D)
    shapes, specs = _tail_out(lay, tile)
    return pl.pallas_call(
        functools.partial(_pool_kernel, lay),
        out_shape=shapes,
        grid=(lay.NT,),
        in_specs=[pl.BlockSpec((TM, D), lambda i: (i, 0)), prev, nxt,
                  pl.BlockSpec(pool_w.shape, lambda i: (0, 0, 0)),
                  _row_spec(D)] + _tail_in_specs(lay, tile),
        out_specs=specs,
        compiler_params=_cparams(("arbitrary",)),
        name="pool",
    )(u, u, u, pool_w, pool_scale.reshape(1, D), h, mod, gain.reshape(1, D), mod, mod, wr, br)


def _route_kernel(lg_ref, info_ref, gate_ref, cnt_ref, carry_ref):
    i = pl.program_id(0)

    @pl.when(i == 0)
    def _():
        carry_ref[...] = jnp.zeros(carry_ref.shape, jnp.float32)

    lg = lg_ref[...]
    lane = lax.broadcasted_iota(jnp.int32, lg.shape, 1)
    lane_f = lane.astype(jnp.float32)
    big = jnp.float32(4 * LANES)

    def first_lane(mask):
        return jnp.min(jnp.where(mask, lane_f, big), axis=1, keepdims=True).astype(jnp.int32)

    is_grp = lane < N_GROUPS
    gl = jnp.where(is_grp, lg, NEG_INF)
    gmax = jnp.max(gl, axis=1, keepdims=True)
    grp = first_lane(jnp.logical_and(is_grp, gl == gmax))
    p_grp = 1.0 / jnp.sum(jnp.where(is_grp, jnp.exp(gl - gmax), 0.0), axis=1, keepdims=True)
    eid = lane - N_GROUPS
    in_grp = jnp.logical_and(lane >= N_GROUPS + grp * EXPERTS_PER_GROUP,
                             lane < N_GROUPS + (grp + 1) * EXPERTS_PER_GROUP)
    el = jnp.where(in_grp, lg, NEG_INF)
    t1 = jnp.max(el, axis=1, keepdims=True)
    e1 = first_lane(jnp.logical_and(in_grp, el == t1)) - N_GROUPS
    rest = jnp.logical_and(in_grp, eid != e1)
    el2 = jnp.where(rest, lg, NEG_INF)
    t2 = jnp.max(el2, axis=1, keepdims=True)
    e2 = first_lane(jnp.logical_and(rest, el2 == t2)) - N_GROUPS
    d = jnp.exp(t2 - t1)
    g1 = p_grp / (1.0 + d)
    g2 = p_grp * d / (1.0 + d)

    oh1 = lane == e1
    oh2 = lane == e2
    oh = jnp.where(jnp.logical_or(oh1, oh2), 1.0, 0.0)
    r = lax.broadcasted_iota(jnp.int32, (TM, TM), 0)
    c = lax.broadcasted_iota(jnp.int32, (TM, TM), 1)
    tri = jnp.where(c < r, 1.0, 0.0).astype(jnp.bfloat16)
    before = _dot(tri, oh.astype(jnp.bfloat16)) + carry_ref[0:1, :]
    r1 = jnp.sum(jnp.where(oh1, before, 0.0), axis=1, keepdims=True)
    r2 = jnp.sum(jnp.where(oh2, before, 0.0), axis=1, keepdims=True)
    carry = carry_ref[0:1, :] + jnp.sum(oh, axis=0, keepdims=True)
    carry_ref[...] = jnp.broadcast_to(carry, carry_ref.shape)

    info = jnp.where(lane == 0, e1, jnp.where(lane == 1, e2, jnp.where(
        lane == 2, r1.astype(jnp.int32), jnp.where(lane == 3, r2.astype(jnp.int32), 0))))
    info_ref[...] = info
    gate_ref[...] = jnp.where(lane == 0, g1, jnp.where(lane == 1, g2, 0.0))
    cnt_ref[...] = jnp.broadcast_to(carry, cnt_ref.shape).astype(jnp.int32)


def _route(lay, logits):
    T = lay.T
    return pl.pallas_call(
        _route_kernel,
        out_shape=(jax.ShapeDtypeStruct((T, LANES), jnp.int32),
                   jax.ShapeDtypeStruct((T, LANES), jnp.float32),
                   jax.ShapeDtypeStruct((SUBLANES_F32, LANES), jnp.int32)),
        grid=(lay.NT,),
        in_specs=[pl.BlockSpec((TM, LANES), lambda i: (i, 0))],
        out_specs=(pl.BlockSpec((TM, LANES), lambda i: (i, 0)),
                   pl.BlockSpec((TM, LANES), lambda i: (i, 0)),
                   pl.BlockSpec((SUBLANES_F32, LANES), lambda i: (0, 0))),
        scratch_shapes=[pltpu.VMEM((SUBLANES_F32, LANES), jnp.float32)],
        compiler_params=_cparams(("arbitrary",)),
        name="moe_route",
    )(logits)


def _row_copy(src, src_row, dst, dst_row, sem):
    return pltpu.make_async_copy(src.at[pl.ds(src_row, 1)], dst.at[pl.ds(dst_row, 1)], sem)


def _dispatch_kernel(dest_ref, v_hbm, xs_in, xs_hbm, sem):
    del xs_in
    base = pl.program_id(0) * TM

    def issue(r, _):
        for k in range(TOP_K):
            _row_copy(v_hbm, base + r, xs_hbm, dest_ref[0, k, r], sem).start()
        return 0

    lax.fori_loop(0, TM, issue, 0)
    pltpu.make_async_copy(v_hbm.at[pl.ds(0, TOP_K * TM)], xs_hbm.at[pl.ds(0, TOP_K * TM)], sem).wait()


def _dispatch(lay, v, dest, P):
    D = lay.D
    xs0 = jnp.zeros((P, D), jnp.float32)
    return pl.pallas_call(
        _dispatch_kernel,
        out_shape=jax.ShapeDtypeStruct((P, D), jnp.float32),
        grid=(lay.NT,),
        in_specs=[pl.BlockSpec((1, TOP_K, TM), lambda i: (i, 0, 0), memory_space=pltpu.SMEM),
                  pl.BlockSpec(memory_space=pl.ANY),
                  pl.BlockSpec(memory_space=pl.ANY)],
        out_specs=pl.BlockSpec(memory_space=pl.ANY),
        scratch_shapes=[pltpu.SemaphoreType.DMA(())],
        input_output_aliases={2: 0},
        compiler_params=pltpu.CompilerParams(dimension_semantics=("arbitrary",), has_side_effects=True),
        name="moe_dispatch",
    )(dest, v, xs0)


def _expert_kernel(be_ref, na_ref, x_ref, wg_ref, wu_ref, wd_ref, y_ref):
    del be_ref

    @pl.when(pl.program_id(0) < na_ref[0])
    def _():
        x = x_ref[...].astype(jnp.bfloat16)
        hg = _dot(x, wg_ref[...])
        hu = _dot(x, wu_ref[...])
        a = (hg * (1.0 / (1.0 + jnp.exp(-hg)))) * hu
        y_ref[...] = _dot(a.astype(jnp.bfloat16), wd_ref[...])

    @pl.when(pl.program_id(0) >= na_ref[0])
    def _():
        y_ref[...] = jnp.zeros(y_ref.shape, y_ref.dtype)


def _experts(lay, xs, blk_expert, n_active, wg, wu, wd):
    D = lay.D
    P = xs.shape[0]
    DE = wg.shape[2]
    grid_spec = pltpu.PrefetchScalarGridSpec(
        num_scalar_prefetch=2,
        grid=(P // MOE_BLOCK,),
        in_specs=[pl.BlockSpec((MOE_BLOCK, D), lambda i, be, na: (i, 0)),
                  pl.BlockSpec((None, D, DE), lambda i, be, na: (be[i], 0, 0)),
                  pl.BlockSpec((None, D, DE), lambda i, be, na: (be[i], 0, 0)),
                  pl.BlockSpec((None, DE, D), lambda i, be, na: (be[i], 0, 0))],
        out_specs=pl.BlockSpec((MOE_BLOCK, D), lambda i, be, na: (i, 0)),
    )
    return pl.pallas_call(
        _expert_kernel,
        out_shape=jax.ShapeDtypeStruct((P, D), jnp.float32),
        grid_spec=grid_spec,
        compiler_params=_cparams(("arbitrary",)),
        name="moe_experts",
    )(blk_expert, n_active, xs, wg, wu, wd)


def _combine_kernel(final, dest_ref, ys_hbm, gate_ref, h_ref, g2_ref, gain_ref, sc_ref, sh_ref, *refs):
    *out_refs, ybuf, sem = refs

    def issue(r, _):
        for k in range(TOP_K):
            pltpu.make_async_copy(ys_hbm.at[pl.ds(dest_ref[0, k, r], 1)], ybuf.at[k, pl.ds(r, 1)], sem).start()
        return 0

    lax.fori_loop(0, TM, issue, 0)
    for k in range(TOP_K):
        pltpu.make_async_copy(ys_hbm.at[pl.ds(0, TM)], ybuf.at[k], sem).wait()
    gate = gate_ref[...]
    f = gate[:, 0:1] * ybuf[0] + gate[:, 1:2] * ybuf[1]
    h2 = h_ref[...] + g2_ref[...] * f
    if final:
        (out_ref,) = out_refs
        out_ref[...] = _rms(h2, gain_ref[...])
    else:
        h2_ref, u_ref = out_refs
        h2_ref[...] = h2
        u_ref[...] = (_rms(h2, gain_ref[...]) * (1.0 + sc_ref[...]) + sh_ref[...]).astype(u_ref.dtype)


def _combine(lay, ys, dest, gates, h1, mod, mod_next, gain_next, final):
    D = lay.D
    row_tile = pl.BlockSpec((TM, D), lambda i: (i, 0))
    if final:
        n = lay.B * lay.LT
        tile = lambda i: (i // lay.LT) * lay.TPB + i % lay.LT
        out_shape = (jax.ShapeDtypeStruct((lay.B * lay.S, D), jnp.float32),)
        out_specs = (row_tile,)
    else:
        n = lay.NT
        tile = lambda i: i
        out_shape = (jax.ShapeDtypeStruct((lay.T, D), jnp.float32),
                     jax.ShapeDtypeStruct((lay.T, D), jnp.bfloat16))
        out_specs = (row_tile, row_tile)
    return pl.pallas_call(
        functools.partial(_combine_kernel, final),
        out_shape=out_shape,
        grid=(n,),
        in_specs=[pl.BlockSpec((1, TOP_K, TM), lambda i: (tile(i), 0, 0), memory_space=pltpu.SMEM),
                  pl.BlockSpec(memory_space=pl.ANY),
                  pl.BlockSpec((TM, LANES), lambda i: (tile(i), 0)),
                  pl.BlockSpec((TM, D), lambda i: (tile(i), 0)),
                  _mod_spec(lay, 5, tile),
                  _row_spec(D),
                  _mod_spec(lay, 1, tile), _mod_spec(lay, 0, tile)],
        out_specs=out_specs,
        scratch_shapes=[pltpu.VMEM((TOP_K, TM, D), jnp.float32), pltpu.SemaphoreType.DMA(())],
        compiler_params=_cparams(("arbitrary",)),
        name="moe_combine_final" if final else "moe_combine",
    )(dest, ys, gates, h1, mod, gain_next.reshape(1, D), mod_next, mod_next)


def _moe(lay, v, logits, h1, mod, mod_next, gain_next, wg, wu, wd, final):
    T, NT = lay.T, lay.NT
    info, gates, cnt = _route(lay, logits)
    counts = cnt[0, :N_EXPERTS]
    padded = ((counts + MOE_BLOCK - 1) // MOE_BLOCK) * MOE_BLOCK
    pad_end = jnp.cumsum(padded)
    pad_start = pad_end - padded
    expert = info[:, :TOP_K]
    dest = info[:, TOP_K:2 * TOP_K] + jnp.take(pad_start, expert)
    dest = dest.reshape(NT, TM, TOP_K).transpose(0, 2, 1).astype(jnp.int32)
    n_blocks = -(-(T * TOP_K + N_EXPERTS * (MOE_BLOCK - 1)) // MOE_BLOCK)
    blk_expert = jnp.minimum(
        jnp.searchsorted(pad_end, jnp.arange(n_blocks, dtype=jnp.int32) * MOE_BLOCK, side='right'),
        N_EXPERTS - 1).astype(jnp.int32)
    n_active = (pad_end[-1:] // MOE_BLOCK).astype(jnp.int32)
    xs = _dispatch(lay, v, dest, n_blocks * MOE_BLOCK)
    ys = _experts(lay, xs, blk_expert, n_active, wg, wu, wd)
    return _combine(lay, ys, dest, gates, h1, mod, mod_next, gain_next, final)


def kernel(x, c, ctx, c_ctx, ada_w, ada_b, norm_mix, norm_ffn, norm_final, conv_in, conv_k, conv_out, gqa_qkv, gqa_q_gain, gqa_k_gain, gqa_out, pool_w, pool_scale, swa_qkv, swa_sink, swa_out, router_grp_w, router_grp_b, router_exp_w, router_exp_b, exp_gate, exp_up, exp_down):
    B, S, D = x.shape
    C = ctx.shape[1]
    L = ada_w.shape[0]
    lay = Layout(B, S, C, D)
    bf = jnp.bfloat16

    R = SUBLANES_BF16
    cvec = jnp.zeros((R, D), jnp.float32).at[:B].set(c).at[B].set(c_ctx)
    mod_all = _modulation(cvec, ada_w, ada_b)[:, :B + 1].reshape(L, B + 1, 6, 1, D).transpose(0, 2, 1, 3, 4)

    pad = LANES - N_GROUPS - N_EXPERTS
    wr_all = jnp.concatenate([router_grp_w, router_exp_w, jnp.zeros((L, D, pad), jnp.float32)], axis=-1).astype(bf)
    br_all = jnp.concatenate([router_grp_b, router_exp_b, jnp.zeros((L, pad), jnp.float32)], axis=-1)

    h = jnp.concatenate([x, ctx], axis=1).reshape(lay.T, D)
    u = _prenorm(lay, h, norm_mix[0], mod_all[0])
    one = jnp.ones((1, LANES), jnp.float32)

    for i in range(L):
        m, j = i % 4, i // 4
        mod = mod_all[i]
        wr, br = wr_all[i], br_all[i].reshape(1, LANES)
        tail = (h, mod, norm_ffn[i], wr, br)
        if m == 0:
            bg, z = _conv_in(lay, u, conv_in[j].astype(bf))
            h1, v, lg = _conv_out(lay, z, bg, conv_k[j], conv_out[j].astype(bf), *tail)
        elif m == 1:
            hd = D // GQA_HEADS
            w = gqa_qkv[j].astype(bf)
            cos_t, sin_t = _rope_tables(lay, hd)
            nq, nk = GQA_HEADS * hd, GQA_KV_HEADS * hd
            qg = gqa_q_gain[j].reshape(1, hd)
            kg = gqa_k_gain[j].reshape(1, hd)
            q = _project(lay, u, w, 0, nq, hd, qg, cos_t, sin_t, True, True, hd ** -0.5 * LOG2E, "gqa_q")
            k = _project(lay, u, w, nq, nk, hd, kg, cos_t, sin_t, True, True, 1.0, "gqa_k")
            vv = _project(lay, u, w, nq + nk, nk, hd, one, cos_t, sin_t, False, False, 1.0, "gqa_v")
            q3, k3, v3 = (a.reshape(B, lay.SB, -1) for a in (q, k, vv))
            n_rep = GQA_HEADS // GQA_KV_HEADS
            tq = 512 if S % 512 == 0 else TM
            tk = 768 if lay.SB % 768 == 0 else TM
            o_lat = _flash(lay, q3, k3, v3, GQA_KV_HEADS, n_rep, tq, tk, 0, S // tq, 0, lay.SB // tk, "gqa_flash")
            o_ctx = _flash(lay, q3, k3, v3, GQA_KV_HEADS, n_rep, C, C, S // C, 1, S // C, 1, "gqa_flash_ctx")
            h1, v, lg = _out_proj(lay, o_lat, o_ctx, gqa_out[j].astype(bf), *tail)
        elif m == 2:
            h1, v, lg = _pool(lay, u, pool_w[j].astype(bf), pool_scale[j], *tail)
        else:
            hd = D // SWA_HEADS
            w = swa_qkv[j].astype(bf)
            cos_t, sin_t = _rope_tables(lay, hd)
            nq, nk = SWA_HEADS * hd, SWA_KV_HEADS * hd
            q = _project(lay, u, w, 0, nq, hd, one, cos_t, sin_t, False, True, hd ** -0.5 * LOG2E, "swa_q")
            k = _project(lay, u, w, nq, nk, hd, one, cos_t, sin_t, False, True, 1.0, "swa_k")
            vv = _project(lay, u, w, nq + nk, nk, hd, one, cos_t, sin_t, False, False, 1.0, "swa_v")
            q3, k3, v3 = (a.reshape(B, lay.SB, -1) for a in (q, k, vv))
            o_lat = _swa(lay, q3, k3, v3, swa_sink[j] * LOG2E)
            h1, v, lg = _out_proj(lay, o_lat, None, swa_out[j].astype(bf), *tail)
        final = i == L - 1
        mod_next = mod if final else mod_all[i + 1]
        gain_next = norm_final if final else norm_mix[i + 1]
        res = _moe(lay, v, lg, h1, mod, mod_next, gain_next,
                   exp_gate[i].astype(bf), exp_up[i].astype(bf), exp_down[i].astype(bf), final)
        if final:
            return res[0].reshape(B, S, D)
        h, u = res
```

```python
import functools

import jax
import jax.numpy as jnp
from jax import lax
from jax.experimental import pallas as pl
from jax.experimental.pallas import tpu as pltpu

GRID_W = 64
NORM_EPS = 1e-6
ROPE_THETA = 10000.0
NEG_INF = -1e30
GQA_HEADS, GQA_KV_HEADS = 16, 4
SWA_HEADS, SWA_KV_HEADS, SWA_WINDOW = 32, 8, 128
POOL_WINDOWS = (2, 4, 8, 16)
N_GROUPS, EXPERTS_PER_GROUP, TOP_K = 8, 4, 2
N_EXPERTS = N_GROUPS * EXPERTS_PER_GROUP
MOE_BLOCK = 128
LOG2E = 1.4426950408889634

LANES = 128
SUBLANES_F32 = 8
SUBLANES_BF16 = 16
VMEM_LIMIT = 56 * 1024 * 1024
TM = 256
HALO = SUBLANES_BF16


def _cparams(sem):
    return pltpu.CompilerParams(dimension_semantics=sem, vmem_limit_bytes=VMEM_LIMIT)


class Layout:
    def __init__(self, B, S, C, D):
        assert S % TM == 0 and C % TM == 0
        self.B, self.S, self.C, self.D = B, S, C, D
        self.SB = S + C
        self.T = B * self.SB
        self.LT = S // TM
        self.TPB = self.SB // TM
        self.NT = B * self.TPB

    def split(self, i):
        return i // self.TPB, i % self.TPB

    def mod_row(self, i):
        b, w = self.split(i)
        return jnp.where(w >= self.LT, self.B, b)


def _mod_spec(lay, chunk, tile_of):
    return pl.BlockSpec((None, None, 1, lay.D),
                        lambda *g: (chunk, lay.mod_row(tile_of(*g)), 0, 0))


def _row_spec(D):
    return pl.BlockSpec((1, D), lambda *g: (0, 0))


def _rms(x, gain):
    ms = jnp.mean(x * x, axis=-1, keepdims=True)
    return (x * lax.rsqrt(ms + NORM_EPS)) * gain


def _dot(a, b):
    return jnp.dot(a, b, preferred_element_type=jnp.float32)


def _mod_kernel(c_ref, w_ref, b_ref, o_ref):
    c = c_ref[...]
    a = (c * (1.0 / (1.0 + jnp.exp(-c)))).astype(jnp.bfloat16)
    o_ref[...] = _dot(a, w_ref[...].astype(jnp.bfloat16)) + b_ref[...]


def _modulation(cvec, ada_w, ada_b):
    L, D, N = ada_w.shape
    R = cvec.shape[0]
    tn = 512
    return pl.pallas_call(
        _mod_kernel,
        out_shape=jax.ShapeDtypeStruct((L, R, N), jnp.float32),
        grid=(L, N // tn),
        in_specs=[pl.BlockSpec((R, D), lambda l, j: (0, 0)),
                  pl.BlockSpec((None, D, tn), lambda l, j: (l, 0, j)),
                  pl.BlockSpec((None, 1, tn), lambda l, j: (l, 0, j))],
        out_specs=pl.BlockSpec((None, R, tn), lambda l, j: (l, 0, j)),
        compiler_params=_cparams(("arbitrary", "arbitrary")),
        name="adaln_mod",
    )(cvec, ada_w, ada_b.reshape(L, 1, N))


def _prenorm_kernel(h_ref, gain_ref, sc_ref, sh_ref, u_ref):
    u_ref[...] = (_rms(h_ref[...], gain_ref[...]) * (1.0 + sc_ref[...]) + sh_ref[...]).astype(u_ref.dtype)


def _prenorm(lay, h, gain, mod):
    D = lay.D
    tile = lambda i: i
    return pl.pallas_call(
        _prenorm_kernel,
        out_shape=jax.ShapeDtypeStruct((lay.T, D), jnp.bfloat16),
        grid=(lay.NT,),
        in_specs=[pl.BlockSpec((TM, D), lambda i: (i, 0)), _row_spec(D),
                  _mod_spec(lay, 1, tile), _mod_spec(lay, 0, tile)],
        out_specs=pl.BlockSpec((TM, D), lambda i: (i, 0)),
        compiler_params=_cparams(("arbitrary",)),
        name="prenorm",
    )(h, gain.reshape(1, D), mod, mod)


def _mixer_tail(y, h_ref, g1_ref, gain_ref, sc_ref, sh_ref, wr_ref, br_ref, h1_ref, v_ref, lg_ref):
    h1 = h_ref[...] + g1_ref[...] * y
    h1_ref[...] = h1
    v = _rms(h1, gain_ref[...]) * (1.0 + sc_ref[...]) + sh_ref[...]
    v_ref[...] = v
    lg_ref[...] = _dot(v.astype(jnp.bfloat16), wr_ref[...]) + br_ref[...]


def _tail_in_specs(lay, tile):
    D = lay.D
    return [pl.BlockSpec((TM, D), lambda *g: (tile(*g), 0)),
            _mod_spec(lay, 2, tile),
            _row_spec(D),
            _mod_spec(lay, 4, tile), _mod_spec(lay, 3, tile),
            pl.BlockSpec((D, LANES), lambda *g: (0, 0)),
            pl.BlockSpec((1, LANES), lambda *g: (0, 0))]


def _tail_out(lay, tile):
    D = lay.D
    shapes = (jax.ShapeDtypeStruct((lay.T, D), jnp.float32),
              jax.ShapeDtypeStruct((lay.T, D), jnp.float32),
              jax.ShapeDtypeStruct((lay.T, LANES), jnp.float32))
    specs = (pl.BlockSpec((TM, D), lambda *g: (tile(*g), 0)),
             pl.BlockSpec((TM, D), lambda *g: (tile(*g), 0)),
             pl.BlockSpec((TM, LANES), lambda *g: (tile(*g), 0)))
    return shapes, specs


def _seq_flags(lay, i):
    _, w = lay.split(i)
    has_prev = jnp.logical_and(w != 0, w != lay.LT)
    has_next = jnp.logical_and(w != lay.LT - 1, w != lay.TPB - 1)
    return has_prev, has_next


def _halo_specs(lay, width, col=lambda *g: 0, tile=lambda i: i):
    per = TM // HALO
    last = lay.T // HALO - 1
    prev = pl.BlockSpec((HALO, width), lambda *g: (jnp.maximum(tile(*g) * per - 1, 0), col(*g)))
    nxt = pl.BlockSpec((HALO, width), lambda *g: (jnp.minimum((tile(*g) + 1) * per, last), col(*g)))
    return prev, nxt


def _convin_kernel(u_ref, wb_ref, wc_ref, wx_ref, bg_ref, z_ref):
    x = u_ref[...]
    bg_ref[...] = _dot(x, wb_ref[...]).astype(bg_ref.dtype)
    z_ref[...] = (_dot(x, wc_ref[...]) * _dot(x, wx_ref[...])).astype(z_ref.dtype)


def _conv_in(lay, u, w_in):
    D = lay.D
    tn = 512
    nb = D // tn
    out = jax.ShapeDtypeStruct((lay.T, D), jnp.bfloat16)
    return pl.pallas_call(
        _convin_kernel,
        out_shape=(out, out),
        grid=(nb, lay.NT),
        in_specs=[pl.BlockSpec((TM, D), lambda j, i: (i, 0)),
                  pl.BlockSpec((D, tn), lambda j, i: (0, j)),
                  pl.BlockSpec((D, tn), lambda j, i: (0, nb + j)),
                  pl.BlockSpec((D, tn), lambda j, i: (0, 2 * nb + j))],
        out_specs=(pl.BlockSpec((TM, tn), lambda j, i: (i, j)),
                   pl.BlockSpec((TM, tn), lambda j, i: (i, j))),
        compiler_params=_cparams(("arbitrary", "arbitrary")),
        name="conv_in",
    )(u, w_in, w_in, w_in)


def _convout_kernel(lay, z_ref, zp_ref, zn_ref, bg_ref, ck_ref, wo_ref, *rest):
    i = pl.program_id(0)
    has_prev, has_next = _seq_flags(lay, i)
    z = z_ref[...].astype(jnp.float32)
    row = lax.broadcasted_iota(jnp.int32, z.shape, 0)
    prev_row = jnp.where(has_prev, zp_ref[HALO - 1:HALO, :].astype(jnp.float32), 0.0)
    next_row = jnp.where(has_next, zn_ref[0:1, :].astype(jnp.float32), 0.0)
    z_m1 = jnp.where(row == 0, prev_row, pltpu.roll(z, 1, 0))
    z_p1 = jnp.where(row == TM - 1, next_row, pltpu.roll(z, TM - 1, 0))
    y = z_m1 * ck_ref[0:1, :] + z * ck_ref[1:2, :] + z_p1 * ck_ref[2:3, :]
    g = (bg_ref[...].astype(jnp.float32) * y).astype(jnp.bfloat16)
    _mixer_tail(_dot(g, wo_ref[...]), *rest)


def _conv_out(lay, z, bg, conv_k, w_out, h, mod, gain, wr, br):
    D = lay.D
    tile = lambda i: i
    prev, nxt = _halo_specs(lay, D)
    shapes, specs = _tail_out(lay, tile)
    return pl.pallas_call(
        functools.partial(_convout_kernel, lay),
        out_shape=shapes,
        grid=(lay.NT,),
        in_specs=[pl.BlockSpec((TM, D), lambda i: (i, 0)), prev, nxt,
                  pl.BlockSpec((TM, D), lambda i: (i, 0)),
                  pl.BlockSpec(conv_k.shape, lambda i: (0, 0)),
                  pl.BlockSpec((D, D), lambda i: (0, 0))] + _tail_in_specs(lay, tile),
        out_specs=specs,
        compiler_params=_cparams(("arbitrary",)),
        name="conv_out",
    )(z, z, z, bg, conv_k, w_out, h, mod, gain.reshape(1, D), mod, mod, wr, br)


def _proj_kernel(head_dim, use_norm, use_rope, scale, u_ref, w_ref, gain_ref, cos_ref, sin_ref, o_ref):
    y = _dot(u_ref[...], w_ref[...])
    tn = y.shape[1]
    lane = lax.broadcasted_iota(jnp.int32, (TM, LANES), 1)
    for g in range(tn // LANES):
        yg = y[:, g * LANES:(g + 1) * LANES]
        if use_norm:
            yg = _rms(yg, gain_ref[...])
        if use_rope:
            if head_dim == LANES:
                rot = pltpu.roll(yg, LANES // 2, 1)
            else:
                q = head_dim // 2
                rot = jnp.where(lane % head_dim < q, pltpu.roll(yg, LANES - q, 1), pltpu.roll(yg, q, 1))
            yg = yg * cos_ref[...] + rot * sin_ref[...]
        if scale != 1.0:
            yg = yg * scale
        o_ref[:, g * LANES:(g + 1) * LANES] = yg.astype(o_ref.dtype)


def _project(lay, u, w, col0, ncols, head_dim, gain, cos_t, sin_t, use_norm, use_rope, scale, name):
    D = lay.D
    tn = 512
    nb = ncols // tn
    cb0 = col0 // tn
    kern = functools.partial(_proj_kernel, head_dim, use_norm, use_rope, scale)
    return pl.pallas_call(
        kern,
        out_shape=jax.ShapeDtypeStruct((lay.T, ncols), jnp.bfloat16),
        grid=(nb, lay.NT),
        in_specs=[pl.BlockSpec((TM, D), lambda j, i: (i, 0)),
                  pl.BlockSpec((D, tn), lambda j, i: (0, cb0 + j)),
                  pl.BlockSpec((1, LANES), lambda j, i: (0, 0)),
                  pl.BlockSpec((TM, LANES), lambda j, i: (i % lay.TPB, 0)),
                  pl.BlockSpec((TM, LANES), lambda j, i: (i % lay.TPB, 0))],
        out_specs=pl.BlockSpec((TM, tn), lambda j, i: (i, j)),
        compiler_params=_cparams(("arbitrary", "arbitrary")),
        name=name,
    )(u, w, gain, cos_t, sin_t)


def _rope_tables(lay, head_dim):
    quarter = head_dim // 4
    rows = lay.S // GRID_W
    row = jnp.repeat(jnp.arange(rows), GRID_W).astype(jnp.float32)
    col = jnp.tile(jnp.arange(GRID_W), rows).astype(jnp.float32)
    inv = ROPE_THETA ** (-jnp.arange(quarter, dtype=jnp.float32) / quarter)
    ang = jnp.concatenate([row[:, None] * inv, col[:, None] * inv], axis=-1)
    cos, sin = jnp.cos(ang), jnp.sin(ang)
    reps = LANES // head_dim
    cos_t = jnp.tile(jnp.concatenate([cos, cos], axis=-1), (1, reps))
    sin_t = jnp.tile(jnp.concatenate([-sin, sin], axis=-1), (1, reps))
    cos_t = jnp.concatenate([cos_t, jnp.ones((lay.C, LANES), jnp.float32)], axis=0)
    sin_t = jnp.concatenate([sin_t, jnp.zeros((lay.C, LANES), jnp.float32)], axis=0)
    return cos_t, sin_t


def _flash_kernel(n_rep, q_ref, k_ref, v_ref, o_ref, m_ref, l_ref, acc_ref):
    kv = pl.program_id(3)
    hd = LANES

    @pl.when(kv == 0)
    def _():
        m_ref[...] = jnp.full(m_ref.shape, NEG_INF, jnp.float32)
        l_ref[...] = jnp.zeros(l_ref.shape, jnp.float32)
        acc_ref[...] = jnp.zeros(acc_ref.shape, jnp.float32)

    k = k_ref[...]
    v = v_ref[...]
    tk = k.shape[0]
    for h in range(n_rep):
        q = q_ref[:, h * hd:(h + 1) * hd]
        s = lax.dot_general(q, k, (((1,), (1,)), ((), ())), preferred_element_type=jnp.float32)
        m_prev = m_ref[h]
        m_next = jnp.maximum(m_prev, jnp.max(s, axis=1, keepdims=True))
        alpha = jnp.exp2(m_prev - m_next)
        p = jnp.exp2(s - jnp.tile(m_next, (1, tk // LANES)))
        l_ref[h] = alpha * l_ref[h] + jnp.sum(p, axis=1, keepdims=True)
        m_ref[h] = m_next
        acc_ref[h] = acc_ref[h] * alpha + _dot(p.astype(jnp.bfloat16), v)

    @pl.when(kv == pl.num_programs(3) - 1)
    def _():
        for h in range(n_rep):
            o_ref[:, h * hd:(h + 1) * hd] = (acc_ref[h] / l_ref[h]).astype(o_ref.dtype)


def _flash(lay, q, k, v, n_kv, n_rep, tq, tk, q_blk0, nq, k_blk0, nk, name):
    B = lay.B
    qw = n_rep * LANES
    return pl.pallas_call(
        functools.partial(_flash_kernel, n_rep),
        out_shape=jax.ShapeDtypeStruct((B, nq * tq, n_kv * qw), jnp.bfloat16),
        grid=(B, n_kv, nq, nk),
        in_specs=[pl.BlockSpec((None, tq, qw), lambda b, g, i, j: (b, q_blk0 + i, g)),
                  pl.BlockSpec((None, tk, LANES), lambda b, g, i, j: (b, k_blk0 + j, g)),
                  pl.BlockSpec((None, tk, LANES), lambda b, g, i, j: (b, k_blk0 + j, g))],
        out_specs=pl.BlockSpec((None, tq, qw), lambda b, g, i, j: (b, i, g)),
        scratch_shapes=[pltpu.VMEM((n_rep, tq, LANES), jnp.float32),
                        pltpu.VMEM((n_rep, tq, LANES), jnp.float32),
                        pltpu.VMEM((n_rep, tq, LANES), jnp.float32)],
        compiler_params=_cparams(("arbitrary", "arbitrary", "arbitrary", "arbitrary")),
        name=name,
    )(q, k, v)


def _swa_kernel(lay, tq, sink_ref, q_ref, kp_ref, kc_ref, kn_ref, kx_ref, vp_ref, vc_ref, vn_ref, vx_ref, o_ref):
    pair = pl.program_id(1)
    qb = pl.program_id(2)
    hd = LANES // 2
    n_rep = SWA_HEADS // SWA_KV_HEADS
    W = SWA_WINDOW
    kk = jnp.concatenate([kp_ref[...], kc_ref[...], kn_ref[...], kx_ref[...]], axis=0)
    vv = jnp.concatenate([vp_ref[...], vc_ref[...], vn_ref[...], vx_ref[...]], axis=0)
    nkeys = kk.shape[0]
    nwin = tq + 2 * W
    lane = lax.broadcasted_iota(jnp.int32, (nkeys, LANES), 1)
    low = lane < hd
    kk_sw = pltpu.roll(kk.astype(jnp.float32), hd, 1).astype(kk.dtype)
    vv_sw = pltpu.roll(vv.astype(jnp.float32), hd, 1).astype(vv.dtype)
    k_dup = (jnp.where(low, kk, kk_sw), jnp.where(low, kk_sw, kk))
    v_dup = (jnp.where(low, vv, vv_sw), jnp.where(low, vv_sw, vv))

    start = qb * tq
    qpos = start + lax.broadcasted_iota(jnp.int32, (tq, nkeys), 0)
    col = lax.broadcasted_iota(jnp.int32, (tq, nkeys), 1)
    kpos = start - W + col
    in_win = jnp.logical_and(jnp.abs(qpos - kpos) <= W, jnp.logical_and(kpos >= 0, kpos < lay.S))
    valid = jnp.logical_or(col >= nwin, in_win)

    qlane = lax.broadcasted_iota(jnp.int32, (tq, LANES), 1)
    qlow = qlane < hd
    for g in range(2 * n_rep * hd // LANES):
        qg = q_ref[:, g * LANES:(g + 1) * LANES]
        kvh = (2 * g) // n_rep
        out = None
        for half in range(2):
            head = pair * 2 * n_rep + 2 * g + half
            qm = jnp.where(qlow if half == 0 else jnp.logical_not(qlow), qg, jnp.zeros_like(qg))
            s = lax.dot_general(qm, k_dup[kvh], (((1,), (1,)), ((), ())), preferred_element_type=jnp.float32)
            s = jnp.where(valid, s, NEG_INF)
            sink = sink_ref[head]
            m = jnp.maximum(jnp.max(s, axis=1, keepdims=True), sink)
            p = jnp.exp2(s - m)
            l = jnp.sum(p, axis=1, keepdims=True) + jnp.exp2(sink - m)
            o = _dot(p.astype(jnp.bfloat16), v_dup[kvh]) / l
            out = o if half == 0 else jnp.where(qlow, out, o)
        o_ref[:, g * LANES:(g + 1) * LANES] = out.astype(o_ref.dtype)


def _swa(lay, q, k, v, sink2):
    B, S = lay.B, lay.S
    tq = TM
    W = SWA_WINDOW
    per = tq // W
    nq = S // tq
    qw = 2 * (SWA_HEADS // SWA_KV_HEADS) * (LANES // 2)
    ctx_blk = S // lay.C
    kv_specs = [pl.BlockSpec((None, W, LANES), lambda b, p, i: (b, jnp.maximum(i * per - 1, 0), p)),
                pl.BlockSpec((None, tq, LANES), lambda b, p, i: (b, i, p)),
                pl.BlockSpec((None, W, LANES), lambda b, p, i: (b, (i + 1) * per, p)),
                pl.BlockSpec((None, lay.C, LANES), lambda b, p, i: (b, ctx_blk, p))]
    return pl.pallas_call(
        functools.partial(_swa_kernel, lay, tq),
        out_shape=jax.ShapeDtypeStruct((B, S, lay.D), jnp.bfloat16),
        grid=(B, SWA_KV_HEADS // 2, nq),
        in_specs=[pl.BlockSpec(memory_space=pltpu.SMEM),
                  pl.BlockSpec((None, tq, qw), lambda b, p, i: (b, i, p))] + kv_specs + kv_specs,
        out_specs=pl.BlockSpec((None, tq, qw), lambda b, p, i: (b, i, p)),
        compiler_params=_cparams(("arbitrary", "arbitrary", "arbitrary")),
        name="swa",
    )(sink2, q, k, k, k, k, v, v, v, v)


def _outproj_kernel(lay, has_ctx, ol_ref, oc_ref, wo_ref, *rest):
    x = ol_ref[...]
    if has_ctx:
        _, w = lay.split(pl.program_id(0))
        x = jnp.where(w >= lay.LT, oc_ref[...], x)
    _mixer_tail(_dot(x, wo_ref[...]), *rest)


def _out_proj(lay, o_lat, o_ctx, w_out, h, mod, gain, wr, br):
    D = lay.D
    tile = lambda i: i
    has_ctx = o_ctx is not None
    if not has_ctx:
        o_ctx = o_lat
    shapes, specs = _tail_out(lay, tile)

    def lat_idx(i):
        b, w = lay.split(i)
        return b, jnp.minimum(w, lay.LT - 1), 0

    def ctx_idx(i):
        b, w = lay.split(i)
        return b, jnp.clip(w - lay.LT, 0, lay.TPB - lay.LT - 1), 0

    return pl.pallas_call(
        functools.partial(_outproj_kernel, lay, has_ctx),
        out_shape=shapes,
        grid=(lay.NT,),
        in_specs=[pl.BlockSpec((None, TM, D), lat_idx),
                  pl.BlockSpec((None, TM, D), ctx_idx if has_ctx else lat_idx),
                  pl.BlockSpec((D, D), lambda i: (0, 0))] + _tail_in_specs(lay, tile),
        out_specs=specs,
        compiler_params=_cparams(("arbitrary",)),
        name="attn_out",
    )(o_lat, o_ctx, w_out, h, mod, gain.reshape(1, D), mod, mod, wr, br)


def _pool_kernel(lay, u_ref, up_ref, un_ref, pw_ref, ps_ref, *rest):
    i = pl.program_id(0)
    _, w = lay.split(i)
    has_prev, has_next = _seq_flags(lay, i)
    in_ctx = w >= lay.LT
    seq_len = jnp.where(in_ctx, lay.C, lay.S)
    pos0 = jnp.where(in_ctx, w - lay.LT, w) * TM
    G = len(POOL_WINDOWS)
    gw = lay.D // G
    E_ROWS = TM + 2 * SUBLANES_F32
    pos = pos0 + lax.broadcasted_iota(jnp.int32, (TM, gw), 0)
    ys = []
    for g, win in enumerate(POOL_WINDOWS):
        sl = slice(g * gw, (g + 1) * gw)
        u = u_ref[:, sl].astype(jnp.float32)
        before = jnp.where(has_prev, up_ref[HALO - SUBLANES_F32:HALO, sl].astype(jnp.float32), 0.0)
        after = jnp.where(has_next, un_ref[0:SUBLANES_F32, sl].astype(jnp.float32), 0.0)
        e = jnp.concatenate([before, u, after], axis=0)
        left = win // 2
        right = win - 1 - left
        assert left == right + 1 and left & (left - 1) == 0
        acc = e
        span = 1
        while span < left:
            acc = acc + pltpu.roll(acc, E_ROWS - span, 0)
            span *= 2
        tot = pltpu.roll(acc, left, 0) + acc
        total = tot[SUBLANES_F32:SUBLANES_F32 + TM]
        cnt = jnp.minimum(pos + right, seq_len - 1) - jnp.maximum(pos - left, 0) + 1
        mean = total / cnt.astype(jnp.float32)
        ys.append(_dot((mean - u).astype(jnp.bfloat16), pw_ref[g]))
    y = jnp.concatenate(ys, axis=1) * ps_ref[...]
    _mixer_tail(y, *rest)


def _pool(lay, u, pool_w, pool_scale, h, mod, gain, wr, br):
    D = lay.D
    tile = lambda i: i
    prev, nxt = _halo_specs(lay, D)
    shapes, specs = _tail_out(lay, tile)
    return pl.pallas_call(
        functools.partial(_pool_kernel, lay),
        out_shape=shapes,
        grid=(lay.NT,),
        in_specs=[pl.BlockSpec((TM, D), lambda i: (i, 0)), prev, nxt,
                  pl.BlockSpec(pool_w.shape, lambda i: (0, 0, 0)),
                  _row_spec(D)] + _tail_in_specs(lay, tile),
        out_specs=specs,
        compiler_params=_cparams(("arbitrary",)),
        name="pool",
    )(u, u, u, pool_w, pool_scale.reshape(1, D), h, mod, gain.reshape(1, D), mod, mod, wr, br)


def _route_kernel(lg_ref, info_ref, gate_ref, cnt_ref, carry_ref):
    i = pl.program_id(0)

    @pl.when(i == 0)
    def _():
        carry_ref[...] = jnp.zeros(carry_ref.shape, jnp.float32)

    lg = lg_ref[...]
    lane = lax.broadcasted_iota(jnp.int32, lg.shape, 1)
    lane_f = lane.astype(jnp.float32)
    big = jnp.float32(4 * LANES)

    def first_lane(mask):
        return jnp.min(jnp.where(mask, lane_f, big), axis=1, keepdims=True).astype(jnp.int32)

    is_grp = lane < N_GROUPS
    gl = jnp.where(is_grp, lg, NEG_INF)
    gmax = jnp.max(gl, axis=1, keepdims=True)
    grp = first_lane(jnp.logical_and(is_grp, gl == gmax))
    p_grp = 1.0 / jnp.sum(jnp.where(is_grp, jnp.exp(gl - gmax), 0.0), axis=1, keepdims=True)
    eid = lane - N_GROUPS
    in_grp = jnp.logical_and(lane >= N_GROUPS + grp * EXPERTS_PER_GROUP,
                             lane < N_GROUPS + (grp + 1) * EXPERTS_PER_GROUP)
    el = jnp.where(in_grp, lg, NEG_INF)
    t1 = jnp.max(el, axis=1, keepdims=True)
    e1 = first_lane(jnp.logical_and(in_grp, el == t1)) - N_GROUPS
    rest = jnp.logical_and(in_grp, eid != e1)
    el2 = jnp.where(rest, lg, NEG_INF)
    t2 = jnp.max(el2, axis=1, keepdims=True)
    e2 = first_lane(jnp.logical_and(rest, el2 == t2)) - N_GROUPS
    d = jnp.exp(t2 - t1)
    g1 = p_grp / (1.0 + d)
    g2 = p_grp * d / (1.0 + d)

    oh1 = lane == e1
    oh2 = lane == e2
    oh = jnp.where(jnp.logical_or(oh1, oh2), 1.0, 0.0)
    r = lax.broadcasted_iota(jnp.int32, (TM, TM), 0)
    c = lax.broadcasted_iota(jnp.int32, (TM, TM), 1)
    tri = jnp.where(c < r, 1.0, 0.0).astype(jnp.bfloat16)
    before = _dot(tri, oh.astype(jnp.bfloat16)) + carry_ref[0:1, :]
    r1 = jnp.sum(jnp.where(oh1, before, 0.0), axis=1, keepdims=True)
    r2 = jnp.sum(jnp.where(oh2, before, 0.0), axis=1, keepdims=True)
    carry = carry_ref[0:1, :] + jnp.sum(oh, axis=0, keepdims=True)
    carry_ref[...] = jnp.broadcast_to(carry, carry_ref.shape)

    info = jnp.where(lane == 0, e1, jnp.where(lane == 1, e2, jnp.where(
        lane == 2, r1.astype(jnp.int32), jnp.where(lane == 3, r2.astype(jnp.int32), 0))))
    info_ref[...] = info
    gate_ref[...] = jnp.where(lane == 0, g1, jnp.where(lane == 1, g2, 0.0))
    cnt_ref[...] = jnp.broadcast_to(carry, cnt_ref.shape).astype(jnp.int32)


def _route(lay, logits):
    T = lay.T
    return pl.pallas_call(
        _route_kernel,
        out_shape=(jax.ShapeDtypeStruct((T, LANES), jnp.int32),
                   jax.ShapeDtypeStruct((T, LANES), jnp.float32),
                   jax.ShapeDtypeStruct((SUBLANES_F32, LANES), jnp.int32)),
        grid=(lay.NT,),
        in_specs=[pl.BlockSpec((TM, LANES), lambda i: (i, 0))],
        out_specs=(pl.BlockSpec((TM, LANES), lambda i: (i, 0)),
                   pl.BlockSpec((TM, LANES), lambda i: (i, 0)),
                   pl.BlockSpec((SUBLANES_F32, LANES), lambda i: (0, 0))),
        scratch_shapes=[pltpu.VMEM((SUBLANES_F32, LANES), jnp.float32)],
        compiler_params=_cparams(("arbitrary",)),
        name="moe_route",
    )(logits)


def _dispatch_kernel(dest_ref, v_ref, xs_in, xs_hbm, sem):
    del xs_in

    def issue(r, _):
        for k in range(TOP_K):
            pltpu.make_async_copy(v_ref.at[pl.ds(r, 1)], xs_hbm.at[pl.ds(dest_ref[0, k, r], 1)], sem).start()
        return 0

    lax.fori_loop(0, TM, issue, 0, unroll=8)
    for k in range(TOP_K):
        pltpu.make_async_copy(v_ref, xs_hbm.at[pl.ds(0, TM)], sem).wait()


def _dispatch(lay, v, dest, P):
    D = lay.D
    xs0 = jnp.zeros((P, D), jnp.float32)
    return pl.pallas_call(
        _dispatch_kernel,
        out_shape=jax.ShapeDtypeStruct((P, D), jnp.float32),
        grid=(lay.NT,),
        in_specs=[pl.BlockSpec((1, TOP_K, TM), lambda i: (i, 0, 0), memory_space=pltpu.SMEM),
                  pl.BlockSpec((TM, D), lambda i: (i, 0)),
                  pl.BlockSpec(memory_space=pl.ANY)],
        out_specs=pl.BlockSpec(memory_space=pl.ANY),
        scratch_shapes=[pltpu.SemaphoreType.DMA(())],
        input_output_aliases={2: 0},
        compiler_params=pltpu.CompilerParams(dimension_semantics=("arbitrary",), has_side_effects=True),
        name="moe_dispatch",
    )(dest, v, xs0)


def _expert_kernel(be_ref, na_ref, x_ref, wg_ref, wu_ref, wd_ref, y_ref):
    del be_ref

    @pl.when(pl.program_id(0) < na_ref[0])
    def _():
        x = x_ref[...].astype(jnp.bfloat16)
        hg = _dot(x, wg_ref[...])
        hu = _dot(x, wu_ref[...])
        a = (hg * (1.0 / (1.0 + jnp.exp(-hg)))) * hu
        y_ref[...] = _dot(a.astype(jnp.bfloat16), wd_ref[...])

    @pl.when(pl.program_id(0) >= na_ref[0])
    def _():
        y_ref[...] = jnp.zeros(y_ref.shape, y_ref.dtype)


def _experts(lay, xs, blk_expert, n_active, wg, wu, wd):
    D = lay.D
    P = xs.shape[0]
    DE = wg.shape[2]
    grid_spec = pltpu.PrefetchScalarGridSpec(
        num_scalar_prefetch=2,
        grid=(P // MOE_BLOCK,),
        in_specs=[pl.BlockSpec((MOE_BLOCK, D), lambda i, be, na: (i, 0)),
                  pl.BlockSpec((None, D, DE), lambda i, be, na: (be[i], 0, 0)),
                  pl.BlockSpec((None, D, DE), lambda i, be, na: (be[i], 0, 0)),
                  pl.BlockSpec((None, DE, D), lambda i, be, na: (be[i], 0, 0))],
        out_specs=pl.BlockSpec((MOE_BLOCK, D), lambda i, be, na: (i, 0)),
    )
    return pl.pallas_call(
        _expert_kernel,
        out_shape=jax.ShapeDtypeStruct((P, D), jnp.float32),
        grid_spec=grid_spec,
        compiler_params=_cparams(("arbitrary",)),
        name="moe_experts",
    )(blk_expert, n_active, xs, wg, wu, wd)


def _combine_kernel(final, dest_ref, ys_hbm, gate_ref, h_ref, g2_ref, gain_ref, sc_ref, sh_ref, *refs):
    *out_refs, ybuf, sem = refs

    def issue(r, _):
        for k in range(TOP_K):
            pltpu.make_async_copy(ys_hbm.at[pl.ds(dest_ref[0, k, r], 1)], ybuf.at[k, pl.ds(r, 1)], sem).start()
        return 0

    lax.fori_loop(0, TM, issue, 0, unroll=8)
    for k in range(TOP_K):
        pltpu.make_async_copy(ys_hbm.at[pl.ds(0, TM)], ybuf.at[k], sem).wait()
    gate = gate_ref[...]
    f = gate[:, 0:1] * ybuf[0] + gate[:, 1:2] * ybuf[1]
    h2 = h_ref[...] + g2_ref[...] * f
    if final:
        (out_ref,) = out_refs
        out_ref[...] = _rms(h2, gain_ref[...])
    else:
        h2_ref, u_ref = out_refs
        h2_ref[...] = h2
        u_ref[...] = (_rms(h2, gain_ref[...]) * (1.0 + sc_ref[...]) + sh_ref[...]).astype(u_ref.dtype)


def _combine(lay, ys, dest, gates, h1, mod, mod_next, gain_next, final):
    D = lay.D
    row_tile = pl.BlockSpec((TM, D), lambda i: (i, 0))
    if final:
        n = lay.B * lay.LT
        tile = lambda i: (i // lay.LT) * lay.TPB + i % lay.LT
        out_shape = (jax.ShapeDtypeStruct((lay.B * lay.S, D), jnp.float32),)
        out_specs = (row_tile,)
    else:
        n = lay.NT
        tile = lambda i: i
        out_shape = (jax.ShapeDtypeStruct((lay.T, D), jnp.float32),
                     jax.ShapeDtypeStruct((lay.T, D), jnp.bfloat16))
        out_specs = (row_tile, row_tile)
    return pl.pallas_call(
        functools.partial(_combine_kernel, final),
        out_shape=out_shape,
        grid=(n,),
        in_specs=[pl.BlockSpec((1, TOP_K, TM), lambda i: (tile(i), 0, 0), memory_space=pltpu.SMEM),
                  pl.BlockSpec(memory_space=pl.ANY),
                  pl.BlockSpec((TM, LANES), lambda i: (tile(i), 0)),
                  pl.BlockSpec((TM, D), lambda i: (tile(i), 0)),
                  _mod_spec(lay, 5, tile),
                  _row_spec(D),
                  _mod_spec(lay, 1, tile), _mod_spec(lay, 0, tile)],
        out_specs=out_specs,
        scratch_shapes=[pltpu.VMEM((TOP_K, TM, D), jnp.float32), pltpu.SemaphoreType.DMA(())],
        compiler_params=_cparams(("arbitrary",)),
        name="moe_combine_final" if final else "moe_combine",
    )(dest, ys, gates, h1, mod, gain_next.reshape(1, D), mod_next, mod_next)


def _moe(lay, v, logits, h1, mod, mod_next, gain_next, wg, wu, wd, final):
    T, NT = lay.T, lay.NT
    info, gates, cnt = _route(lay, logits)
    counts = cnt[0, :N_EXPERTS]
    padded = ((counts + MOE_BLOCK - 1) // MOE_BLOCK) * MOE_BLOCK
    pad_end = jnp.cumsum(padded)
    pad_start = pad_end - padded
    expert = info[:, :TOP_K]
    onehot = expert[:, :, None] == jnp.arange(N_EXPERTS, dtype=jnp.int32)
    dest = info[:, TOP_K:2 * TOP_K] + jnp.sum(jnp.where(onehot, pad_start, 0), axis=-1)
    dest = dest.reshape(NT, TM, TOP_K).transpose(0, 2, 1).astype(jnp.int32)
    n_blocks = -(-(T * TOP_K + N_EXPERTS * (MOE_BLOCK - 1)) // MOE_BLOCK)
    blk_start = jnp.arange(n_blocks, dtype=jnp.int32) * MOE_BLOCK
    blk_expert = jnp.minimum(jnp.sum(pad_end[None, :] <= blk_start[:, None], axis=1),
                             N_EXPERTS - 1).astype(jnp.int32)
    n_active = (pad_end[-1:] // MOE_BLOCK).astype(jnp.int32)
    xs = _dispatch(lay, v, dest, n_blocks * MOE_BLOCK)
    ys = _experts(lay, xs, blk_expert, n_active, wg, wu, wd)
    return _combine(lay, ys, dest, gates, h1, mod, mod_next, gain_next, final)


def kernel(x, c, ctx, c_ctx, ada_w, ada_b, norm_mix, norm_ffn, norm_final, conv_in, conv_k, conv_out, gqa_qkv, gqa_q_gain, gqa_k_gain, gqa_out, pool_w, pool_scale, swa_qkv, swa_sink, swa_out, router_grp_w, router_grp_b, router_exp_w, router_exp_b, exp_gate, exp_up, exp_down):
    B, S, D = x.shape
    C = ctx.shape[1]
    L = ada_w.shape[0]
    lay = Layout(B, S, C, D)
    bf = jnp.bfloat16

    R = SUBLANES_BF16
    cvec = jnp.zeros((R, D), jnp.float32).at[:B].set(c).at[B].set(c_ctx)
    mod_all = _modulation(cvec, ada_w, ada_b)[:, :B + 1].reshape(L, B + 1, 6, 1, D).transpose(0, 2, 1, 3, 4)

    pad = LANES - N_GROUPS - N_EXPERTS
    wr_all = jnp.concatenate([router_grp_w, router_exp_w, jnp.zeros((L, D, pad), jnp.float32)], axis=-1).astype(bf)
    br_all = jnp.concatenate([router_grp_b, router_exp_b, jnp.zeros((L, pad), jnp.float32)], axis=-1)

    h = jnp.concatenate([x, ctx], axis=1).reshape(lay.T, D)
    u = _prenorm(lay, h, norm_mix[0], mod_all[0])
    one = jnp.ones((1, LANES), jnp.float32)

    for i in range(L):
        m, j = i % 4, i // 4
        mod = mod_all[i]
        wr, br = wr_all[i], br_all[i].reshape(1, LANES)
        tail = (h, mod, norm_ffn[i], wr, br)
        if m == 0:
            bg, z = _conv_in(lay, u, conv_in[j].astype(bf))
            h1, v, lg = _conv_out(lay, z, bg, conv_k[j], conv_out[j].astype(bf), *tail)
        elif m == 1:
            hd = D // GQA_HEADS
            w = gqa_qkv[j].astype(bf)
            cos_t, sin_t = _rope_tables(lay, hd)
            nq, nk = GQA_HEADS * hd, GQA_KV_HEADS * hd
            qg = gqa_q_gain[j].reshape(1, hd)
            kg = gqa_k_gain[j].reshape(1, hd)
            q = _project(lay, u, w, 0, nq, hd, qg, cos_t, sin_t, True, True, hd ** -0.5 * LOG2E, "gqa_q")
            k = _project(lay, u, w, nq, nk, hd, kg, cos_t, sin_t, True, True, 1.0, "gqa_k")
            vv = _project(lay, u, w, nq + nk, nk, hd, one, cos_t, sin_t, False, False, 1.0, "gqa_v")
            q3, k3, v3 = (a.reshape(B, lay.SB, -1) for a in (q, k, vv))
            n_rep = GQA_HEADS // GQA_KV_HEADS
            tq = 512 if S % 512 == 0 else TM
            tk = 768 if lay.SB % 768 == 0 else TM
            o_lat = _flash(lay, q3, k3, v3, GQA_KV_HEADS, n_rep, tq, tk, 0, S // tq, 0, lay.SB // tk, "gqa_flash")
            o_ctx = _flash(lay, q3, k3, v3, GQA_KV_HEADS, n_rep, C, C, S // C, 1, S // C, 1, "gqa_flash_ctx")
            h1, v, lg = _out_proj(lay, o_lat, o_ctx, gqa_out[j].astype(bf), *tail)
        elif m == 2:
            h1, v, lg = _pool(lay, u, pool_w[j].astype(bf), pool_scale[j], *tail)
        else:
            hd = D // SWA_HEADS
            w = swa_qkv[j].astype(bf)
            cos_t, sin_t = _rope_tables(lay, hd)
            nq, nk = SWA_HEADS * hd, SWA_KV_HEADS * hd
            q = _project(lay, u, w, 0, nq, hd, one, cos_t, sin_t, False, True, hd ** -0.5 * LOG2E, "swa_q")
            k = _project(lay, u, w, nq, nk, hd, one, cos_t, sin_t, False, True, 1.0, "swa_k")
            vv = _project(lay, u, w, nq + nk, nk, hd, one, cos_t, sin_t, False, False, 1.0, "swa_v")
            q3, k3, v3 = (a.reshape(B, lay.SB, -1) for a in (q, k, vv))
            o_lat = _swa(lay, q3, k3, v3, swa_sink[j] * LOG2E)
            h1, v, lg = _out_proj(lay, o_lat, None, swa_out[j].astype(bf), *tail)
        final = i == L - 1
        mod_next = mod if final else mod_all[i + 1]
        gain_next = norm_final if final else norm_mix[i + 1]
        res = _moe(lay, v, lg, h1, mod, mod_next, gain_next,
                   exp_gate[i].astype(bf), exp_up[i].astype(bf), exp_down[i].astype(bf), final)
        if final:
            return res[0].reshape(B, S, D)
        h, u = res
```

```python
import functools

import jax
import jax.numpy as jnp
from jax import lax
from jax.experimental import pallas as pl
from jax.experimental.pallas import tpu as pltpu

GRID_W = 64
NORM_EPS = 1e-6
ROPE_THETA = 10000.0
NEG_INF = -1e30
GQA_HEADS, GQA_KV_HEADS = 16, 4
SWA_HEADS, SWA_KV_HEADS, SWA_WINDOW = 32, 8, 128
POOL_WINDOWS = (2, 4, 8, 16)
N_GROUPS, EXPERTS_PER_GROUP, TOP_K = 8, 4, 2
N_EXPERTS = N_GROUPS * EXPERTS_PER_GROUP
LOG2E = 1.4426950408889634

LANES = 128
SUBLANES_F32 = 8
SUBLANES_BF16 = 16
VMEM_LIMIT = 56 * 1024 * 1024
TM = 256
EXPERT_ROWS = 256
HALO = SUBLANES_BF16


def _cparams(sem):
    return pltpu.CompilerParams(dimension_semantics=sem, vmem_limit_bytes=VMEM_LIMIT)


class Layout:
    def __init__(self, B, S, C, D):
        assert S % TM == 0 and C % TM == 0
        self.B, self.S, self.C, self.D = B, S, C, D
        self.SB = S + C
        self.T = B * self.SB
        self.LT = S // TM
        self.TPB = self.SB // TM
        self.NT = B * self.TPB

    def split(self, i):
        return i // self.TPB, i % self.TPB

    def mod_row(self, i):
        b, w = self.split(i)
        return jnp.where(w >= self.LT, self.B, b)


def _mod_spec(lay, chunk, tile_of):
    return pl.BlockSpec((None, None, 1, lay.D),
                        lambda *g: (chunk, lay.mod_row(tile_of(*g)), 0, 0))


def _row_spec(D):
    return pl.BlockSpec((1, D), lambda *g: (0, 0))


def _rms(x, gain):
    ms = jnp.mean(x * x, axis=-1, keepdims=True)
    return (x * lax.rsqrt(ms + NORM_EPS)) * gain


def _dot(a, b):
    return jnp.dot(a, b, preferred_element_type=jnp.float32)


def _mod_kernel(c_ref, w_ref, b_ref, o_ref):
    c = c_ref[...]
    a = (c * (1.0 / (1.0 + jnp.exp(-c)))).astype(jnp.bfloat16)
    o_ref[...] = _dot(a, w_ref[...].astype(jnp.bfloat16)) + b_ref[...]


def _modulation(cvec, ada_w, ada_b):
    L, D, N = ada_w.shape
    R = cvec.shape[0]
    tn = 512
    return pl.pallas_call(
        _mod_kernel,
        out_shape=jax.ShapeDtypeStruct((L, R, N), jnp.float32),
        grid=(L, N // tn),
        in_specs=[pl.BlockSpec((R, D), lambda l, j: (0, 0)),
                  pl.BlockSpec((None, D, tn), lambda l, j: (l, 0, j)),
                  pl.BlockSpec((None, 1, tn), lambda l, j: (l, 0, j))],
        out_specs=pl.BlockSpec((None, R, tn), lambda l, j: (l, 0, j)),
        compiler_params=_cparams(("arbitrary", "arbitrary")),
        name="adaln_mod",
    )(cvec, ada_w, ada_b.reshape(L, 1, N))


def _prenorm_kernel(h_ref, gain_ref, sc_ref, sh_ref, u_ref):
    u_ref[...] = (_rms(h_ref[...], gain_ref[...]) * (1.0 + sc_ref[...]) + sh_ref[...]).astype(u_ref.dtype)


def _prenorm(lay, h, gain, mod):
    D = lay.D
    tile = lambda i: i
    return pl.pallas_call(
        _prenorm_kernel,
        out_shape=jax.ShapeDtypeStruct((lay.T, D), jnp.bfloat16),
        grid=(lay.NT,),
        in_specs=[pl.BlockSpec((TM, D), lambda i: (i, 0)), _row_spec(D),
                  _mod_spec(lay, 1, tile), _mod_spec(lay, 0, tile)],
        out_specs=pl.BlockSpec((TM, D), lambda i: (i, 0)),
        compiler_params=_cparams(("arbitrary",)),
        name="prenorm",
    )(h, gain.reshape(1, D), mod, mod)


def _mixer_tail(y, h_ref, g1_ref, gain_ref, sc_ref, sh_ref, wr_ref, br_ref, h1_ref, v_ref, lg_ref):
    h1 = h_ref[...] + g1_ref[...] * y
    h1_ref[...] = h1
    v = _rms(h1, gain_ref[...]) * (1.0 + sc_ref[...]) + sh_ref[...]
    v_ref[...] = v
    lg_ref[...] = _dot(v.astype(jnp.bfloat16), wr_ref[...]) + br_ref[...]


def _tail_in_specs(lay, tile):
    D = lay.D
    return [pl.BlockSpec((TM, D), lambda *g: (tile(*g), 0)),
            _mod_spec(lay, 2, tile),
            _row_spec(D),
            _mod_spec(lay, 4, tile), _mod_spec(lay, 3, tile),
            pl.BlockSpec((D, LANES), lambda *g: (0, 0)),
            pl.BlockSpec((1, LANES), lambda *g: (0, 0))]


def _tail_out(lay, tile):
    D = lay.D
    shapes = (jax.ShapeDtypeStruct((lay.T, D), jnp.float32),
              jax.ShapeDtypeStruct((lay.T, D), jnp.float32),
              jax.ShapeDtypeStruct((lay.T, LANES), jnp.float32))
    specs = (pl.BlockSpec((TM, D), lambda *g: (tile(*g), 0)),
             pl.BlockSpec((TM, D), lambda *g: (tile(*g), 0)),
             pl.BlockSpec((TM, LANES), lambda *g: (tile(*g), 0)))
    return shapes, specs


def _seq_flags(lay, i):
    _, w = lay.split(i)
    has_prev = jnp.logical_and(w != 0, w != lay.LT)
    has_next = jnp.logical_and(w != lay.LT - 1, w != lay.TPB - 1)
    return has_prev, has_next


def _halo_specs(lay, width, col=lambda *g: 0, tile=lambda i: i):
    per = TM // HALO
    last = lay.T // HALO - 1
    prev = pl.BlockSpec((HALO, width), lambda *g: (jnp.maximum(tile(*g) * per - 1, 0), col(*g)))
    nxt = pl.BlockSpec((HALO, width), lambda *g: (jnp.minimum((tile(*g) + 1) * per, last), col(*g)))
    return prev, nxt


def _convin_kernel(u_ref, wb_ref, wc_ref, wx_ref, bg_ref, z_ref):
    x = u_ref[...]
    bg_ref[...] = _dot(x, wb_ref[...]).astype(bg_ref.dtype)
    z_ref[...] = (_dot(x, wc_ref[...]) * _dot(x, wx_ref[...])).astype(z_ref.dtype)


def _conv_in(lay, u, w_in):
    D = lay.D
    tn = 512
    nb = D // tn
    out = jax.ShapeDtypeStruct((lay.T, D), jnp.bfloat16)
    return pl.pallas_call(
        _convin_kernel,
        out_shape=(out, out),
        grid=(nb, lay.NT),
        in_specs=[pl.BlockSpec((TM, D), lambda j, i: (i, 0)),
                  pl.BlockSpec((D, tn), lambda j, i: (0, j)),
                  pl.BlockSpec((D, tn), lambda j, i: (0, nb + j)),
                  pl.BlockSpec((D, tn), lambda j, i: (0, 2 * nb + j))],
        out_specs=(pl.BlockSpec((TM, tn), lambda j, i: (i, j)),
                   pl.BlockSpec((TM, tn), lambda j, i: (i, j))),
        compiler_params=_cparams(("arbitrary", "arbitrary")),
        name="conv_in",
    )(u, w_in, w_in, w_in)


def _convout_kernel(lay, z_ref, zp_ref, zn_ref, bg_ref, ck_ref, wo_ref, *rest):
    i = pl.program_id(0)
    has_prev, has_next = _seq_flags(lay, i)
    z = z_ref[...].astype(jnp.float32)
    row = lax.broadcasted_iota(jnp.int32, z.shape, 0)
    prev_row = jnp.where(has_prev, zp_ref[HALO - 1:HALO, :].astype(jnp.float32), 0.0)
    next_row = jnp.where(has_next, zn_ref[0:1, :].astype(jnp.float32), 0.0)
    z_m1 = jnp.where(row == 0, prev_row, pltpu.roll(z, 1, 0))
    z_p1 = jnp.where(row == TM - 1, next_row, pltpu.roll(z, TM - 1, 0))
    y = z_m1 * ck_ref[0:1, :] + z * ck_ref[1:2, :] + z_p1 * ck_ref[2:3, :]
    g = (bg_ref[...].astype(jnp.float32) * y).astype(jnp.bfloat16)
    _mixer_tail(_dot(g, wo_ref[...]), *rest)


def _conv_out(lay, z, bg, conv_k, w_out, h, mod, gain, wr, br):
    D = lay.D
    tile = lambda i: i
    prev, nxt = _halo_specs(lay, D)
    shapes, specs = _tail_out(lay, tile)
    return pl.pallas_call(
        functools.partial(_convout_kernel, lay),
        out_shape=shapes,
        grid=(lay.NT,),
        in_specs=[pl.BlockSpec((TM, D), lambda i: (i, 0)), prev, nxt,
                  pl.BlockSpec((TM, D), lambda i: (i, 0)),
                  pl.BlockSpec(conv_k.shape, lambda i: (0, 0)),
                  pl.BlockSpec((D, D), lambda i: (0, 0))] + _tail_in_specs(lay, tile),
        out_specs=specs,
        compiler_params=_cparams(("arbitrary",)),
        name="conv_out",
    )(z, z, z, bg, conv_k, w_out, h, mod, gain.reshape(1, D), mod, mod, wr, br)


def _proj_kernel(head_dim, use_norm, use_rope, scale, u_ref, w_ref, gain_ref, cos_ref, sin_ref, o_ref):
    y = _dot(u_ref[...], w_ref[...])
    tn = y.shape[1]
    lane = lax.broadcasted_iota(jnp.int32, (TM, LANES), 1)
    for g in range(tn // LANES):
        yg = y[:, g * LANES:(g + 1) * LANES]
        if use_norm:
            yg = _rms(yg, gain_ref[...])
        if use_rope:
            if head_dim == LANES:
                rot = pltpu.roll(yg, LANES // 2, 1)
            else:
                q = head_dim // 2
                rot = jnp.where(lane % head_dim < q, pltpu.roll(yg, LANES - q, 1), pltpu.roll(yg, q, 1))
            yg = yg * cos_ref[...] + rot * sin_ref[...]
        if scale != 1.0:
            yg = yg * scale
        o_ref[:, g * LANES:(g + 1) * LANES] = yg.astype(o_ref.dtype)


def _project(lay, u, w, col0, ncols, head_dim, gain, cos_t, sin_t, use_norm, use_rope, scale, name):
    D = lay.D
    tn = 512
    nb = ncols // tn
    cb0 = col0 // tn
    kern = functools.partial(_proj_kernel, head_dim, use_norm, use_rope, scale)
    return pl.pallas_call(
        kern,
        out_shape=jax.ShapeDtypeStruct((lay.T, ncols), jnp.bfloat16),
        grid=(nb, lay.NT),
        in_specs=[pl.BlockSpec((TM, D), lambda j, i: (i, 0)),
                  pl.BlockSpec((D, tn), lambda j, i: (0, cb0 + j)),
                  pl.BlockSpec((1, LANES), lambda j, i: (0, 0)),
                  pl.BlockSpec((TM, LANES), lambda j, i: (i % lay.TPB, 0)),
                  pl.BlockSpec((TM, LANES), lambda j, i: (i % lay.TPB, 0))],
        out_specs=pl.BlockSpec((TM, tn), lambda j, i: (i, j)),
        compiler_params=_cparams(("arbitrary", "arbitrary")),
        name=name,
    )(u, w, gain, cos_t, sin_t)


def _rope_tables(lay, head_dim):
    quarter = head_dim // 4
    rows = lay.S // GRID_W
    row = jnp.repeat(jnp.arange(rows), GRID_W).astype(jnp.float32)
    col = jnp.tile(jnp.arange(GRID_W), rows).astype(jnp.float32)
    inv = ROPE_THETA ** (-jnp.arange(quarter, dtype=jnp.float32) / quarter)
    ang = jnp.concatenate([row[:, None] * inv, col[:, None] * inv], axis=-1)
    cos, sin = jnp.cos(ang), jnp.sin(ang)
    reps = LANES // head_dim
    cos_t = jnp.tile(jnp.concatenate([cos, cos], axis=-1), (1, reps))
    sin_t = jnp.tile(jnp.concatenate([-sin, sin], axis=-1), (1, reps))
    cos_t = jnp.concatenate([cos_t, jnp.ones((lay.C, LANES), jnp.float32)], axis=0)
    sin_t = jnp.concatenate([sin_t, jnp.zeros((lay.C, LANES), jnp.float32)], axis=0)
    return cos_t, sin_t


def _flash_kernel(n_rep, tk, q_ref, k_ref, v_ref, o_ref, vx_ref, s_ref, m_ref, l_ref, acc_ref):
    hd = LANES
    nk = k_ref.shape[0] // tk

    @pl.when(pl.program_id(2) == 0)
    def _():
        vx_ref[:, :hd] = v_ref[...]
        vx_ref[:, hd:] = jnp.ones((vx_ref.shape[0], hd), vx_ref.dtype)

    m_ref[...] = jnp.full(m_ref.shape, NEG_INF, jnp.float32)
    l_ref[...] = jnp.zeros(l_ref.shape, jnp.float32)
    acc_ref[...] = jnp.zeros(acc_ref.shape, jnp.float32)

    def rows(j):
        start = j * tk
        return pl.ds(start if isinstance(start, int) else pl.multiple_of(start, tk), tk)

    def step(j, slot, with_next):
        vx = vx_ref[rows(j), :]
        if with_next:
            k_next = k_ref[rows(j + 1), :]
        for h in range(n_rep):
            if with_next:
                q = q_ref[:, h * hd:(h + 1) * hd]
                s_ref[1 - slot, h] = lax.dot_general(q, k_next, (((1,), (1,)), ((), ())),
                                                     preferred_element_type=jnp.float32)
            s = s_ref[slot, h]
            m_prev = m_ref[h]
            m_next = jnp.maximum(m_prev, jnp.max(s, axis=1, keepdims=True))
            alpha = jnp.exp2(m_prev - m_next)
            p = jnp.exp2(s - m_next[:, :1]).astype(jnp.bfloat16)
            pv = _dot(p, vx)
            m_ref[h] = m_next
            l_ref[h] = alpha * l_ref[h] + pv[:, hd:]
            acc_ref[h] = alpha * acc_ref[h] + pv[:, :hd]

    k0 = k_ref[rows(0), :]
    for h in range(n_rep):
        s_ref[0, h] = lax.dot_general(q_ref[:, h * hd:(h + 1) * hd], k0, (((1,), (1,)), ((), ())),
                                      preferred_element_type=jnp.float32)

    n_pairs = (nk - 1) // 2

    def body(t, _):
        step(2 * t, 0, True)
        step(2 * t + 1, 1, True)
        return 0

    lax.fori_loop(0, n_pairs, body, 0)
    for j in range(2 * n_pairs, nk):
        step(j, j % 2, j + 1 < nk)
    for h in range(n_rep):
        o_ref[:, h * hd:(h + 1) * hd] = (acc_ref[h] / l_ref[h]).astype(o_ref.dtype)


def _flash(lay, q, k, v, n_kv, n_rep, tq, tk, q_blk0, nq, kb, k_blk0, name):
    B = lay.B
    qw = n_rep * LANES
    return pl.pallas_call(
        functools.partial(_flash_kernel, n_rep, tk),
        out_shape=jax.ShapeDtypeStruct((B, nq * tq, n_kv * qw), jnp.bfloat16),
        grid=(B, n_kv, nq),
        in_specs=[pl.BlockSpec((None, tq, qw), lambda b, g, i: (b, q_blk0 + i, g)),
                  pl.BlockSpec((None, kb, LANES), lambda b, g, i: (b, k_blk0, g)),
                  pl.BlockSpec((None, kb, LANES), lambda b, g, i: (b, k_blk0, g))],
        out_specs=pl.BlockSpec((None, tq, qw), lambda b, g, i: (b, i, g)),
        scratch_shapes=[pltpu.VMEM((kb, 2 * LANES), jnp.bfloat16),
                        pltpu.VMEM((2, n_rep, tq, tk), jnp.float32),
                        pltpu.VMEM((n_rep, tq, LANES), jnp.float32),
                        pltpu.VMEM((n_rep, tq, LANES), jnp.float32),
                        pltpu.VMEM((n_rep, tq, LANES), jnp.float32)],
        compiler_params=_cparams(("arbitrary", "arbitrary", "arbitrary")),
        name=name,
    )(q, k, v)


def _swa_kernel(lay, tq, sink_ref, q_ref, kp_ref, kc_ref, kn_ref, kx_ref, vp_ref, vc_ref, vn_ref, vx_ref, o_ref):
    pair = pl.program_id(1)
    qb = pl.program_id(2)
    hd = LANES // 2
    n_rep = SWA_HEADS // SWA_KV_HEADS
    W = SWA_WINDOW
    kk = jnp.concatenate([kp_ref[...], kc_ref[...], kn_ref[...], kx_ref[...]], axis=0)
    vv = jnp.concatenate([vp_ref[...], vc_ref[...], vn_ref[...], vx_ref[...]], axis=0)
    nkeys = kk.shape[0]
    nwin = tq + 2 * W
    lane = lax.broadcasted_iota(jnp.int32, (nkeys, LANES), 1)
    low = lane < hd
    kk_sw = pltpu.roll(kk.astype(jnp.float32), hd, 1).astype(kk.dtype)
    vv_sw = pltpu.roll(vv.astype(jnp.float32), hd, 1).astype(vv.dtype)
    k_dup = (jnp.where(low, kk, kk_sw), jnp.where(low, kk_sw, kk))
    v_dup = (jnp.where(low, vv, vv_sw), jnp.where(low, vv_sw, vv))

    start = qb * tq
    qpos = start + lax.broadcasted_iota(jnp.int32, (tq, nkeys), 0)
    col = lax.broadcasted_iota(jnp.int32, (tq, nkeys), 1)
    kpos = start - W + col
    in_win = jnp.logical_and(jnp.abs(qpos - kpos) <= W, jnp.logical_and(kpos >= 0, kpos < lay.S))
    valid = jnp.logical_or(col >= nwin, in_win)

    qlane = lax.broadcasted_iota(jnp.int32, (tq, LANES), 1)
    qlow = qlane < hd
    for g in range(2 * n_rep * hd // LANES):
        qg = q_ref[:, g * LANES:(g + 1) * LANES]
        kvh = (2 * g) // n_rep
        out = None
        for half in range(2):
            head = pair * 2 * n_rep + 2 * g + half
            qm = jnp.where(qlow if half == 0 else jnp.logical_not(qlow), qg, jnp.zeros_like(qg))
            s = lax.dot_general(qm, k_dup[kvh], (((1,), (1,)), ((), ())), preferred_element_type=jnp.float32)
            s = jnp.where(valid, s, NEG_INF)
            sink = sink_ref[head]
            m = jnp.maximum(jnp.max(s, axis=1, keepdims=True), sink)
            p = jnp.exp2(s - m)
            l = jnp.sum(p, axis=1, keepdims=True) + jnp.exp2(sink - m)
            o = _dot(p.astype(jnp.bfloat16), v_dup[kvh]) / l
            out = o if half == 0 else jnp.where(qlow, out, o)
        o_ref[:, g * LANES:(g + 1) * LANES] = out.astype(o_ref.dtype)


def _swa(lay, q, k, v, sink2):
    B, S = lay.B, lay.S
    tq = TM
    W = SWA_WINDOW
    per = tq // W
    nq = S // tq
    qw = 2 * (SWA_HEADS // SWA_KV_HEADS) * (LANES // 2)
    ctx_blk = S // lay.C
    kv_specs = [pl.BlockSpec((None, W, LANES), lambda b, p, i: (b, jnp.maximum(i * per - 1, 0), p)),
                pl.BlockSpec((None, tq, LANES), lambda b, p, i: (b, i, p)),
                pl.BlockSpec((None, W, LANES), lambda b, p, i: (b, (i + 1) * per, p)),
                pl.BlockSpec((None, lay.C, LANES), lambda b, p, i: (b, ctx_blk, p))]
    return pl.pallas_call(
        functools.partial(_swa_kernel, lay, tq),
        out_shape=jax.ShapeDtypeStruct((B, S, lay.D), jnp.bfloat16),
        grid=(B, SWA_KV_HEADS // 2, nq),
        in_specs=[pl.BlockSpec(memory_space=pltpu.SMEM),
                  pl.BlockSpec((None, tq, qw), lambda b, p, i: (b, i, p))] + kv_specs + kv_specs,
        out_specs=pl.BlockSpec((None, tq, qw), lambda b, p, i: (b, i, p)),
        compiler_params=_cparams(("arbitrary", "arbitrary", "arbitrary")),
        name="swa",
    )(sink2, q, k, k, k, k, v, v, v, v)


def _outproj_kernel(lay, has_ctx, ol_ref, oc_ref, wo_ref, *rest):
    x = ol_ref[...]
    if has_ctx:
        _, w = lay.split(pl.program_id(0))
        x = jnp.where(w >= lay.LT, oc_ref[...], x)
    _mixer_tail(_dot(x, wo_ref[...]), *rest)


def _out_proj(lay, o_lat, o_ctx, w_out, h, mod, gain, wr, br):
    D = lay.D
    tile = lambda i: i
    has_ctx = o_ctx is not None
    if not has_ctx:
        o_ctx = o_lat
    shapes, specs = _tail_out(lay, tile)

    def lat_idx(i):
        b, w = lay.split(i)
        return b, jnp.minimum(w, lay.LT - 1), 0

    def ctx_idx(i):
        b, w = lay.split(i)
        return b, jnp.clip(w - lay.LT, 0, lay.TPB - lay.LT - 1), 0

    return pl.pallas_call(
        functools.partial(_outproj_kernel, lay, has_ctx),
        out_shape=shapes,
        grid=(lay.NT,),
        in_specs=[pl.BlockSpec((None, TM, D), lat_idx),
                  pl.BlockSpec((None, TM, D), ctx_idx if has_ctx else lat_idx),
                  pl.BlockSpec((D, D), lambda i: (0, 0))] + _tail_in_specs(lay, tile),
        out_specs=specs,
        compiler_params=_cparams(("arbitrary",)),
        name="attn_out",
    )(o_lat, o_ctx, w_out, h, mod, gain.reshape(1, D), mod, mod, wr, br)


def _pool_kernel(lay, u_ref, up_ref, un_ref, pw_ref, ps_ref, *rest):
    i = pl.program_id(0)
    _, w = lay.split(i)
    has_prev, has_next = _seq_flags(lay, i)
    in_ctx = w >= lay.LT
    seq_len = jnp.where(in_ctx, lay.C, lay.S)
    pos0 = jnp.where(in_ctx, w - lay.LT, w) * TM
    G = len(POOL_WINDOWS)
    gw = lay.D // G
    E_ROWS = TM + 2 * SUBLANES_F32
    pos = pos0 + lax.broadcasted_iota(jnp.int32, (TM, gw), 0)
    ys = []
    for g, win in enumerate(POOL_WINDOWS):
        sl = slice(g * gw, (g + 1) * gw)
        u = u_ref[:, sl].astype(jnp.float32)
        before = jnp.where(has_prev, up_ref[HALO - SUBLANES_F32:HALO, sl].astype(jnp.float32), 0.0)
        after = jnp.where(has_next, un_ref[0:SUBLANES_F32, sl].astype(jnp.float32), 0.0)
        e = jnp.concatenate([before, u, after], axis=0)
        left = win // 2
        right = win - 1 - left
        assert left == right + 1 and left & (left - 1) == 0
        acc = e
        span = 1
        while span < left:
            acc = acc + pltpu.roll(acc, E_ROWS - span, 0)
            span *= 2
        tot = pltpu.roll(acc, left, 0) + acc
        total = tot[SUBLANES_F32:SUBLANES_F32 + TM]
        cnt = jnp.minimum(pos + right, seq_len - 1) - jnp.maximum(pos - left, 0) + 1
        mean = total / cnt.astype(jnp.float32)
        ys.append(_dot((mean - u).astype(jnp.bfloat16), pw_ref[g]))
    y = jnp.concatenate(ys, axis=1) * ps_ref[...]
    _mixer_tail(y, *rest)


def _pool(lay, u, pool_w, pool_scale, h, mod, gain, wr, br):
    D = lay.D
    tile = lambda i: i
    prev, nxt = _halo_specs(lay, D)
    shapes, specs = _tail_out(lay, tile)
    return pl.pallas_call(
        functools.partial(_pool_kernel, lay),
        out_shape=shapes,
        grid=(lay.NT,),
        in_specs=[pl.BlockSpec((TM, D), lambda i: (i, 0)), prev, nxt,
                  pl.BlockSpec(pool_w.shape, lambda i: (0, 0, 0)),
                  _row_spec(D)] + _tail_in_specs(lay, tile),
        out_specs=specs,
        compiler_params=_cparams(("arbitrary",)),
        name="pool",
    )(u, u, u, pool_w, pool_scale.reshape(1, D), h, mod, gain.reshape(1, D), mod, mod, wr, br)


def _route_kernel(lg_ref, info_ref, gate_ref, cnt_ref, carry_ref):
    i = pl.program_id(0)

    @pl.when(i == 0)
    def _():
        carry_ref[...] = jnp.zeros(carry_ref.shape, jnp.float32)

    lg = lg_ref[...]
    lane = lax.broadcasted_iota(jnp.int32, lg.shape, 1)
    lane_f = lane.astype(jnp.float32)
    big = jnp.float32(4 * LANES)

    def first_lane(mask):
        return jnp.min(jnp.where(mask, lane_f, big), axis=1, keepdims=True).astype(jnp.int32)

    is_grp = lane < N_GROUPS
    gl = jnp.where(is_grp, lg, NEG_INF)
    gmax = jnp.max(gl, axis=1, keepdims=True)
    grp = first_lane(jnp.logical_and(is_grp, gl == gmax))
    p_grp = 1.0 / jnp.sum(jnp.where(is_grp, jnp.exp(gl - gmax), 0.0), axis=1, keepdims=True)
    eid = lane - N_GROUPS
    in_grp = jnp.logical_and(lane >= N_GROUPS + grp * EXPERTS_PER_GROUP,
                             lane < N_GROUPS + (grp + 1) * EXPERTS_PER_GROUP)
    el = jnp.where(in_grp, lg, NEG_INF)
    t1 = jnp.max(el, axis=1, keepdims=True)
    e1 = first_lane(jnp.logical_and(in_grp, el == t1)) - N_GROUPS
    rest = jnp.logical_and(in_grp, eid != e1)
    el2 = jnp.where(rest, lg, NEG_INF)
    t2 = jnp.max(el2, axis=1, keepdims=True)
    e2 = first_lane(jnp.logical_and(rest, el2 == t2)) - N_GROUPS
    d = jnp.exp(t2 - t1)
    g1 = p_grp / (1.0 + d)
    g2 = p_grp * d / (1.0 + d)

    oh1 = lane == e1
    oh2 = lane == e2
    oh = jnp.where(jnp.logical_or(oh1, oh2), 1.0, 0.0)
    r = lax.broadcasted_iota(jnp.int32, (TM, TM), 0)
    c = lax.broadcasted_iota(jnp.int32, (TM, TM), 1)
    tri = jnp.where(c < r, 1.0, 0.0).astype(jnp.bfloat16)
    before = _dot(tri, oh.astype(jnp.bfloat16)) + carry_ref[0:1, :]
    r1 = jnp.sum(jnp.where(oh1, before, 0.0), axis=1, keepdims=True)
    r2 = jnp.sum(jnp.where(oh2, before, 0.0), axis=1, keepdims=True)
    carry = carry_ref[0:1, :] + jnp.sum(oh, axis=0, keepdims=True)
    carry_ref[...] = jnp.broadcast_to(carry, carry_ref.shape)

    info = jnp.where(lane == 0, e1, jnp.where(lane == 1, e2, jnp.where(
        lane == 2, r1.astype(jnp.int32), jnp.where(lane == 3, r2.astype(jnp.int32), 0))))
    info_ref[...] = info
    gate_ref[...] = jnp.where(lane == 0, g1, jnp.where(lane == 1, g2, 0.0))
    cnt_ref[...] = jnp.broadcast_to(carry, cnt_ref.shape).astype(jnp.int32)


def _route(lay, logits):
    T = lay.T
    return pl.pallas_call(
        _route_kernel,
        out_shape=(jax.ShapeDtypeStruct((T, LANES), jnp.int32),
                   jax.ShapeDtypeStruct((T, LANES), jnp.float32),
                   jax.ShapeDtypeStruct((SUBLANES_F32, LANES), jnp.int32)),
        grid=(lay.NT,),
        in_specs=[pl.BlockSpec((TM, LANES), lambda i: (i, 0))],
        out_specs=(pl.BlockSpec((TM, LANES), lambda i: (i, 0)),
                   pl.BlockSpec((TM, LANES), lambda i: (i, 0)),
                   pl.BlockSpec((SUBLANES_F32, LANES), lambda i: (0, 0))),
        scratch_shapes=[pltpu.VMEM((SUBLANES_F32, LANES), jnp.float32)],
        compiler_params=_cparams(("arbitrary",)),
        name="moe_route",
    )(logits)


def _dispatch_kernel(dest_ref, v_ref, xs_in, xs_hbm, sem):
    del xs_in

    def issue(r, _):
        for k in range(TOP_K):
            pltpu.make_async_copy(v_ref.at[pl.ds(r, 1)], xs_hbm.at[pl.ds(dest_ref[0, k, r], 1)], sem).start()
        return 0

    lax.fori_loop(0, TM, issue, 0, unroll=8)
    for k in range(TOP_K):
        pltpu.make_async_copy(v_ref, xs_hbm.at[pl.ds(0, TM)], sem).wait()


def _dispatch(lay, v, dest, P):
    D = lay.D
    xs0 = jnp.zeros((P, D), jnp.float32)
    return pl.pallas_call(
        _dispatch_kernel,
        out_shape=jax.ShapeDtypeStruct((P, D), jnp.float32),
        grid=(lay.NT,),
        in_specs=[pl.BlockSpec((1, TOP_K, TM), lambda i: (i, 0, 0), memory_space=pltpu.SMEM),
                  pl.BlockSpec((TM, D), lambda i: (i, 0)),
                  pl.BlockSpec(memory_space=pl.ANY)],
        out_specs=pl.BlockSpec(memory_space=pl.ANY),
        scratch_shapes=[pltpu.SemaphoreType.DMA(())],
        input_output_aliases={2: 0},
        compiler_params=pltpu.CompilerParams(dimension_semantics=("arbitrary",), has_side_effects=True),
        name="moe_dispatch",
    )(dest, v, xs0)


def _expert_kernel(be_ref, na_ref, x_ref, wg_ref, wu_ref, wd_ref, y_ref, wgu_s, wd_s):
    i = pl.program_id(0)
    de = wg_ref.shape[1]
    active = i < na_ref[0]

    @pl.when(jnp.logical_and(active, jnp.logical_or(i == 0, be_ref[i] != be_ref[jnp.maximum(i - 1, 0)])))
    def _():
        wgu_s[:, :de] = wg_ref[...].astype(wgu_s.dtype)
        wgu_s[:, de:] = wu_ref[...].astype(wgu_s.dtype)
        wd_s[...] = wd_ref[...].astype(wd_s.dtype)

    @pl.when(active)
    def _():
        hgu = _dot(x_ref[...].astype(jnp.bfloat16), wgu_s[...])
        hg, hu = hgu[:, :de], hgu[:, de:]
        a = (hg * (1.0 / (1.0 + jnp.exp(-hg)))) * hu
        y_ref[...] = _dot(a.astype(jnp.bfloat16), wd_s[...])

    @pl.when(jnp.logical_not(active))
    def _():
        y_ref[...] = jnp.zeros(y_ref.shape, y_ref.dtype)


def _experts(lay, xs, blk_expert, n_active, layer, wg, wu, wd):
    D = lay.D
    P = xs.shape[0]
    DE = wg.shape[3]
    grid_spec = pltpu.PrefetchScalarGridSpec(
        num_scalar_prefetch=2,
        grid=(P // EXPERT_ROWS,),
        in_specs=[pl.BlockSpec((EXPERT_ROWS, D), lambda i, be, na: (i, 0)),
                  pl.BlockSpec((None, None, D, DE), lambda i, be, na: (layer, be[i], 0, 0)),
                  pl.BlockSpec((None, None, D, DE), lambda i, be, na: (layer, be[i], 0, 0)),
                  pl.BlockSpec((None, None, DE, D), lambda i, be, na: (layer, be[i], 0, 0))],
        out_specs=pl.BlockSpec((EXPERT_ROWS, D), lambda i, be, na: (i, 0)),
        scratch_shapes=[pltpu.VMEM((D, 2 * DE), jnp.bfloat16), pltpu.VMEM((DE, D), jnp.bfloat16)],
    )
    return pl.pallas_call(
        _expert_kernel,
        out_shape=jax.ShapeDtypeStruct((P, D), jnp.float32),
        grid_spec=grid_spec,
        compiler_params=_cparams(("arbitrary",)),
        name="moe_experts",
    )(blk_expert, n_active, xs, wg, wu, wd)


def _combine_kernel(final, dest_ref, ys_hbm, gate_ref, h_ref, g2_ref, gain_ref, sc_ref, sh_ref, *refs):
    *out_refs, ybuf, sem = refs

    def issue(r, _):
        for k in range(TOP_K):
            pltpu.make_async_copy(ys_hbm.at[pl.ds(dest_ref[0, k, r], 1)], ybuf.at[k, pl.ds(r, 1)], sem).start()
        return 0

    lax.fori_loop(0, TM, issue, 0, unroll=8)
    for k in range(TOP_K):
        pltpu.make_async_copy(ys_hbm.at[pl.ds(0, TM)], ybuf.at[k], sem).wait()
    gate = gate_ref[...]
    f = gate[:, 0:1] * ybuf[0] + gate[:, 1:2] * ybuf[1]
    h2 = h_ref[...] + g2_ref[...] * f
    if final:
        (out_ref,) = out_refs
        out_ref[...] = _rms(h2, gain_ref[...])
    else:
        h2_ref, u_ref = out_refs
        h2_ref[...] = h2
        u_ref[...] = (_rms(h2, gain_ref[...]) * (1.0 + sc_ref[...]) + sh_ref[...]).astype(u_ref.dtype)


def _combine(lay, ys, dest, gates, h1, mod, mod_next, gain_next, final):
    D = lay.D
    row_tile = pl.BlockSpec((TM, D), lambda i: (i, 0))
    if final:
        n = lay.B * lay.LT
        tile = lambda i: (i // lay.LT) * lay.TPB + i % lay.LT
        out_shape = (jax.ShapeDtypeStruct((lay.B * lay.S, D), jnp.float32),)
        out_specs = (row_tile,)
    else:
        n = lay.NT
        tile = lambda i: i
        out_shape = (jax.ShapeDtypeStruct((lay.T, D), jnp.float32),
                     jax.ShapeDtypeStruct((lay.T, D), jnp.bfloat16))
        out_specs = (row_tile, row_tile)
    return pl.pallas_call(
        functools.partial(_combine_kernel, final),
        out_shape=out_shape,
        grid=(n,),
        in_specs=[pl.BlockSpec((1, TOP_K, TM), lambda i: (tile(i), 0, 0), memory_space=pltpu.SMEM),
                  pl.BlockSpec(memory_space=pl.ANY),
                  pl.BlockSpec((TM, LANES), lambda i: (tile(i), 0)),
                  pl.BlockSpec((TM, D), lambda i: (tile(i), 0)),
                  _mod_spec(lay, 5, tile),
                  _row_spec(D),
                  _mod_spec(lay, 1, tile), _mod_spec(lay, 0, tile)],
        out_specs=out_specs,
        scratch_shapes=[pltpu.VMEM((TOP_K, TM, D), jnp.float32), pltpu.SemaphoreType.DMA(())],
        compiler_params=_cparams(("arbitrary",)),
        name="moe_combine_final" if final else "moe_combine",
    )(dest, ys, gates, h1, mod, gain_next.reshape(1, D), mod_next, mod_next)


def _moe(lay, v, logits, h1, mod, mod_next, gain_next, layer, wg, wu, wd, final):
    T, NT = lay.T, lay.NT
    info, gates, cnt = _route(lay, logits)
    counts = cnt[0, :N_EXPERTS]
    padded = ((counts + EXPERT_ROWS - 1) // EXPERT_ROWS) * EXPERT_ROWS
    pad_end = jnp.cumsum(padded)
    pad_start = pad_end - padded
    expert = info[:, :TOP_K]
    onehot = expert[:, :, None] == jnp.arange(N_EXPERTS, dtype=jnp.int32)
    dest = info[:, TOP_K:2 * TOP_K] + jnp.sum(jnp.where(onehot, pad_start, 0), axis=-1)
    dest = dest.reshape(NT, TM, TOP_K).transpose(0, 2, 1).astype(jnp.int32)
    n_blocks = -(-(T * TOP_K + N_EXPERTS * (EXPERT_ROWS - 1)) // EXPERT_ROWS)
    blk_start = jnp.arange(n_blocks, dtype=jnp.int32) * EXPERT_ROWS
    blk_expert = jnp.minimum(jnp.sum(pad_end[None, :] <= blk_start[:, None], axis=1),
                             N_EXPERTS - 1).astype(jnp.int32)
    n_active = (pad_end[-1:] // EXPERT_ROWS).astype(jnp.int32)
    xs = _dispatch(lay, v, dest, n_blocks * EXPERT_ROWS)
    ys = _experts(lay, xs, blk_expert, n_active, layer, wg, wu, wd)
    return _combine(lay, ys, dest, gates, h1, mod, mod_next, gain_next, final)


def kernel(x, c, ctx, c_ctx, ada_w, ada_b, norm_mix, norm_ffn, norm_final, conv_in, conv_k, conv_out, gqa_qkv, gqa_q_gain, gqa_k_gain, gqa_out, pool_w, pool_scale, swa_qkv, swa_sink, swa_out, router_grp_w, router_grp_b, router_exp_w, router_exp_b, exp_gate, exp_up, exp_down):
    B, S, D = x.shape
    C = ctx.shape[1]
    L = ada_w.shape[0]
    lay = Layout(B, S, C, D)
    bf = jnp.bfloat16

    R = SUBLANES_BF16
    cvec = jnp.zeros((R, D), jnp.float32).at[:B].set(c).at[B].set(c_ctx)
    mod_all = _modulation(cvec, ada_w, ada_b)[:, :B + 1].reshape(L, B + 1, 6, 1, D).transpose(0, 2, 1, 3, 4)

    pad = LANES - N_GROUPS - N_EXPERTS
    wr_all = jnp.concatenate([router_grp_w, router_exp_w, jnp.zeros((L, D, pad), jnp.float32)], axis=-1).astype(bf)
    br_all = jnp.concatenate([router_grp_b, router_exp_b, jnp.zeros((L, pad), jnp.float32)], axis=-1)

    h = jnp.concatenate([x, ctx], axis=1).reshape(lay.T, D)
    u = _prenorm(lay, h, norm_mix[0], mod_all[0])
    one = jnp.ones((1, LANES), jnp.float32)

    for i in range(L):
        m, j = i % 4, i // 4
        mod = mod_all[i]
        wr, br = wr_all[i], br_all[i].reshape(1, LANES)
        tail = (h, mod, norm_ffn[i], wr, br)
        if m == 0:
            bg, z = _conv_in(lay, u, conv_in[j].astype(bf))
            h1, v, lg = _conv_out(lay, z, bg, conv_k[j], conv_out[j].astype(bf), *tail)
        elif m == 1:
            hd = D // GQA_HEADS
            w = gqa_qkv[j].astype(bf)
            cos_t, sin_t = _rope_tables(lay, hd)
            nq, nk = GQA_HEADS * hd, GQA_KV_HEADS * hd
            qg = gqa_q_gain[j].reshape(1, hd)
            kg = gqa_k_gain[j].reshape(1, hd)
            q = _project(lay, u, w, 0, nq, hd, qg, cos_t, sin_t, True, True, hd ** -0.5 * LOG2E, "gqa_q")
            k = _project(lay, u, w, nq, nk, hd, kg, cos_t, sin_t, True, True, 1.0, "gqa_k")
            vv = _project(lay, u, w, nq + nk, nk, hd, one, cos_t, sin_t, False, False, 1.0, "gqa_v")
            q3, k3, v3 = (a.reshape(B, lay.SB, -1) for a in (q, k, vv))
            n_rep = GQA_HEADS // GQA_KV_HEADS
            tq = 512 if S % 512 == 0 else TM
            tk = 768 if lay.SB % 768 == 0 else TM
            o_lat = _flash(lay, q3, k3, v3, GQA_KV_HEADS, n_rep, tq, tk, 0, S // tq, lay.SB, 0, "gqa_flash")
            o_ctx = _flash(lay, q3, k3, v3, GQA_KV_HEADS, n_rep, C, C, S // C, 1, C, S // C, "gqa_flash_ctx")
            h1, v, lg = _out_proj(lay, o_lat, o_ctx, gqa_out[j].astype(bf), *tail)
        elif m == 2:
            h1, v, lg = _pool(lay, u, pool_w[j].astype(bf), pool_scale[j], *tail)
        else:
            hd = D // SWA_HEADS
            w = swa_qkv[j].astype(bf)
            cos_t, sin_t = _rope_tables(lay, hd)
            nq, nk = SWA_HEADS * hd, SWA_KV_HEADS * hd
            q = _project(lay, u, w, 0, nq, hd, one, cos_t, sin_t, False, True, hd ** -0.5 * LOG2E, "swa_q")
            k = _project(lay, u, w, nq, nk, hd, one, cos_t, sin_t, False, True, 1.0, "swa_k")
            vv = _project(lay, u, w, nq + nk, nk, hd, one, cos_t, sin_t, False, False, 1.0, "swa_v")
            q3, k3, v3 = (a.reshape(B, lay.SB, -1) for a in (q, k, vv))
            o_lat = _swa(lay, q3, k3, v3, swa_sink[j] * LOG2E)
            h1, v, lg = _out_proj(lay, o_lat, None, swa_out[j].astype(bf), *tail)
        final = i == L - 1
        mod_next = mod if final else mod_all[i + 1]
        gain_next = norm_final if final else norm_mix[i + 1]
        res = _moe(lay, v, lg, h1, mod, mod_next, gain_next, i, exp_gate, exp_up, exp_down, final)
        if final:
            return res[0].reshape(B, S, D)
        h, u = res
```

```python
import functools

import jax
import jax.numpy as jnp
from jax import lax
from jax.experimental import pallas as pl
from jax.experimental.pallas import tpu as pltpu

GRID_W = 64
NORM_EPS = 1e-6
ROPE_THETA = 10000.0
NEG_INF = -1e30
GQA_HEADS, GQA_KV_HEADS = 16, 4
SWA_HEADS, SWA_KV_HEADS, SWA_WINDOW = 32, 8, 128
POOL_WINDOWS = (2, 4, 8, 16)
N_GROUPS, EXPERTS_PER_GROUP, TOP_K = 8, 4, 2
N_EXPERTS = N_GROUPS * EXPERTS_PER_GROUP
LOG2E = 1.4426950408889634

LANES = 128
SUBLANES_F32 = 8
SUBLANES_BF16 = 16
VMEM_LIMIT = 56 * 1024 * 1024
TM = 256
EXPERT_ROWS = 256
HALO = SUBLANES_BF16


def _cparams(sem):
    return pltpu.CompilerParams(dimension_semantics=sem, vmem_limit_bytes=VMEM_LIMIT)


class Layout:
    def __init__(self, B, S, C, D):
        assert S % TM == 0 and C % TM == 0
        self.B, self.S, self.C, self.D = B, S, C, D
        self.SB = S + C
        self.T = B * self.SB
        self.LT = S // TM
        self.TPB = self.SB // TM
        self.NT = B * self.TPB

    def split(self, i):
        return i // self.TPB, i % self.TPB

    def mod_row(self, i):
        b, w = self.split(i)
        return jnp.where(w >= self.LT, self.B, b)


def _mod_spec(lay, chunk, tile_of):
    return pl.BlockSpec((None, None, 1, lay.D),
                        lambda *g: (chunk, lay.mod_row(tile_of(*g)), 0, 0))


def _row_spec(D):
    return pl.BlockSpec((1, D), lambda *g: (0, 0))


def _rms(x, gain):
    ms = jnp.mean(x * x, axis=-1, keepdims=True)
    return (x * lax.rsqrt(ms + NORM_EPS)) * gain


def _dot(a, b):
    return jnp.dot(a, b, preferred_element_type=jnp.float32)


def _mod_kernel(c_ref, w_ref, b_ref, o_ref):
    c = c_ref[...]
    a = (c * (1.0 / (1.0 + jnp.exp(-c)))).astype(jnp.bfloat16)
    o_ref[...] = _dot(a, w_ref[...].astype(jnp.bfloat16)) + b_ref[...]


def _modulation(cvec, ada_w, ada_b):
    L, D, N = ada_w.shape
    R = cvec.shape[0]
    tn = 512
    return pl.pallas_call(
        _mod_kernel,
        out_shape=jax.ShapeDtypeStruct((L, R, N), jnp.float32),
        grid=(L, N // tn),
        in_specs=[pl.BlockSpec((R, D), lambda l, j: (0, 0)),
                  pl.BlockSpec((None, D, tn), lambda l, j: (l, 0, j)),
                  pl.BlockSpec((None, 1, tn), lambda l, j: (l, 0, j))],
        out_specs=pl.BlockSpec((None, R, tn), lambda l, j: (l, 0, j)),
        compiler_params=_cparams(("arbitrary", "arbitrary")),
        name="adaln_mod",
    )(cvec, ada_w, ada_b.reshape(L, 1, N))


def _prenorm_kernel(h_ref, gain_ref, sc_ref, sh_ref, u_ref):
    u_ref[...] = (_rms(h_ref[...], gain_ref[...]) * (1.0 + sc_ref[...]) + sh_ref[...]).astype(u_ref.dtype)


def _prenorm(lay, h, gain, mod):
    D = lay.D
    tile = lambda i: i
    return pl.pallas_call(
        _prenorm_kernel,
        out_shape=jax.ShapeDtypeStruct((lay.T, D), jnp.bfloat16),
        grid=(lay.NT,),
        in_specs=[pl.BlockSpec((TM, D), lambda i: (i, 0)), _row_spec(D),
                  _mod_spec(lay, 1, tile), _mod_spec(lay, 0, tile)],
        out_specs=pl.BlockSpec((TM, D), lambda i: (i, 0)),
        compiler_params=_cparams(("arbitrary",)),
        name="prenorm",
    )(h, gain.reshape(1, D), mod, mod)


def _pack_rows(x):
    w = x.shape[1] // 2
    hi = pltpu.bitcast(x[:, :w].astype(jnp.bfloat16).astype(jnp.float32), jnp.uint32)
    lo = pltpu.bitcast(x[:, w:].astype(jnp.bfloat16).astype(jnp.float32), jnp.uint32)
    return hi | (lo >> 16)


def _unpack_rows(words):
    hi = pltpu.bitcast(words & jnp.uint32(0xFFFF0000), jnp.float32)
    lo = pltpu.bitcast(words << 16, jnp.float32)
    return hi, lo


def _mixer_tail(y, h_ref, g1_ref, gain_ref, sc_ref, sh_ref, wr_ref, br_ref, h1_ref, v_ref, lg_ref):
    h1 = h_ref[...] + g1_ref[...] * y
    h1_ref[...] = h1
    v = _rms(h1, gain_ref[...]) * (1.0 + sc_ref[...]) + sh_ref[...]
    v_ref[...] = _pack_rows(v)
    lg_ref[...] = _dot(v.astype(jnp.bfloat16), wr_ref[...]) + br_ref[...]


def _tail_in_specs(lay, tile):
    D = lay.D
    return [pl.BlockSpec((TM, D), lambda *g: (tile(*g), 0)),
            _mod_spec(lay, 2, tile),
            _row_spec(D),
            _mod_spec(lay, 4, tile), _mod_spec(lay, 3, tile),
            pl.BlockSpec((D, LANES), lambda *g: (0, 0)),
            pl.BlockSpec((1, LANES), lambda *g: (0, 0))]


def _tail_out(lay, tile):
    D = lay.D
    shapes = (jax.ShapeDtypeStruct((lay.T, D), jnp.float32),
              jax.ShapeDtypeStruct((lay.T, D // 2), jnp.uint32),
              jax.ShapeDtypeStruct((lay.T, LANES), jnp.float32))
    specs = (pl.BlockSpec((TM, D), lambda *g: (tile(*g), 0)),
             pl.BlockSpec((TM, D // 2), lambda *g: (tile(*g), 0)),
             pl.BlockSpec((TM, LANES), lambda *g: (tile(*g), 0)))
    return shapes, specs


def _seq_flags(lay, i):
    _, w = lay.split(i)
    has_prev = jnp.logical_and(w != 0, w != lay.LT)
    has_next = jnp.logical_and(w != lay.LT - 1, w != lay.TPB - 1)
    return has_prev, has_next


def _halo_specs(lay, width, col=lambda *g: 0, tile=lambda i: i):
    per = TM // HALO
    last = lay.T // HALO - 1
    prev = pl.BlockSpec((HALO, width), lambda *g: (jnp.maximum(tile(*g) * per - 1, 0), col(*g)))
    nxt = pl.BlockSpec((HALO, width), lambda *g: (jnp.minimum((tile(*g) + 1) * per, last), col(*g)))
    return prev, nxt


def _convin_kernel(u_ref, wb_ref, wc_ref, wx_ref, bg_ref, z_ref):
    x = u_ref[...]
    bg_ref[...] = _dot(x, wb_ref[...]).astype(bg_ref.dtype)
    z_ref[...] = (_dot(x, wc_ref[...]) * _dot(x, wx_ref[...])).astype(z_ref.dtype)


def _conv_in(lay, u, w_in):
    D = lay.D
    tn = 512
    nb = D // tn
    out = jax.ShapeDtypeStruct((lay.T, D), jnp.bfloat16)
    return pl.pallas_call(
        _convin_kernel,
        out_shape=(out, out),
        grid=(nb, lay.NT),
        in_specs=[pl.BlockSpec((TM, D), lambda j, i: (i, 0)),
                  pl.BlockSpec((D, tn), lambda j, i: (0, j)),
                  pl.BlockSpec((D, tn), lambda j, i: (0, nb + j)),
                  pl.BlockSpec((D, tn), lambda j, i: (0, 2 * nb + j))],
        out_specs=(pl.BlockSpec((TM, tn), lambda j, i: (i, j)),
                   pl.BlockSpec((TM, tn), lambda j, i: (i, j))),
        compiler_params=_cparams(("arbitrary", "arbitrary")),
        name="conv_in",
    )(u, w_in, w_in, w_in)


def _convout_kernel(lay, z_ref, zp_ref, zn_ref, bg_ref, ck_ref, wo_ref, *rest):
    i = pl.program_id(0)
    has_prev, has_next = _seq_flags(lay, i)
    z = z_ref[...].astype(jnp.float32)
    row = lax.broadcasted_iota(jnp.int32, z.shape, 0)
    prev_row = jnp.where(has_prev, zp_ref[HALO - 1:HALO, :].astype(jnp.float32), 0.0)
    next_row = jnp.where(has_next, zn_ref[0:1, :].astype(jnp.float32), 0.0)
    z_m1 = jnp.where(row == 0, prev_row, pltpu.roll(z, 1, 0))
    z_p1 = jnp.where(row == TM - 1, next_row, pltpu.roll(z, TM - 1, 0))
    y = z_m1 * ck_ref[0:1, :] + z * ck_ref[1:2, :] + z_p1 * ck_ref[2:3, :]
    g = (bg_ref[...].astype(jnp.float32) * y).astype(jnp.bfloat16)
    _mixer_tail(_dot(g, wo_ref[...]), *rest)


def _conv_out(lay, z, bg, conv_k, w_out, h, mod, gain, wr, br):
    D = lay.D
    tile = lambda i: i
    prev, nxt = _halo_specs(lay, D)
    shapes, specs = _tail_out(lay, tile)
    return pl.pallas_call(
        functools.partial(_convout_kernel, lay),
        out_shape=shapes,
        grid=(lay.NT,),
        in_specs=[pl.BlockSpec((TM, D), lambda i: (i, 0)), prev, nxt,
                  pl.BlockSpec((TM, D), lambda i: (i, 0)),
                  pl.BlockSpec(conv_k.shape, lambda i: (0, 0)),
                  pl.BlockSpec((D, D), lambda i: (0, 0))] + _tail_in_specs(lay, tile),
        out_specs=specs,
        compiler_params=_cparams(("arbitrary",)),
        name="conv_out",
    )(z, z, z, bg, conv_k, w_out, h, mod, gain.reshape(1, D), mod, mod, wr, br)


def _proj_kernel(head_dim, use_norm, use_rope, scale, u_ref, w_ref, gain_ref, cos_ref, sin_ref, o_ref):
    y = _dot(u_ref[...], w_ref[...])
    tn = y.shape[1]
    lane = lax.broadcasted_iota(jnp.int32, (TM, LANES), 1)
    for g in range(tn // LANES):
        yg = y[:, g * LANES:(g + 1) * LANES]
        if use_norm:
            yg = _rms(yg, gain_ref[...])
        if use_rope:
            if head_dim == LANES:
                rot = pltpu.roll(yg, LANES // 2, 1)
            else:
                q = head_dim // 2
                rot = jnp.where(lane % head_dim < q, pltpu.roll(yg, LANES - q, 1), pltpu.roll(yg, q, 1))
            yg = yg * cos_ref[...] + rot * sin_ref[...]
        if scale != 1.0:
            yg = yg * scale
        o_ref[:, g * LANES:(g + 1) * LANES] = yg.astype(o_ref.dtype)


def _project(lay, u, w, col0, ncols, head_dim, gain, cos_t, sin_t, use_norm, use_rope, scale, name):
    D = lay.D
    tn = 512
    nb = ncols // tn
    cb0 = col0 // tn
    kern = functools.partial(_proj_kernel, head_dim, use_norm, use_rope, scale)
    return pl.pallas_call(
        kern,
        out_shape=jax.ShapeDtypeStruct((lay.T, ncols), jnp.bfloat16),
        grid=(nb, lay.NT),
        in_specs=[pl.BlockSpec((TM, D), lambda j, i: (i, 0)),
                  pl.BlockSpec((D, tn), lambda j, i: (0, cb0 + j)),
                  pl.BlockSpec((1, LANES), lambda j, i: (0, 0)),
                  pl.BlockSpec((TM, LANES), lambda j, i: (i % lay.TPB, 0)),
                  pl.BlockSpec((TM, LANES), lambda j, i: (i % lay.TPB, 0))],
        out_specs=pl.BlockSpec((TM, tn), lambda j, i: (i, j)),
        compiler_params=_cparams(("arbitrary", "arbitrary")),
        name=name,
    )(u, w, gain, cos_t, sin_t)


def _rope_tables(lay, head_dim):
    quarter = head_dim // 4
    rows = lay.S // GRID_W
    row = jnp.repeat(jnp.arange(rows), GRID_W).astype(jnp.float32)
    col = jnp.tile(jnp.arange(GRID_W), rows).astype(jnp.float32)
    inv = ROPE_THETA ** (-jnp.arange(quarter, dtype=jnp.float32) / quarter)
    ang = jnp.concatenate([row[:, None] * inv, col[:, None] * inv], axis=-1)
    cos, sin = jnp.cos(ang), jnp.sin(ang)
    reps = LANES // head_dim
    cos_t = jnp.tile(jnp.concatenate([cos, cos], axis=-1), (1, reps))
    sin_t = jnp.tile(jnp.concatenate([-sin, sin], axis=-1), (1, reps))
    cos_t = jnp.concatenate([cos_t, jnp.ones((lay.C, LANES), jnp.float32)], axis=0)
    sin_t = jnp.concatenate([sin_t, jnp.zeros((lay.C, LANES), jnp.float32)], axis=0)
    return cos_t, sin_t


def _flash_kernel(n_rep, tk, q_ref, k_ref, v_ref, o_ref, vx_ref, s_ref, m_ref, l_ref, acc_ref):
    hd = LANES
    nk = k_ref.shape[0] // tk

    @pl.when(pl.program_id(2) == 0)
    def _():
        vx_ref[:, :hd] = v_ref[...]
        vx_ref[:, hd:] = jnp.ones((vx_ref.shape[0], hd), vx_ref.dtype)

    m_ref[...] = jnp.full(m_ref.shape, NEG_INF, jnp.float32)
    l_ref[...] = jnp.zeros(l_ref.shape, jnp.float32)
    acc_ref[...] = jnp.zeros(acc_ref.shape, jnp.float32)

    def rows(j):
        start = j * tk
        return pl.ds(start if isinstance(start, int) else pl.multiple_of(start, tk), tk)

    def step(j, slot, with_next):
        vx = vx_ref[rows(j), :]
        if with_next:
            k_next = k_ref[rows(j + 1), :]
        for h in range(n_rep):
            if with_next:
                q = q_ref[:, h * hd:(h + 1) * hd]
                s_ref[1 - slot, h] = lax.dot_general(q, k_next, (((1,), (1,)), ((), ())),
                                                     preferred_element_type=jnp.float32)
            s = s_ref[slot, h]
            m_prev = m_ref[h]
            m_next = jnp.maximum(m_prev, jnp.max(s, axis=1, keepdims=True))
            alpha = jnp.exp2(m_prev - m_next)
            p = jnp.exp2(s - m_next[:, :1]).astype(jnp.bfloat16)
            pv = _dot(p, vx)
            m_ref[h] = m_next
            l_ref[h] = alpha * l_ref[h] + pv[:, hd:]
            acc_ref[h] = alpha * acc_ref[h] + pv[:, :hd]

    k0 = k_ref[rows(0), :]
    for h in range(n_rep):
        s_ref[0, h] = lax.dot_general(q_ref[:, h * hd:(h + 1) * hd], k0, (((1,), (1,)), ((), ())),
                                      preferred_element_type=jnp.float32)

    n_pairs = (nk - 1) // 2

    def body(t, _):
        step(2 * t, 0, True)
        step(2 * t + 1, 1, True)
        return 0

    lax.fori_loop(0, n_pairs, body, 0)
    for j in range(2 * n_pairs, nk):
        step(j, j % 2, j + 1 < nk)
    for h in range(n_rep):
        o_ref[:, h * hd:(h + 1) * hd] = (acc_ref[h] / l_ref[h]).astype(o_ref.dtype)


def _flash(lay, q, k, v, n_kv, n_rep, tq, tk, q_blk0, nq, kb, k_blk0, name):
    B = lay.B
    qw = n_rep * LANES
    return pl.pallas_call(
        functools.partial(_flash_kernel, n_rep, tk),
        out_shape=jax.ShapeDtypeStruct((B, nq * tq, n_kv * qw), jnp.bfloat16),
        grid=(B, n_kv, nq),
        in_specs=[pl.BlockSpec((None, tq, qw), lambda b, g, i: (b, q_blk0 + i, g)),
                  pl.BlockSpec((None, kb, LANES), lambda b, g, i: (b, k_blk0, g)),
                  pl.BlockSpec((None, kb, LANES), lambda b, g, i: (b, k_blk0, g))],
        out_specs=pl.BlockSpec((None, tq, qw), lambda b, g, i: (b, i, g)),
        scratch_shapes=[pltpu.VMEM((kb, 2 * LANES), jnp.bfloat16),
                        pltpu.VMEM((2, n_rep, tq, tk), jnp.float32),
                        pltpu.VMEM((n_rep, tq, LANES), jnp.float32),
                        pltpu.VMEM((n_rep, tq, LANES), jnp.float32),
                        pltpu.VMEM((n_rep, tq, LANES), jnp.float32)],
        compiler_params=_cparams(("arbitrary", "arbitrary", "arbitrary")),
        name=name,
    )(q, k, v)


def _swa_kernel(lay, tq, sink_ref, q_ref, kp_ref, kc_ref, kn_ref, kx_ref, vp_ref, vc_ref, vn_ref, vx_ref, o_ref):
    pair = pl.program_id(1)
    qb = pl.program_id(2)
    hd = LANES // 2
    n_rep = SWA_HEADS // SWA_KV_HEADS
    W = SWA_WINDOW
    kk = jnp.concatenate([kp_ref[...], kc_ref[...], kn_ref[...], kx_ref[...]], axis=0)
    vv = jnp.concatenate([vp_ref[...], vc_ref[...], vn_ref[...], vx_ref[...]], axis=0)
    nkeys = kk.shape[0]
    nwin = tq + 2 * W
    lane = lax.broadcasted_iota(jnp.int32, (nkeys, LANES), 1)
    low = lane < hd
    kk_sw = pltpu.roll(kk.astype(jnp.float32), hd, 1).astype(kk.dtype)
    vv_sw = pltpu.roll(vv.astype(jnp.float32), hd, 1).astype(vv.dtype)
    k_dup = (jnp.where(low, kk, kk_sw), jnp.where(low, kk_sw, kk))
    v_dup = (jnp.where(low, vv, vv_sw), jnp.where(low, vv_sw, vv))

    start = qb * tq
    qpos = start + lax.broadcasted_iota(jnp.int32, (tq, nkeys), 0)
    col = lax.broadcasted_iota(jnp.int32, (tq, nkeys), 1)
    kpos = start - W + col
    in_win = jnp.logical_and(jnp.abs(qpos - kpos) <= W, jnp.logical_and(kpos >= 0, kpos < lay.S))
    valid = jnp.logical_or(col >= nwin, in_win)

    qlane = lax.broadcasted_iota(jnp.int32, (tq, LANES), 1)
    qlow = qlane < hd
    for g in range(2 * n_rep * hd // LANES):
        qg = q_ref[:, g * LANES:(g + 1) * LANES]
        kvh = (2 * g) // n_rep
        out = None
        for half in range(2):
            head = pair * 2 * n_rep + 2 * g + half
            qm = jnp.where(qlow if half == 0 else jnp.logical_not(qlow), qg, jnp.zeros_like(qg))
            s = lax.dot_general(qm, k_dup[kvh], (((1,), (1,)), ((), ())), preferred_element_type=jnp.float32)
            s = jnp.where(valid, s, NEG_INF)
            sink = sink_ref[head]
            m = jnp.maximum(jnp.max(s, axis=1, keepdims=True), sink)
            p = jnp.exp2(s - m)
            l = jnp.sum(p, axis=1, keepdims=True) + jnp.exp2(sink - m)
            o = _dot(p.astype(jnp.bfloat16), v_dup[kvh]) / l
            out = o if half == 0 else jnp.where(qlow, out, o)
        o_ref[:, g * LANES:(g + 1) * LANES] = out.astype(o_ref.dtype)


def _swa(lay, q, k, v, sink2):
    B, S = lay.B, lay.S
    tq = TM
    W = SWA_WINDOW
    per = tq // W
    nq = S // tq
    qw = 2 * (SWA_HEADS // SWA_KV_HEADS) * (LANES // 2)
    ctx_blk = S // lay.C
    kv_specs = [pl.BlockSpec((None, W, LANES), lambda b, p, i: (b, jnp.maximum(i * per - 1, 0), p)),
                pl.BlockSpec((None, tq, LANES), lambda b, p, i: (b, i, p)),
                pl.BlockSpec((None, W, LANES), lambda b, p, i: (b, (i + 1) * per, p)),
                pl.BlockSpec((None, lay.C, LANES), lambda b, p, i: (b, ctx_blk, p))]
    return pl.pallas_call(
        functools.partial(_swa_kernel, lay, tq),
        out_shape=jax.ShapeDtypeStruct((B, S, lay.D), jnp.bfloat16),
        grid=(B, SWA_KV_HEADS // 2, nq),
        in_specs=[pl.BlockSpec(memory_space=pltpu.SMEM),
                  pl.BlockSpec((None, tq, qw), lambda b, p, i: (b, i, p))] + kv_specs + kv_specs,
        out_specs=pl.BlockSpec((None, tq, qw), lambda b, p, i: (b, i, p)),
        compiler_params=_cparams(("arbitrary", "arbitrary", "arbitrary")),
        name="swa",
    )(sink2, q, k, k, k, k, v, v, v, v)


def _outproj_kernel(lay, has_ctx, ol_ref, oc_ref, wo_ref, *rest):
    x = ol_ref[...]
    if has_ctx:
        _, w = lay.split(pl.program_id(0))
        x = jnp.where(w >= lay.LT, oc_ref[...], x)
    _mixer_tail(_dot(x, wo_ref[...]), *rest)


def _out_proj(lay, o_lat, o_ctx, w_out, h, mod, gain, wr, br):
    D = lay.D
    tile = lambda i: i
    has_ctx = o_ctx is not None
    if not has_ctx:
        o_ctx = o_lat
    shapes, specs = _tail_out(lay, tile)

    def lat_idx(i):
        b, w = lay.split(i)
        return b, jnp.minimum(w, lay.LT - 1), 0

    def ctx_idx(i):
        b, w = lay.split(i)
        return b, jnp.clip(w - lay.LT, 0, lay.TPB - lay.LT - 1), 0

    return pl.pallas_call(
        functools.partial(_outproj_kernel, lay, has_ctx),
        out_shape=shapes,
        grid=(lay.NT,),
        in_specs=[pl.BlockSpec((None, TM, D), lat_idx),
                  pl.BlockSpec((None, TM, D), ctx_idx if has_ctx else lat_idx),
                  pl.BlockSpec((D, D), lambda i: (0, 0))] + _tail_in_specs(lay, tile),
        out_specs=specs,
        compiler_params=_cparams(("arbitrary",)),
        name="attn_out",
    )(o_lat, o_ctx, w_out, h, mod, gain.reshape(1, D), mod, mod, wr, br)


def _pool_kernel(lay, u_ref, up_ref, un_ref, pw_ref, ps_ref, *rest):
    i = pl.program_id(0)
    _, w = lay.split(i)
    has_prev, has_next = _seq_flags(lay, i)
    in_ctx = w >= lay.LT
    seq_len = jnp.where(in_ctx, lay.C, lay.S)
    pos0 = jnp.where(in_ctx, w - lay.LT, w) * TM
    G = len(POOL_WINDOWS)
    gw = lay.D // G
    E_ROWS = TM + 2 * SUBLANES_F32
    pos = pos0 + lax.broadcasted_iota(jnp.int32, (TM, gw), 0)
    ys = []
    for g, win in enumerate(POOL_WINDOWS):
        sl = slice(g * gw, (g + 1) * gw)
        u = u_ref[:, sl].astype(jnp.float32)
        before = jnp.where(has_prev, up_ref[HALO - SUBLANES_F32:HALO, sl].astype(jnp.float32), 0.0)
        after = jnp.where(has_next, un_ref[0:SUBLANES_F32, sl].astype(jnp.float32), 0.0)
        e = jnp.concatenate([before, u, after], axis=0)
        left = win // 2
        right = win - 1 - left
        assert left == right + 1 and left & (left - 1) == 0
        acc = e
        span = 1
        while span < left:
            acc = acc + pltpu.roll(acc, E_ROWS - span, 0)
            span *= 2
        tot = pltpu.roll(acc, left, 0) + acc
        total = tot[SUBLANES_F32:SUBLANES_F32 + TM]
        cnt = jnp.minimum(pos + right, seq_len - 1) - jnp.maximum(pos - left, 0) + 1
        mean = total / cnt.astype(jnp.float32)
        ys.append(_dot((mean - u).astype(jnp.bfloat16), pw_ref[g]))
    y = jnp.concatenate(ys, axis=1) * ps_ref[...]
    _mixer_tail(y, *rest)


def _pool(lay, u, pool_w, pool_scale, h, mod, gain, wr, br):
    D = lay.D
    tile = lambda i: i
    prev, nxt = _halo_specs(lay, D)
    shapes, specs = _tail_out(lay, tile)
    return pl.pallas_call(
        functools.partial(_pool_kernel, lay),
        out_shape=shapes,
        grid=(lay.NT,),
        in_specs=[pl.BlockSpec((TM, D), lambda i: (i, 0)), prev, nxt,
                  pl.BlockSpec(pool_w.shape, lambda i: (0, 0, 0)),
                  _row_spec(D)] + _tail_in_specs(lay, tile),
        out_specs=specs,
        compiler_params=_cparams(("arbitrary",)),
        name="pool",
    )(u, u, u, pool_w, pool_scale.reshape(1, D), h, mod, gain.reshape(1, D), mod, mod, wr, br)


def _route_kernel(lg_ref, info_ref, gate_ref, cnt_ref, carry_ref):
    i = pl.program_id(0)

    @pl.when(i == 0)
    def _():
        carry_ref[...] = jnp.zeros(carry_ref.shape, jnp.float32)

    lg = lg_ref[...]
    lane = lax.broadcasted_iota(jnp.int32, lg.shape, 1)
    lane_f = lane.astype(jnp.float32)
    big = jnp.float32(4 * LANES)

    def first_lane(mask):
        return jnp.min(jnp.where(mask, lane_f, big), axis=1, keepdims=True).astype(jnp.int32)

    is_grp = lane < N_GROUPS
    gl = jnp.where(is_grp, lg, NEG_INF)
    gmax = jnp.max(gl, axis=1, keepdims=True)
    grp = first_lane(jnp.logical_and(is_grp, gl == gmax))
    p_grp = 1.0 / jnp.sum(jnp.where(is_grp, jnp.exp(gl - gmax), 0.0), axis=1, keepdims=True)
    eid = lane - N_GROUPS
    in_grp = jnp.logical_and(lane >= N_GROUPS + grp * EXPERTS_PER_GROUP,
                             lane < N_GROUPS + (grp + 1) * EXPERTS_PER_GROUP)
    el = jnp.where(in_grp, lg, NEG_INF)
    t1 = jnp.max(el, axis=1, keepdims=True)
    e1 = first_lane(jnp.logical_and(in_grp, el == t1)) - N_GROUPS
    rest = jnp.logical_and(in_grp, eid != e1)
    el2 = jnp.where(rest, lg, NEG_INF)
    t2 = jnp.max(el2, axis=1, keepdims=True)
    e2 = first_lane(jnp.logical_and(rest, el2 == t2)) - N_GROUPS
    d = jnp.exp(t2 - t1)
    g1 = p_grp / (1.0 + d)
    g2 = p_grp * d / (1.0 + d)

    oh1 = lane == e1
    oh2 = lane == e2
    oh = jnp.where(jnp.logical_or(oh1, oh2), 1.0, 0.0)
    r = lax.broadcasted_iota(jnp.int32, (TM, TM), 0)
    c = lax.broadcasted_iota(jnp.int32, (TM, TM), 1)
    tri = jnp.where(c < r, 1.0, 0.0).astype(jnp.bfloat16)
    before = _dot(tri, oh.astype(jnp.bfloat16)) + carry_ref[0:1, :]
    r1 = jnp.sum(jnp.where(oh1, before, 0.0), axis=1, keepdims=True)
    r2 = jnp.sum(jnp.where(oh2, before, 0.0), axis=1, keepdims=True)
    carry = carry_ref[0:1, :] + jnp.sum(oh, axis=0, keepdims=True)
    carry_ref[...] = jnp.broadcast_to(carry, carry_ref.shape)

    info = jnp.where(lane == 0, e1, jnp.where(lane == 1, e2, jnp.where(
        lane == 2, r1.astype(jnp.int32), jnp.where(lane == 3, r2.astype(jnp.int32), 0))))
    info_ref[...] = info
    gate_ref[...] = jnp.where(lane == 0, g1, jnp.where(lane == 1, g2, 0.0))
    cnt_ref[...] = jnp.broadcast_to(carry, cnt_ref.shape).astype(jnp.int32)


def _route(lay, logits):
    T = lay.T
    return pl.pallas_call(
        _route_kernel,
        out_shape=(jax.ShapeDtypeStruct((T, LANES), jnp.int32),
                   jax.ShapeDtypeStruct((T, LANES), jnp.float32),
                   jax.ShapeDtypeStruct((SUBLANES_F32, LANES), jnp.int32)),
        grid=(lay.NT,),
        in_specs=[pl.BlockSpec((TM, LANES), lambda i: (i, 0))],
        out_specs=(pl.BlockSpec((TM, LANES), lambda i: (i, 0)),
                   pl.BlockSpec((TM, LANES), lambda i: (i, 0)),
                   pl.BlockSpec((SUBLANES_F32, LANES), lambda i: (0, 0))),
        scratch_shapes=[pltpu.VMEM((SUBLANES_F32, LANES), jnp.float32)],
        compiler_params=_cparams(("arbitrary",)),
        name="moe_route",
    )(logits)


def _dispatch_kernel(n_blocks, plan_ref, dest_ref, v_ref, xs_hbm, zbuf, sem, zsem):
    @pl.when(pl.program_id(0) == 0)
    def _():
        zbuf[...] = jnp.zeros(zbuf.shape, zbuf.dtype)

        def zero_block(row0):
            rows = pl.ds(pl.multiple_of(row0, EXPERT_ROWS), EXPERT_ROWS)
            return pltpu.make_async_copy(zbuf, xs_hbm.at[rows], zsem)

        def for_each_zero_block(fn):
            def seg(e, _):
                @pl.when(plan_ref[N_EXPERTS + e] > 0)
                def _():
                    fn(zero_block(plan_ref[e] - EXPERT_ROWS))
                return 0

            def tail(b, _):
                fn(zero_block(b * EXPERT_ROWS))
                return 0

            lax.fori_loop(0, N_EXPERTS, seg, 0)
            lax.fori_loop(plan_ref[2 * N_EXPERTS], n_blocks, tail, 0)

        for_each_zero_block(lambda cp: cp.start())
        for_each_zero_block(lambda cp: cp.wait())

    def issue(r, _):
        for k in range(TOP_K):
            pltpu.make_async_copy(v_ref.at[pl.ds(r, 1)], xs_hbm.at[pl.ds(dest_ref[0, k, r], 1)], sem).start()
        return 0

    lax.fori_loop(0, TM, issue, 0, unroll=8)
    for k in range(TOP_K):
        pltpu.make_async_copy(v_ref, xs_hbm.at[pl.ds(0, TM)], sem).wait()


def _dispatch(lay, v, dest, plan, n_blocks):
    W = v.shape[1]
    return pl.pallas_call(
        functools.partial(_dispatch_kernel, n_blocks),
        out_shape=jax.ShapeDtypeStruct((n_blocks * EXPERT_ROWS, W), v.dtype),
        grid=(lay.NT,),
        in_specs=[pl.BlockSpec(memory_space=pltpu.SMEM),
                  pl.BlockSpec((1, TOP_K, TM), lambda i: (i, 0, 0), memory_space=pltpu.SMEM),
                  pl.BlockSpec((TM, W), lambda i: (i, 0))],
        out_specs=pl.BlockSpec(memory_space=pl.ANY),
        scratch_shapes=[pltpu.VMEM((EXPERT_ROWS, W), v.dtype),
                        pltpu.SemaphoreType.DMA(()), pltpu.SemaphoreType.DMA(())],
        compiler_params=pltpu.CompilerParams(dimension_semantics=("arbitrary",), has_side_effects=True),
        name="moe_dispatch",
    )(plan, dest, v)


def _expert_kernel(be_ref, na_ref, x_ref, wg_ref, wu_ref, wd_ref, y_ref, wgu_s, wd_s):
    i = pl.program_id(0)
    de = wg_ref.shape[1]
    active = i < na_ref[0]

    @pl.when(jnp.logical_and(active, jnp.logical_or(i == 0, be_ref[i] != be_ref[jnp.maximum(i - 1, 0)])))
    def _():
        wgu_s[:, :de] = wg_ref[...].astype(wgu_s.dtype)
        wgu_s[:, de:] = wu_ref[...].astype(wgu_s.dtype)
        wd_s[...] = wd_ref[...].astype(wd_s.dtype)

    @pl.when(active)
    def _():
        hi, lo = _unpack_rows(x_ref[...])
        x = jnp.concatenate([hi.astype(jnp.bfloat16), lo.astype(jnp.bfloat16)], axis=1)
        hgu = _dot(x, wgu_s[...])
        hg, hu = hgu[:, :de], hgu[:, de:]
        a = (hg * (1.0 / (1.0 + jnp.exp(-hg)))) * hu
        y_ref[...] = _pack_rows(_dot(a.astype(jnp.bfloat16), wd_s[...]))

    @pl.when(jnp.logical_not(active))
    def _():
        y_ref[...] = jnp.zeros(y_ref.shape, y_ref.dtype)


def _experts(lay, xs, blk_expert, n_active, layer, wg, wu, wd):
    D = lay.D
    P, W = xs.shape
    DE = wg.shape[3]
    grid_spec = pltpu.PrefetchScalarGridSpec(
        num_scalar_prefetch=2,
        grid=(P // EXPERT_ROWS,),
        in_specs=[pl.BlockSpec((EXPERT_ROWS, W), lambda i, be, na: (i, 0)),
                  pl.BlockSpec((None, None, D, DE), lambda i, be, na: (layer, be[i], 0, 0)),
                  pl.BlockSpec((None, None, D, DE), lambda i, be, na: (layer, be[i], 0, 0)),
                  pl.BlockSpec((None, None, DE, D), lambda i, be, na: (layer, be[i], 0, 0))],
        out_specs=pl.BlockSpec((EXPERT_ROWS, W), lambda i, be, na: (i, 0)),
        scratch_shapes=[pltpu.VMEM((D, 2 * DE), jnp.bfloat16), pltpu.VMEM((DE, D), jnp.bfloat16)],
    )
    return pl.pallas_call(
        _expert_kernel,
        out_shape=jax.ShapeDtypeStruct((P, W), xs.dtype),
        grid_spec=grid_spec,
        compiler_params=_cparams(("arbitrary",)),
        name="moe_experts",
    )(blk_expert, n_active, xs, wg, wu, wd)


def _combine_kernel(final, dest_ref, ys_hbm, gate_ref, h_ref, g2_ref, gain_ref, sc_ref, sh_ref, *refs):
    *out_refs, ybuf, sem = refs

    def issue(r, _):
        for k in range(TOP_K):
            pltpu.make_async_copy(ys_hbm.at[pl.ds(dest_ref[0, k, r], 1)], ybuf.at[k, pl.ds(r, 1)], sem).start()
        return 0

    lax.fori_loop(0, TM, issue, 0, unroll=8)
    for k in range(TOP_K):
        pltpu.make_async_copy(ys_hbm.at[pl.ds(0, TM)], ybuf.at[k], sem).wait()
    gate = gate_ref[...]
    hi0, lo0 = _unpack_rows(ybuf[0])
    hi1, lo1 = _unpack_rows(ybuf[1])
    g0, g1 = gate[:, 0:1], gate[:, 1:2]
    f = jnp.concatenate([g0 * hi0 + g1 * hi1, g0 * lo0 + g1 * lo1], axis=1)
    h2 = h_ref[...] + g2_ref[...] * f
    if final:
        (out_ref,) = out_refs
        out_ref[...] = _rms(h2, gain_ref[...])
    else:
        h2_ref, u_ref = out_refs
        h2_ref[...] = h2
        u_ref[...] = (_rms(h2, gain_ref[...]) * (1.0 + sc_ref[...]) + sh_ref[...]).astype(u_ref.dtype)


def _combine(lay, ys, dest, gates, h1, mod, mod_next, gain_next, final):
    D = lay.D
    row_tile = pl.BlockSpec((TM, D), lambda i: (i, 0))
    if final:
        n = lay.B * lay.LT
        tile = lambda i: (i // lay.LT) * lay.TPB + i % lay.LT
        out_shape = (jax.ShapeDtypeStruct((lay.B * lay.S, D), jnp.float32),)
        out_specs = (row_tile,)
    else:
        n = lay.NT
        tile = lambda i: i
        out_shape = (jax.ShapeDtypeStruct((lay.T, D), jnp.float32),
                     jax.ShapeDtypeStruct((lay.T, D), jnp.bfloat16))
        out_specs = (row_tile, row_tile)
    return pl.pallas_call(
        functools.partial(_combine_kernel, final),
        out_shape=out_shape,
        grid=(n,),
        in_specs=[pl.BlockSpec((1, TOP_K, TM), lambda i: (tile(i), 0, 0), memory_space=pltpu.SMEM),
                  pl.BlockSpec(memory_space=pl.ANY),
                  pl.BlockSpec((TM, LANES), lambda i: (tile(i), 0)),
                  pl.BlockSpec((TM, D), lambda i: (tile(i), 0)),
                  _mod_spec(lay, 5, tile),
                  _row_spec(D),
                  _mod_spec(lay, 1, tile), _mod_spec(lay, 0, tile)],
        out_specs=out_specs,
        scratch_shapes=[pltpu.VMEM((TOP_K, TM, ys.shape[1]), ys.dtype), pltpu.SemaphoreType.DMA(())],
        compiler_params=_cparams(("arbitrary",)),
        name="moe_combine_final" if final else "moe_combine",
    )(dest, ys, gates, h1, mod, gain_next.reshape(1, D), mod_next, mod_next)


def _moe(lay, v, logits, h1, mod, mod_next, gain_next, layer, wg, wu, wd, final):
    T, NT = lay.T, lay.NT
    info, gates, cnt = _route(lay, logits)
    counts = cnt[0, :N_EXPERTS]
    padded = ((counts + EXPERT_ROWS - 1) // EXPERT_ROWS) * EXPERT_ROWS
    pad_end = jnp.cumsum(padded)
    pad_start = pad_end - padded
    expert = info[:, :TOP_K]
    onehot = expert[:, :, None] == jnp.arange(N_EXPERTS, dtype=jnp.int32)
    dest = info[:, TOP_K:2 * TOP_K] + jnp.sum(jnp.where(onehot, pad_start, 0), axis=-1)
    dest = dest.reshape(NT, TM, TOP_K).transpose(0, 2, 1).astype(jnp.int32)
    n_blocks = -(-(T * TOP_K + N_EXPERTS * (EXPERT_ROWS - 1)) // EXPERT_ROWS)
    blk_start = jnp.arange(n_blocks, dtype=jnp.int32) * EXPERT_ROWS
    blk_expert = jnp.minimum(jnp.sum(pad_end[None, :] <= blk_start[:, None], axis=1),
                             N_EXPERTS - 1).astype(jnp.int32)
    n_active = (pad_end[-1:] // EXPERT_ROWS).astype(jnp.int32)
    plan = jnp.concatenate([pad_end, padded, n_active]).astype(jnp.int32)
    xs = _dispatch(lay, v, dest, plan, n_blocks)
    ys = _experts(lay, xs, blk_expert, n_active, layer, wg, wu, wd)
    return _combine(lay, ys, dest, gates, h1, mod, mod_next, gain_next, final)


def kernel(x, c, ctx, c_ctx, ada_w, ada_b, norm_mix, norm_ffn, norm_final, conv_in, conv_k, conv_out, gqa_qkv, gqa_q_gain, gqa_k_gain, gqa_out, pool_w, pool_scale, swa_qkv, swa_sink, swa_out, router_grp_w, router_grp_b, router_exp_w, router_exp_b, exp_gate, exp_up, exp_down):
    B, S, D = x.shape
    C = ctx.shape[1]
    L = ada_w.shape[0]
    lay = Layout(B, S, C, D)
    bf = jnp.bfloat16

    R = SUBLANES_BF16
    cvec = jnp.zeros((R, D), jnp.float32).at[:B].set(c).at[B].set(c_ctx)
    mod_all = _modulation(cvec, ada_w, ada_b)[:, :B + 1].reshape(L, B + 1, 6, 1, D).transpose(0, 2, 1, 3, 4)

    pad = LANES - N_GROUPS - N_EXPERTS
    wr_all = jnp.concatenate([router_grp_w, router_exp_w, jnp.zeros((L, D, pad), jnp.float32)], axis=-1).astype(bf)
    br_all = jnp.concatenate([router_grp_b, router_exp_b, jnp.zeros((L, pad), jnp.float32)], axis=-1)

    h = jnp.concatenate([x, ctx], axis=1).reshape(lay.T, D)
    u = _prenorm(lay, h, norm_mix[0], mod_all[0])
    one = jnp.ones((1, LANES), jnp.float32)

    for i in range(L):
        m, j = i % 4, i // 4
        mod = mod_all[i]
        wr, br = wr_all[i], br_all[i].reshape(1, LANES)
        tail = (h, mod, norm_ffn[i], wr, br)
        if m == 0:
            bg, z = _conv_in(lay, u, conv_in[j].astype(bf))
            h1, v, lg = _conv_out(lay, z, bg, conv_k[j], conv_out[j].astype(bf), *tail)
        elif m == 1:
            hd = D // GQA_HEADS
            w = gqa_qkv[j].astype(bf)
            cos_t, sin_t = _rope_tables(lay, hd)
            nq, nk = GQA_HEADS * hd, GQA_KV_HEADS * hd
            qg = gqa_q_gain[j].reshape(1, hd)
            kg = gqa_k_gain[j].reshape(1, hd)
            q = _project(lay, u, w, 0, nq, hd, qg, cos_t, sin_t, True, True, hd ** -0.5 * LOG2E, "gqa_q")
            k = _project(lay, u, w, nq, nk, hd, kg, cos_t, sin_t, True, True, 1.0, "gqa_k")
            vv = _project(lay, u, w, nq + nk, nk, hd, one, cos_t, sin_t, False, False, 1.0, "gqa_v")
            q3, k3, v3 = (a.reshape(B, lay.SB, -1) for a in (q, k, vv))
            n_rep = GQA_HEADS // GQA_KV_HEADS
            tq = 512 if S % 512 == 0 else TM
            tk = 768 if lay.SB % 768 == 0 else TM
            o_lat = _flash(lay, q3, k3, v3, GQA_KV_HEADS, n_rep, tq, tk, 0, S // tq, lay.SB, 0, "gqa_flash")
            o_ctx = _flash(lay, q3, k3, v3, GQA_KV_HEADS, n_rep, C, C, S // C, 1, C, S // C, "gqa_flash_ctx")
            h1, v, lg = _out_proj(lay, o_lat, o_ctx, gqa_out[j].astype(bf), *tail)
        elif m == 2:
            h1, v, lg = _pool(lay, u, pool_w[j].astype(bf), pool_scale[j], *tail)
        else:
            hd = D // SWA_HEADS
            w = swa_qkv[j].astype(bf)
            cos_t, sin_t = _rope_tables(lay, hd)
            nq, nk = SWA_HEADS * hd, SWA_KV_HEADS * hd
            q = _project(lay, u, w, 0, nq, hd, one, cos_t, sin_t, False, True, hd ** -0.5 * LOG2E, "swa_q")
            k = _project(lay, u, w, nq, nk, hd, one, cos_t, sin_t, False, True, 1.0, "swa_k")
            vv = _project(lay, u, w, nq + nk, nk, hd, one, cos_t, sin_t, False, False, 1.0, "swa_v")
            q3, k3, v3 = (a.reshape(B, lay.SB, -1) for a in (q, k, vv))
            o_lat = _swa(lay, q3, k3, v3, swa_sink[j] * LOG2E)
            h1, v, lg = _out_proj(lay, o_lat, None, swa_out[j].astype(bf), *tail)
        final = i == L - 1
        mod_next = mod if final else mod_all[i + 1]
        gain_next = norm_final if final else norm_mix[i + 1]
        res = _moe(lay, v, lg, h1, mod, mod_next, gain_next, i, exp_gate, exp_up, exp_down, final)
        if final:
            return res[0].reshape(B, S, D)
        h, u = res
```

```python
import functools

import jax
import jax.numpy as jnp
from jax import lax
from jax.experimental import pallas as pl
from jax.experimental.pallas import tpu as pltpu

GRID_W = 64
NORM_EPS = 1e-6
ROPE_THETA = 10000.0
NEG_INF = -1e30
GQA_HEADS, GQA_KV_HEADS = 16, 4
SWA_HEADS, SWA_KV_HEADS, SWA_WINDOW = 32, 8, 128
POOL_WINDOWS = (2, 4, 8, 16)
N_GROUPS, EXPERTS_PER_GROUP, TOP_K = 8, 4, 2
N_EXPERTS = N_GROUPS * EXPERTS_PER_GROUP
LOG2E = 1.4426950408889634

LANES = 128
SUBLANES_F32 = 8
SUBLANES_BF16 = 16
VMEM_LIMIT = 56 * 1024 * 1024
TM = 256
EXPERT_ROWS = 256
HALO = SUBLANES_BF16


def _cparams(sem):
    return pltpu.CompilerParams(dimension_semantics=sem, vmem_limit_bytes=VMEM_LIMIT)


class Layout:
    def __init__(self, B, S, C, D):
        assert S % TM == 0 and C % TM == 0
        self.B, self.S, self.C, self.D = B, S, C, D
        self.SB = S + C
        self.T = B * self.SB
        self.LT = S // TM
        self.TPB = self.SB // TM
        self.NT = B * self.TPB

    def split(self, i):
        return i // self.TPB, i % self.TPB

    def mod_row(self, i):
        b, w = self.split(i)
        return jnp.where(w >= self.LT, self.B, b)


def _mod_spec(lay, chunk, tile_of):
    return pl.BlockSpec((None, None, 1, lay.D),
                        lambda *g: (chunk, lay.mod_row(tile_of(*g)), 0, 0))


def _row_spec(D):
    return pl.BlockSpec((1, D), lambda *g: (0, 0))


def _rms(x, gain):
    ms = jnp.mean(x * x, axis=-1, keepdims=True)
    return (x * lax.rsqrt(ms + NORM_EPS)) * gain


def _dot(a, b):
    return jnp.dot(a, b, preferred_element_type=jnp.float32)


def _mod_kernel(c_ref, w_ref, b_ref, o_ref):
    c = c_ref[...]
    a = (c * (1.0 / (1.0 + jnp.exp(-c)))).astype(jnp.bfloat16)
    o_ref[...] = _dot(a, w_ref[...].astype(jnp.bfloat16)) + b_ref[...]


def _modulation(cvec, ada_w, ada_b):
    L, D, N = ada_w.shape
    R = cvec.shape[0]
    tn = 512
    return pl.pallas_call(
        _mod_kernel,
        out_shape=jax.ShapeDtypeStruct((L, R, N), jnp.float32),
        grid=(L, N // tn),
        in_specs=[pl.BlockSpec((R, D), lambda l, j: (0, 0)),
                  pl.BlockSpec((None, D, tn), lambda l, j: (l, 0, j)),
                  pl.BlockSpec((None, 1, tn), lambda l, j: (l, 0, j))],
        out_specs=pl.BlockSpec((None, R, tn), lambda l, j: (l, 0, j)),
        compiler_params=_cparams(("arbitrary", "arbitrary")),
        name="adaln_mod",
    )(cvec, ada_w, ada_b.reshape(L, 1, N))


def _prenorm_kernel(h_ref, gain_ref, sc_ref, sh_ref, u_ref):
    u_ref[...] = (_rms(h_ref[...], gain_ref[...]) * (1.0 + sc_ref[...]) + sh_ref[...]).astype(u_ref.dtype)


def _prenorm(lay, h, gain, mod):
    D = lay.D
    tile = lambda i: i
    return pl.pallas_call(
        _prenorm_kernel,
        out_shape=jax.ShapeDtypeStruct((lay.T, D), jnp.bfloat16),
        grid=(lay.NT,),
        in_specs=[pl.BlockSpec((TM, D), lambda i: (i, 0)), _row_spec(D),
                  _mod_spec(lay, 1, tile), _mod_spec(lay, 0, tile)],
        out_specs=pl.BlockSpec((TM, D), lambda i: (i, 0)),
        compiler_params=_cparams(("arbitrary",)),
        name="prenorm",
    )(h, gain.reshape(1, D), mod, mod)


def _pack_rows(x):
    w = x.shape[1] // 2
    hi = pltpu.bitcast(x[:, :w].astype(jnp.bfloat16).astype(jnp.float32), jnp.uint32)
    lo = pltpu.bitcast(x[:, w:].astype(jnp.bfloat16).astype(jnp.float32), jnp.uint32)
    return hi | (lo >> 16)


def _unpack_rows(words):
    hi = pltpu.bitcast(words & jnp.uint32(0xFFFF0000), jnp.float32)
    lo = pltpu.bitcast(words << 16, jnp.float32)
    return hi, lo


def _mixer_tail(y, h_ref, g1_ref, gain_ref, sc_ref, sh_ref, wr_ref, br_ref, h1_ref, v_ref, lg_ref):
    h1 = h_ref[...] + g1_ref[...] * y
    h1_ref[...] = h1
    v = _rms(h1, gain_ref[...]) * (1.0 + sc_ref[...]) + sh_ref[...]
    v_ref[...] = _pack_rows(v)
    lg_ref[...] = _dot(v.astype(jnp.bfloat16), wr_ref[...]) + br_ref[...]


def _tail_in_specs(lay, tile):
    D = lay.D
    return [pl.BlockSpec((TM, D), lambda *g: (tile(*g), 0)),
            _mod_spec(lay, 2, tile),
            _row_spec(D),
            _mod_spec(lay, 4, tile), _mod_spec(lay, 3, tile),
            pl.BlockSpec((D, LANES), lambda *g: (0, 0)),
            pl.BlockSpec((1, LANES), lambda *g: (0, 0))]


def _tail_out(lay, tile):
    D = lay.D
    shapes = (jax.ShapeDtypeStruct((lay.T, D), jnp.float32),
              jax.ShapeDtypeStruct((lay.T, D // 2), jnp.uint32),
              jax.ShapeDtypeStruct((lay.T, LANES), jnp.float32))
    specs = (pl.BlockSpec((TM, D), lambda *g: (tile(*g), 0)),
             pl.BlockSpec((TM, D // 2), lambda *g: (tile(*g), 0)),
             pl.BlockSpec((TM, LANES), lambda *g: (tile(*g), 0)))
    return shapes, specs


def _seq_flags(lay, i):
    _, w = lay.split(i)
    has_prev = jnp.logical_and(w != 0, w != lay.LT)
    has_next = jnp.logical_and(w != lay.LT - 1, w != lay.TPB - 1)
    return has_prev, has_next


def _halo_specs(lay, width, col=lambda *g: 0, tile=lambda i: i):
    per = TM // HALO
    last = lay.T // HALO - 1
    prev = pl.BlockSpec((HALO, width), lambda *g: (jnp.maximum(tile(*g) * per - 1, 0), col(*g)))
    nxt = pl.BlockSpec((HALO, width), lambda *g: (jnp.minimum((tile(*g) + 1) * per, last), col(*g)))
    return prev, nxt


def _convin_kernel(u_ref, wb_ref, wc_ref, wx_ref, bg_ref, z_ref):
    x = u_ref[...]
    bg_ref[...] = _dot(x, wb_ref[...]).astype(bg_ref.dtype)
    z_ref[...] = (_dot(x, wc_ref[...]) * _dot(x, wx_ref[...])).astype(z_ref.dtype)


def _conv_in(lay, u, w_in):
    D = lay.D
    tn = 1024
    nb = D // tn
    out = jax.ShapeDtypeStruct((lay.T, D), jnp.bfloat16)
    return pl.pallas_call(
        _convin_kernel,
        out_shape=(out, out),
        grid=(nb, lay.NT),
        in_specs=[pl.BlockSpec((TM, D), lambda j, i: (i, 0)),
                  pl.BlockSpec((D, tn), lambda j, i: (0, j)),
                  pl.BlockSpec((D, tn), lambda j, i: (0, nb + j)),
                  pl.BlockSpec((D, tn), lambda j, i: (0, 2 * nb + j))],
        out_specs=(pl.BlockSpec((TM, tn), lambda j, i: (i, j)),
                   pl.BlockSpec((TM, tn), lambda j, i: (i, j))),
        compiler_params=_cparams(("arbitrary", "arbitrary")),
        name="conv_in",
    )(u, w_in, w_in, w_in)


def _convout_kernel(lay, z_ref, zp_ref, zn_ref, bg_ref, ck_ref, wo_ref, *rest):
    i = pl.program_id(0)
    has_prev, has_next = _seq_flags(lay, i)
    z = z_ref[...].astype(jnp.float32)
    row = lax.broadcasted_iota(jnp.int32, z.shape, 0)
    prev_row = jnp.where(has_prev, zp_ref[HALO - 1:HALO, :].astype(jnp.float32), 0.0)
    next_row = jnp.where(has_next, zn_ref[0:1, :].astype(jnp.float32), 0.0)
    z_m1 = jnp.where(row == 0, prev_row, pltpu.roll(z, 1, 0))
    z_p1 = jnp.where(row == TM - 1, next_row, pltpu.roll(z, TM - 1, 0))
    y = z_m1 * ck_ref[0:1, :] + z * ck_ref[1:2, :] + z_p1 * ck_ref[2:3, :]
    g = (bg_ref[...].astype(jnp.float32) * y).astype(jnp.bfloat16)
    _mixer_tail(_dot(g, wo_ref[...]), *rest)


def _conv_out(lay, z, bg, conv_k, w_out, h, mod, gain, wr, br):
    D = lay.D
    tile = lambda i: i
    prev, nxt = _halo_specs(lay, D)
    shapes, specs = _tail_out(lay, tile)
    return pl.pallas_call(
        functools.partial(_convout_kernel, lay),
        out_shape=shapes,
        grid=(lay.NT,),
        in_specs=[pl.BlockSpec((TM, D), lambda i: (i, 0)), prev, nxt,
                  pl.BlockSpec((TM, D), lambda i: (i, 0)),
                  pl.BlockSpec(conv_k.shape, lambda i: (0, 0)),
                  pl.BlockSpec((D, D), lambda i: (0, 0))] + _tail_in_specs(lay, tile),
        out_specs=specs,
        compiler_params=_cparams(("arbitrary",)),
        name="conv_out",
    )(z, z, z, bg, conv_k, w_out, h, mod, gain.reshape(1, D), mod, mod, wr, br)


def _qkv_kernel(head_dim, use_norm, q_scale, u_ref, w_ref, qg_ref, kg_ref, cos_ref, sin_ref, q_ref, k_ref, v_ref):
    y = _dot(u_ref[...], w_ref[...])
    nq, nk = q_ref.shape[1], k_ref.shape[1]
    lane = lax.broadcasted_iota(jnp.int32, (TM, LANES), 1)

    def rotary(yg, gain_ref):
        if use_norm:
            yg = _rms(yg, gain_ref[...])
        if head_dim == LANES:
            rot = pltpu.roll(yg, LANES // 2, 1)
        else:
            q = head_dim // 2
            rot = jnp.where(lane % head_dim < q, pltpu.roll(yg, LANES - q, 1), pltpu.roll(yg, q, 1))
        return yg * cos_ref[...] + rot * sin_ref[...]

    for g in range(nq // LANES):
        cols = slice(g * LANES, (g + 1) * LANES)
        q_ref[:, cols] = (rotary(y[:, cols], qg_ref) * q_scale).astype(q_ref.dtype)
    for g in range(nk // LANES):
        cols = slice(g * LANES, (g + 1) * LANES)
        k_ref[:, cols] = rotary(y[:, nq + g * LANES:nq + (g + 1) * LANES], kg_ref).astype(k_ref.dtype)
    v_ref[...] = y[:, nq + nk:].astype(v_ref.dtype)


def _qkv_project(lay, u, w, nq, nk, head_dim, q_gain, k_gain, cos_t, sin_t, use_norm, q_scale, name):
    D = lay.D
    row = lambda n: pl.BlockSpec((TM, n), lambda i: (i, 0))
    table = pl.BlockSpec((TM, LANES), lambda i: (i % lay.TPB, 0))
    gain = pl.BlockSpec((1, LANES), lambda i: (0, 0))
    bf = jnp.bfloat16
    return pl.pallas_call(
        functools.partial(_qkv_kernel, head_dim, use_norm, q_scale),
        out_shape=(jax.ShapeDtypeStruct((lay.T, nq), bf), jax.ShapeDtypeStruct((lay.T, nk), bf),
                   jax.ShapeDtypeStruct((lay.T, nk), bf)),
        grid=(lay.NT,),
        in_specs=[row(D), pl.BlockSpec((D, nq + 2 * nk), lambda i: (0, 0)), gain, gain, table, table],
        out_specs=(row(nq), row(nk), row(nk)),
        compiler_params=_cparams(("arbitrary",)),
        name=name,
    )(u, w, q_gain, k_gain, cos_t, sin_t)


def _rope_tables(lay, head_dim):
    quarter = head_dim // 4
    rows = lay.S // GRID_W
    row = jnp.repeat(jnp.arange(rows), GRID_W).astype(jnp.float32)
    col = jnp.tile(jnp.arange(GRID_W), rows).astype(jnp.float32)
    inv = ROPE_THETA ** (-jnp.arange(quarter, dtype=jnp.float32) / quarter)
    ang = jnp.concatenate([row[:, None] * inv, col[:, None] * inv], axis=-1)
    cos, sin = jnp.cos(ang), jnp.sin(ang)
    reps = LANES // head_dim
    cos_t = jnp.tile(jnp.concatenate([cos, cos], axis=-1), (1, reps))
    sin_t = jnp.tile(jnp.concatenate([-sin, sin], axis=-1), (1, reps))
    cos_t = jnp.concatenate([cos_t, jnp.ones((lay.C, LANES), jnp.float32)], axis=0)
    sin_t = jnp.concatenate([sin_t, jnp.zeros((lay.C, LANES), jnp.float32)], axis=0)
    return cos_t, sin_t


def _flash_kernel(n_rep, tk, q_ref, k_ref, v_ref, o_ref, vx_ref, s_ref, m_ref, l_ref, acc_ref):
    hd = LANES
    nk = k_ref.shape[0] // tk

    @pl.when(pl.program_id(2) == 0)
    def _():
        vx_ref[:, :hd] = v_ref[...]
        vx_ref[:, hd:] = jnp.ones((vx_ref.shape[0], hd), vx_ref.dtype)

    m_ref[...] = jnp.full(m_ref.shape, NEG_INF, jnp.float32)
    l_ref[...] = jnp.zeros(l_ref.shape, jnp.float32)
    acc_ref[...] = jnp.zeros(acc_ref.shape, jnp.float32)

    def rows(j):
        start = j * tk
        return pl.ds(start if isinstance(start, int) else pl.multiple_of(start, tk), tk)

    def step(j, slot, with_next):
        vx = vx_ref[rows(j), :]
        if with_next:
            k_next = k_ref[rows(j + 1), :]
        for h in range(n_rep):
            if with_next:
                q = q_ref[:, h * hd:(h + 1) * hd]
                s_ref[1 - slot, h] = lax.dot_general(q, k_next, (((1,), (1,)), ((), ())),
                                                     preferred_element_type=jnp.float32)
            s = s_ref[slot, h]
            m_prev = m_ref[h]
            m_next = jnp.maximum(m_prev, jnp.max(s, axis=1, keepdims=True))
            alpha = jnp.exp2(m_prev - m_next)
            p = jnp.exp2(s - m_next[:, :1]).astype(jnp.bfloat16)
            pv = _dot(p, vx)
            m_ref[h] = m_next
            l_ref[h] = alpha * l_ref[h] + pv[:, hd:]
            acc_ref[h] = alpha * acc_ref[h] + pv[:, :hd]

    k0 = k_ref[rows(0), :]
    for h in range(n_rep):
        s_ref[0, h] = lax.dot_general(q_ref[:, h * hd:(h + 1) * hd], k0, (((1,), (1,)), ((), ())),
                                      preferred_element_type=jnp.float32)

    n_pairs = (nk - 1) // 2

    def body(t, _):
        step(2 * t, 0, True)
        step(2 * t + 1, 1, True)
        return 0

    lax.fori_loop(0, n_pairs, body, 0)
    for j in range(2 * n_pairs, nk):
        step(j, j % 2, j + 1 < nk)
    for h in range(n_rep):
        o_ref[:, h * hd:(h + 1) * hd] = (acc_ref[h] / l_ref[h]).astype(o_ref.dtype)


def _flash(lay, q, k, v, n_kv, n_rep, tq, tk, q_blk0, nq, kb, k_blk0, name):
    B = lay.B
    qw = n_rep * LANES
    return pl.pallas_call(
        functools.partial(_flash_kernel, n_rep, tk),
        out_shape=jax.ShapeDtypeStruct((B, nq * tq, n_kv * qw), jnp.bfloat16),
        grid=(B, n_kv, nq),
        in_specs=[pl.BlockSpec((None, tq, qw), lambda b, g, i: (b, q_blk0 + i, g)),
                  pl.BlockSpec((None, kb, LANES), lambda b, g, i: (b, k_blk0, g)),
                  pl.BlockSpec((None, kb, LANES), lambda b, g, i: (b, k_blk0, g))],
        out_specs=pl.BlockSpec((None, tq, qw), lambda b, g, i: (b, i, g)),
        scratch_shapes=[pltpu.VMEM((kb, 2 * LANES), jnp.bfloat16),
                        pltpu.VMEM((2, n_rep, tq, tk), jnp.float32),
                        pltpu.VMEM((n_rep, tq, LANES), jnp.float32),
                        pltpu.VMEM((n_rep, tq, LANES), jnp.float32),
                        pltpu.VMEM((n_rep, tq, LANES), jnp.float32)],
        compiler_params=_cparams(("arbitrary", "arbitrary", "arbitrary")),
        name=name,
    )(q, k, v)


def _swa_kernel(lay, tq, sink_ref, q_ref, kp_ref, kc_ref, kn_ref, kx_ref, vp_ref, vc_ref, vn_ref, vx_ref, o_ref):
    pair = pl.program_id(1)
    qb = pl.program_id(2)
    hd = LANES // 2
    n_rep = SWA_HEADS // SWA_KV_HEADS
    W = SWA_WINDOW
    kk = jnp.concatenate([kp_ref[...], kc_ref[...], kn_ref[...], kx_ref[...]], axis=0)
    vv = jnp.concatenate([vp_ref[...], vc_ref[...], vn_ref[...], vx_ref[...]], axis=0)
    nkeys = kk.shape[0]
    nwin = tq + 2 * W
    lane = lax.broadcasted_iota(jnp.int32, (nkeys, LANES), 1)
    low = lane < hd
    kk_sw = pltpu.roll(kk.astype(jnp.float32), hd, 1).astype(kk.dtype)
    vv_sw = pltpu.roll(vv.astype(jnp.float32), hd, 1).astype(vv.dtype)
    k_dup = (jnp.where(low, kk, kk_sw), jnp.where(low, kk_sw, kk))
    v_dup = (jnp.where(low, vv, vv_sw), jnp.where(low, vv_sw, vv))

    start = qb * tq
    qpos = start + lax.broadcasted_iota(jnp.int32, (tq, nkeys), 0)
    col = lax.broadcasted_iota(jnp.int32, (tq, nkeys), 1)
    kpos = start - W + col
    in_win = jnp.logical_and(jnp.abs(qpos - kpos) <= W, jnp.logical_and(kpos >= 0, kpos < lay.S))
    valid = jnp.logical_or(col >= nwin, in_win)

    qlane = lax.broadcasted_iota(jnp.int32, (tq, LANES), 1)
    qlow = qlane < hd
    for g in range(2 * n_rep * hd // LANES):
        qg = q_ref[:, g * LANES:(g + 1) * LANES]
        kvh = (2 * g) // n_rep
        out = None
        for half in range(2):
            head = pair * 2 * n_rep + 2 * g + half
            qm = jnp.where(qlow if half == 0 else jnp.logical_not(qlow), qg, jnp.zeros_like(qg))
            s = lax.dot_general(qm, k_dup[kvh], (((1,), (1,)), ((), ())), preferred_element_type=jnp.float32)
            s = jnp.where(valid, s, NEG_INF)
            sink = sink_ref[head]
            m = jnp.maximum(jnp.max(s, axis=1, keepdims=True), sink)
            p = jnp.exp2(s - m)
            l = jnp.sum(p, axis=1, keepdims=True) + jnp.exp2(sink - m)
            o = _dot(p.astype(jnp.bfloat16), v_dup[kvh]) / l
            out = o if half == 0 else jnp.where(qlow, out, o)
        o_ref[:, g * LANES:(g + 1) * LANES] = out.astype(o_ref.dtype)


def _swa(lay, q, k, v, sink2):
    B, S = lay.B, lay.S
    tq = TM
    W = SWA_WINDOW
    per = tq // W
    nq = S // tq
    qw = 2 * (SWA_HEADS // SWA_KV_HEADS) * (LANES // 2)
    ctx_blk = S // lay.C
    kv_specs = [pl.BlockSpec((None, W, LANES), lambda b, p, i: (b, jnp.maximum(i * per - 1, 0), p)),
                pl.BlockSpec((None, tq, LANES), lambda b, p, i: (b, i, p)),
                pl.BlockSpec((None, W, LANES), lambda b, p, i: (b, (i + 1) * per, p)),
                pl.BlockSpec((None, lay.C, LANES), lambda b, p, i: (b, ctx_blk, p))]
    return pl.pallas_call(
        functools.partial(_swa_kernel, lay, tq),
        out_shape=jax.ShapeDtypeStruct((B, S, lay.D), jnp.bfloat16),
        grid=(B, SWA_KV_HEADS // 2, nq),
        in_specs=[pl.BlockSpec(memory_space=pltpu.SMEM),
                  pl.BlockSpec((None, tq, qw), lambda b, p, i: (b, i, p))] + kv_specs + kv_specs,
        out_specs=pl.BlockSpec((None, tq, qw), lambda b, p, i: (b, i, p)),
        compiler_params=_cparams(("arbitrary", "arbitrary", "arbitrary")),
        name="swa",
    )(sink2, q, k, k, k, k, v, v, v, v)


def _outproj_kernel(lay, has_ctx, ol_ref, oc_ref, wo_ref, *rest):
    x = ol_ref[...]
    if has_ctx:
        _, w = lay.split(pl.program_id(0))
        x = jnp.where(w >= lay.LT, oc_ref[...], x)
    _mixer_tail(_dot(x, wo_ref[...]), *rest)


def _out_proj(lay, o_lat, o_ctx, w_out, h, mod, gain, wr, br):
    D = lay.D
    tile = lambda i: i
    has_ctx = o_ctx is not None
    if not has_ctx:
        o_ctx = o_lat
    shapes, specs = _tail_out(lay, tile)

    def lat_idx(i):
        b, w = lay.split(i)
        return b, jnp.minimum(w, lay.LT - 1), 0

    def ctx_idx(i):
        b, w = lay.split(i)
        return b, jnp.clip(w - lay.LT, 0, lay.TPB - lay.LT - 1), 0

    return pl.pallas_call(
        functools.partial(_outproj_kernel, lay, has_ctx),
        out_shape=shapes,
        grid=(lay.NT,),
        in_specs=[pl.BlockSpec((None, TM, D), lat_idx),
                  pl.BlockSpec((None, TM, D), ctx_idx if has_ctx else lat_idx),
                  pl.BlockSpec((D, D), lambda i: (0, 0))] + _tail_in_specs(lay, tile),
        out_specs=specs,
        compiler_params=_cparams(("arbitrary",)),
        name="attn_out",
    )(o_lat, o_ctx, w_out, h, mod, gain.reshape(1, D), mod, mod, wr, br)


def _pool_kernel(lay, u_ref, up_ref, un_ref, pw_ref, ps_ref, *rest):
    i = pl.program_id(0)
    _, w = lay.split(i)
    has_prev, has_next = _seq_flags(lay, i)
    in_ctx = w >= lay.LT
    seq_len = jnp.where(in_ctx, lay.C, lay.S)
    pos0 = jnp.where(in_ctx, w - lay.LT, w) * TM
    G = len(POOL_WINDOWS)
    gw = lay.D // G
    E_ROWS = TM + 2 * SUBLANES_F32
    pos = pos0 + lax.broadcasted_iota(jnp.int32, (TM, gw), 0)
    ys = []
    for g, win in enumerate(POOL_WINDOWS):
        sl = slice(g * gw, (g + 1) * gw)
        u = u_ref[:, sl].astype(jnp.float32)
        before = jnp.where(has_prev, up_ref[HALO - SUBLANES_F32:HALO, sl].astype(jnp.float32), 0.0)
        after = jnp.where(has_next, un_ref[0:SUBLANES_F32, sl].astype(jnp.float32), 0.0)
        e = jnp.concatenate([before, u, after], axis=0)
        left = win // 2
        right = win - 1 - left
        assert left == right + 1 and left & (left - 1) == 0
        acc = e
        span = 1
        while span < left:
            acc = acc + pltpu.roll(acc, E_ROWS - span, 0)
            span *= 2
        tot = pltpu.roll(acc, left, 0) + acc
        total = tot[SUBLANES_F32:SUBLANES_F32 + TM]
        cnt = jnp.minimum(pos + right, seq_len - 1) - jnp.maximum(pos - left, 0) + 1
        mean = total / cnt.astype(jnp.float32)
        ys.append(_dot((mean - u).astype(jnp.bfloat16), pw_ref[g]))
    y = jnp.concatenate(ys, axis=1) * ps_ref[...]
    _mixer_tail(y, *rest)


def _pool(lay, u, pool_w, pool_scale, h, mod, gain, wr, br):
    D = lay.D
    tile = lambda i: i
    prev, nxt = _halo_specs(lay, D)
    shapes, specs = _tail_out(lay, tile)
    return pl.pallas_call(
        functools.partial(_pool_kernel, lay),
        out_shape=shapes,
        grid=(lay.NT,),
        in_specs=[pl.BlockSpec((TM, D), lambda i: (i, 0)), prev, nxt,
                  pl.BlockSpec(pool_w.shape, lambda i: (0, 0, 0)),
                  _row_spec(D)] + _tail_in_specs(lay, tile),
        out_specs=specs,
        compiler_params=_cparams(("arbitrary",)),
        name="pool",
    )(u, u, u, pool_w, pool_scale.reshape(1, D), h, mod, gain.reshape(1, D), mod, mod, wr, br)


def _route_kernel(lg_ref, info_ref, gate_ref, cnt_ref, carry_ref):
    i = pl.program_id(0)

    @pl.when(i == 0)
    def _():
        carry_ref[...] = jnp.zeros(carry_ref.shape, jnp.float32)

    lg = lg_ref[...]
    lane = lax.broadcasted_iota(jnp.int32, lg.shape, 1)
    lane_f = lane.astype(jnp.float32)
    big = jnp.float32(4 * LANES)

    def first_lane(mask):
        return jnp.min(jnp.where(mask, lane_f, big), axis=1, keepdims=True).astype(jnp.int32)

    is_grp = lane < N_GROUPS
    gl = jnp.where(is_grp, lg, NEG_INF)
    gmax = jnp.max(gl, axis=1, keepdims=True)
    grp = first_lane(jnp.logical_and(is_grp, gl == gmax))
    p_grp = 1.0 / jnp.sum(jnp.where(is_grp, jnp.exp(gl - gmax), 0.0), axis=1, keepdims=True)
    eid = lane - N_GROUPS
    in_grp = jnp.logical_and(lane >= N_GROUPS + grp * EXPERTS_PER_GROUP,
                             lane < N_GROUPS + (grp + 1) * EXPERTS_PER_GROUP)
    el = jnp.where(in_grp, lg, NEG_INF)
    t1 = jnp.max(el, axis=1, keepdims=True)
    e1 = first_lane(jnp.logical_and(in_grp, el == t1)) - N_GROUPS
    rest = jnp.logical_and(in_grp, eid != e1)
    el2 = jnp.where(rest, lg, NEG_INF)
    t2 = jnp.max(el2, axis=1, keepdims=True)
    e2 = first_lane(jnp.logical_and(rest, el2 == t2)) - N_GROUPS
    d = jnp.exp(t2 - t1)
    g1 = p_grp / (1.0 + d)
    g2 = p_grp * d / (1.0 + d)

    oh1 = lane == e1
    oh2 = lane == e2
    oh = jnp.where(jnp.logical_or(oh1, oh2), 1.0, 0.0)
    r = lax.broadcasted_iota(jnp.int32, (TM, TM), 0)
    c = lax.broadcasted_iota(jnp.int32, (TM, TM), 1)
    tri = jnp.where(c < r, 1.0, 0.0).astype(jnp.bfloat16)
    before = _dot(tri, oh.astype(jnp.bfloat16)) + carry_ref[0:1, :]
    r1 = jnp.sum(jnp.where(oh1, before, 0.0), axis=1, keepdims=True)
    r2 = jnp.sum(jnp.where(oh2, before, 0.0), axis=1, keepdims=True)
    carry = carry_ref[0:1, :] + jnp.sum(oh, axis=0, keepdims=True)
    carry_ref[...] = jnp.broadcast_to(carry, carry_ref.shape)

    info = jnp.where(lane == 0, e1, jnp.where(lane == 1, e2, jnp.where(
        lane == 2, r1.astype(jnp.int32), jnp.where(lane == 3, r2.astype(jnp.int32), 0))))
    info_ref[...] = info
    gate_ref[...] = jnp.where(lane == 0, g1, jnp.where(lane == 1, g2, 0.0))
    cnt_ref[...] = jnp.broadcast_to(carry, cnt_ref.shape).astype(jnp.int32)


def _route(lay, logits):
    T = lay.T
    return pl.pallas_call(
        _route_kernel,
        out_shape=(jax.ShapeDtypeStruct((T, LANES), jnp.int32),
                   jax.ShapeDtypeStruct((T, LANES), jnp.float32),
                   jax.ShapeDtypeStruct((SUBLANES_F32, LANES), jnp.int32)),
        grid=(lay.NT,),
        in_specs=[pl.BlockSpec((TM, LANES), lambda i: (i, 0))],
        out_specs=(pl.BlockSpec((TM, LANES), lambda i: (i, 0)),
                   pl.BlockSpec((TM, LANES), lambda i: (i, 0)),
                   pl.BlockSpec((SUBLANES_F32, LANES), lambda i: (0, 0))),
        scratch_shapes=[pltpu.VMEM((SUBLANES_F32, LANES), jnp.float32)],
        compiler_params=_cparams(("arbitrary",)),
        name="moe_route",
    )(logits)


def _dispatch_kernel(n_blocks, plan_ref, dest_ref, v_ref, xs_hbm, zbuf, sem, zsem):
    @pl.when(pl.program_id(0) == 0)
    def _():
        zbuf[...] = jnp.zeros(zbuf.shape, zbuf.dtype)

        def zero_block(row0):
            rows = pl.ds(pl.multiple_of(row0, EXPERT_ROWS), EXPERT_ROWS)
            return pltpu.make_async_copy(zbuf, xs_hbm.at[rows], zsem)

        def for_each_zero_block(fn):
            def seg(e, _):
                @pl.when(plan_ref[N_EXPERTS + e] > 0)
                def _():
                    fn(zero_block(plan_ref[e] - EXPERT_ROWS))
                return 0

            def tail(b, _):
                fn(zero_block(b * EXPERT_ROWS))
                return 0

            lax.fori_loop(0, N_EXPERTS, seg, 0)
            lax.fori_loop(plan_ref[2 * N_EXPERTS], n_blocks, tail, 0)

        for_each_zero_block(lambda cp: cp.start())
        for_each_zero_block(lambda cp: cp.wait())

    def issue(r, _):
        for k in range(TOP_K):
            pltpu.make_async_copy(v_ref.at[pl.ds(r, 1)], xs_hbm.at[pl.ds(dest_ref[0, k, r], 1)], sem).start()
        return 0

    lax.fori_loop(0, TM, issue, 0, unroll=8)
    for k in range(TOP_K):
        pltpu.make_async_copy(v_ref, xs_hbm.at[pl.ds(0, TM)], sem).wait()


def _dispatch(lay, v, dest, plan, n_blocks):
    W = v.shape[1]
    return pl.pallas_call(
        functools.partial(_dispatch_kernel, n_blocks),
        out_shape=jax.ShapeDtypeStruct((n_blocks * EXPERT_ROWS, W), v.dtype),
        grid=(lay.NT,),
        in_specs=[pl.BlockSpec(memory_space=pltpu.SMEM),
                  pl.BlockSpec((1, TOP_K, TM), lambda i: (i, 0, 0), memory_space=pltpu.SMEM),
                  pl.BlockSpec((TM, W), lambda i: (i, 0))],
        out_specs=pl.BlockSpec(memory_space=pl.ANY),
        scratch_shapes=[pltpu.VMEM((EXPERT_ROWS, W), v.dtype),
                        pltpu.SemaphoreType.DMA(()), pltpu.SemaphoreType.DMA(())],
        compiler_params=pltpu.CompilerParams(dimension_semantics=("arbitrary",), has_side_effects=True),
        name="moe_dispatch",
    )(plan, dest, v)


def _expert_kernel(layer, be_ref, na_ref, nxt_ref, slot_ref, x_ref, wg_hbm, wu_hbm, wd_hbm, y_ref,
                   wg_buf, wu_buf, wd_buf, wgu_s, wd_s, sem):
    i = pl.program_id(0)
    de = wg_buf.shape[2]
    active = i < na_ref[0]
    e = be_ref[i]

    def fetch(expert, slot):
        return (pltpu.make_async_copy(wg_hbm.at[layer, expert], wg_buf.at[slot], sem.at[slot, 0]),
                pltpu.make_async_copy(wu_hbm.at[layer, expert], wu_buf.at[slot], sem.at[slot, 1]),
                pltpu.make_async_copy(wd_hbm.at[layer, expert], wd_buf.at[slot], sem.at[slot, 2]))

    @pl.when(i == 0)
    def _():
        for cp in fetch(e, slot_ref[e]):
            cp.start()

    @pl.when(jnp.logical_and(active, jnp.logical_or(i == 0, e != be_ref[jnp.maximum(i - 1, 0)])))
    def _():
        slot = slot_ref[e]
        for cp in fetch(e, slot):
            cp.wait()

        @pl.when(nxt_ref[e] >= 0)
        def _():
            for cp in fetch(nxt_ref[e], 1 - slot):
                cp.start()

        wgu_s[:, :de] = wg_buf[slot].astype(wgu_s.dtype)
        wgu_s[:, de:] = wu_buf[slot].astype(wgu_s.dtype)
        wd_s[...] = wd_buf[slot].astype(wd_s.dtype)

    @pl.when(active)
    def _():
        hi, lo = _unpack_rows(x_ref[...])
        x = jnp.concatenate([hi.astype(jnp.bfloat16), lo.astype(jnp.bfloat16)], axis=1)
        hgu = _dot(x, wgu_s[...])
        hg, hu = hgu[:, :de], hgu[:, de:]
        a = (hg * (1.0 / (1.0 + jnp.exp(-hg)))) * hu
        y_ref[...] = _pack_rows(_dot(a.astype(jnp.bfloat16), wd_s[...]))

    @pl.when(jnp.logical_not(active))
    def _():
        y_ref[...] = jnp.zeros(y_ref.shape, y_ref.dtype)


def _experts(lay, xs, blk_expert, n_active, nxt, slot, layer, wg, wu, wd):
    D = lay.D
    P, W = xs.shape
    DE = wg.shape[3]
    row_block = pl.BlockSpec((EXPERT_ROWS, W), lambda i, *_: (i, 0))
    hbm = pl.BlockSpec(memory_space=pl.ANY)
    grid_spec = pltpu.PrefetchScalarGridSpec(
        num_scalar_prefetch=4,
        grid=(P // EXPERT_ROWS,),
        in_specs=[row_block, hbm, hbm, hbm],
        out_specs=row_block,
        scratch_shapes=[pltpu.VMEM((2, D, DE), jnp.float32), pltpu.VMEM((2, D, DE), jnp.float32),
                        pltpu.VMEM((2, DE, D), jnp.float32),
                        pltpu.VMEM((D, 2 * DE), jnp.bfloat16), pltpu.VMEM((DE, D), jnp.bfloat16),
                        pltpu.SemaphoreType.DMA((2, 3))],
    )
    return pl.pallas_call(
        functools.partial(_expert_kernel, layer),
        out_shape=jax.ShapeDtypeStruct((P, W), xs.dtype),
        grid_spec=grid_spec,
        compiler_params=_cparams(("arbitrary",)),
        name="moe_experts",
    )(blk_expert, n_active, nxt, slot, xs, wg, wu, wd)


def _combine_kernel(final, dest_ref, ys_hbm, gate_ref, h_ref, g2_ref, gain_ref, sc_ref, sh_ref, *refs):
    *out_refs, ybuf, sem = refs

    def issue(r, _):
        for k in range(TOP_K):
            pltpu.make_async_copy(ys_hbm.at[pl.ds(dest_ref[0, k, r], 1)], ybuf.at[k, pl.ds(r, 1)], sem).start()
        return 0

    lax.fori_loop(0, TM, issue, 0, unroll=8)
    for k in range(TOP_K):
        pltpu.make_async_copy(ys_hbm.at[pl.ds(0, TM)], ybuf.at[k], sem).wait()
    gate = gate_ref[...]
    hi0, lo0 = _unpack_rows(ybuf[0])
    hi1, lo1 = _unpack_rows(ybuf[1])
    g0, g1 = gate[:, 0:1], gate[:, 1:2]
    f = jnp.concatenate([g0 * hi0 + g1 * hi1, g0 * lo0 + g1 * lo1], axis=1)
    h2 = h_ref[...] + g2_ref[...] * f
    if final:
        (out_ref,) = out_refs
        out_ref[...] = _rms(h2, gain_ref[...])
    else:
        h2_ref, u_ref = out_refs
        h2_ref[...] = h2
        u_ref[...] = (_rms(h2, gain_ref[...]) * (1.0 + sc_ref[...]) + sh_ref[...]).astype(u_ref.dtype)


def _combine(lay, ys, dest, gates, h1, mod, mod_next, gain_next, final):
    D = lay.D
    row_tile = pl.BlockSpec((TM, D), lambda i: (i, 0))
    if final:
        n = lay.B * lay.LT
        tile = lambda i: (i // lay.LT) * lay.TPB + i % lay.LT
        out_shape = (jax.ShapeDtypeStruct((lay.B * lay.S, D), jnp.float32),)
        out_specs = (row_tile,)
    else:
        n = lay.NT
        tile = lambda i: i
        out_shape = (jax.ShapeDtypeStruct((lay.T, D), jnp.float32),
                     jax.ShapeDtypeStruct((lay.T, D), jnp.bfloat16))
        out_specs = (row_tile, row_tile)
    return pl.pallas_call(
        functools.partial(_combine_kernel, final),
        out_shape=out_shape,
        grid=(n,),
        in_specs=[pl.BlockSpec((1, TOP_K, TM), lambda i: (tile(i), 0, 0), memory_space=pltpu.SMEM),
                  pl.BlockSpec(memory_space=pl.ANY),
                  pl.BlockSpec((TM, LANES), lambda i: (tile(i), 0)),
                  pl.BlockSpec((TM, D), lambda i: (tile(i), 0)),
                  _mod_spec(lay, 5, tile),
                  _row_spec(D),
                  _mod_spec(lay, 1, tile), _mod_spec(lay, 0, tile)],
        out_specs=out_specs,
        scratch_shapes=[pltpu.VMEM((TOP_K, TM, ys.shape[1]), ys.dtype), pltpu.SemaphoreType.DMA(())],
        compiler_params=_cparams(("arbitrary",)),
        name="moe_combine_final" if final else "moe_combine",
    )(dest, ys, gates, h1, mod, gain_next.reshape(1, D), mod_next, mod_next)


def _moe(lay, v, logits, h1, mod, mod_next, gain_next, layer, wg, wu, wd, final):
    T, NT = lay.T, lay.NT
    info, gates, cnt = _route(lay, logits)
    counts = cnt[0, :N_EXPERTS]
    padded = ((counts + EXPERT_ROWS - 1) // EXPERT_ROWS) * EXPERT_ROWS
    pad_end = jnp.cumsum(padded)
    pad_start = pad_end - padded
    expert = info[:, :TOP_K]
    onehot = expert[:, :, None] == jnp.arange(N_EXPERTS, dtype=jnp.int32)
    dest = info[:, TOP_K:2 * TOP_K] + jnp.sum(jnp.where(onehot, pad_start, 0), axis=-1)
    dest = dest.reshape(NT, TM, TOP_K).transpose(0, 2, 1).astype(jnp.int32)
    n_blocks = -(-(T * TOP_K + N_EXPERTS * (EXPERT_ROWS - 1)) // EXPERT_ROWS)
    blk_start = jnp.arange(n_blocks, dtype=jnp.int32) * EXPERT_ROWS
    blk_expert = jnp.minimum(jnp.sum(pad_end[None, :] <= blk_start[:, None], axis=1),
                             N_EXPERTS - 1).astype(jnp.int32)
    n_active = (pad_end[-1:] // EXPERT_ROWS).astype(jnp.int32)
    plan = jnp.concatenate([pad_end, padded, n_active]).astype(jnp.int32)
    xs = _dispatch(lay, v, dest, plan, n_blocks)
    owns = padded > 0
    ids = jnp.arange(N_EXPERTS, dtype=jnp.int32)
    later = jnp.where(owns[None, :] & (ids[None, :] > ids[:, None]), ids[None, :], N_EXPERTS)
    nxt = jnp.min(later, axis=1)
    nxt = jnp.where(nxt == N_EXPERTS, -1, nxt).astype(jnp.int32)
    slot = ((jnp.cumsum(owns.astype(jnp.int32)) - 1) % 2).astype(jnp.int32)
    ys = _experts(lay, xs, blk_expert, n_active, nxt, slot, layer, wg, wu, wd)
    return _combine(lay, ys, dest, gates, h1, mod, mod_next, gain_next, final)


def kernel(x, c, ctx, c_ctx, ada_w, ada_b, norm_mix, norm_ffn, norm_final, conv_in, conv_k, conv_out, gqa_qkv, gqa_q_gain, gqa_k_gain, gqa_out, pool_w, pool_scale, swa_qkv, swa_sink, swa_out, router_grp_w, router_grp_b, router_exp_w, router_exp_b, exp_gate, exp_up, exp_down):
    B, S, D = x.shape
    C = ctx.shape[1]
    L = ada_w.shape[0]
    lay = Layout(B, S, C, D)
    bf = jnp.bfloat16

    R = SUBLANES_BF16
    cvec = jnp.zeros((R, D), jnp.float32).at[:B].set(c).at[B].set(c_ctx)
    mod_all = _modulation(cvec, ada_w, ada_b)[:, :B + 1].reshape(L, B + 1, 6, 1, D).transpose(0, 2, 1, 3, 4)

    pad = LANES - N_GROUPS - N_EXPERTS
    wr_all = jnp.concatenate([router_grp_w, router_exp_w, jnp.zeros((L, D, pad), jnp.float32)], axis=-1).astype(bf)
    br_all = jnp.concatenate([router_grp_b, router_exp_b, jnp.zeros((L, pad), jnp.float32)], axis=-1)

    h = jnp.concatenate([x, ctx], axis=1).reshape(lay.T, D)
    u = _prenorm(lay, h, norm_mix[0], mod_all[0])
    one = jnp.ones((1, LANES), jnp.float32)

    for i in range(L):
        m, j = i % 4, i // 4
        mod = mod_all[i]
        wr, br = wr_all[i], br_all[i].reshape(1, LANES)
        tail = (h, mod, norm_ffn[i], wr, br)
        if m == 0:
            bg, z = _conv_in(lay, u, conv_in[j].astype(bf))
            h1, v, lg = _conv_out(lay, z, bg, conv_k[j], conv_out[j].astype(bf), *tail)
        elif m == 1:
            hd = D // GQA_HEADS
            w = gqa_qkv[j].astype(bf)
            cos_t, sin_t = _rope_tables(lay, hd)
            nq, nk = GQA_HEADS * hd, GQA_KV_HEADS * hd
            qg = gqa_q_gain[j].reshape(1, hd)
            kg = gqa_k_gain[j].reshape(1, hd)
            q, k, vv = _qkv_project(lay, u, w, nq, nk, hd, qg, kg, cos_t, sin_t, True, hd ** -0.5 * LOG2E, "gqa_qkv")
            q3, k3, v3 = (a.reshape(B, lay.SB, -1) for a in (q, k, vv))
            n_rep = GQA_HEADS // GQA_KV_HEADS
            tq = 512 if S % 512 == 0 else TM
            tk = 768 if lay.SB % 768 == 0 else TM
            o_lat = _flash(lay, q3, k3, v3, GQA_KV_HEADS, n_rep, tq, tk, 0, S // tq, lay.SB, 0, "gqa_flash")
            o_ctx = _flash(lay, q3, k3, v3, GQA_KV_HEADS, n_rep, C, C, S // C, 1, C, S // C, "gqa_flash_ctx")
            h1, v, lg = _out_proj(lay, o_lat, o_ctx, gqa_out[j].astype(bf), *tail)
        elif m == 2:
            h1, v, lg = _pool(lay, u, pool_w[j].astype(bf), pool_scale[j], *tail)
        else:
            hd = D // SWA_HEADS
            w = swa_qkv[j].astype(bf)
            cos_t, sin_t = _rope_tables(lay, hd)
            nq, nk = SWA_HEADS * hd, SWA_KV_HEADS * hd
            q, k, vv = _qkv_project(lay, u, w, nq, nk, hd, one, one, cos_t, sin_t, False, hd ** -0.5 * LOG2E, "swa_qkv")
            q3, k3, v3 = (a.reshape(B, lay.SB, -1) for a in (q, k, vv))
            o_lat = _swa(lay, q3, k3, v3, swa_sink[j] * LOG2E)
            h1, v, lg = _out_proj(lay, o_lat, None, swa_out[j].astype(bf), *tail)
        final = i == L - 1
        mod_next = mod if final else mod_all[i + 1]
        gain_next = norm_final if final else norm_mix[i + 1]
        res = _moe(lay, v, lg, h1, mod, mod_next, gain_next, i, exp_gate, exp_up, exp_down, final)
        if final:
            return res[0].reshape(B, S, D)
        h, u = res
```

```python
import functools

import jax
import jax.numpy as jnp
from jax import lax
from jax.experimental import pallas as pl
from jax.experimental.pallas import tpu as pltpu

GRID_W = 64
NORM_EPS = 1e-6
ROPE_THETA = 10000.0
NEG_INF = -1e30
GQA_HEADS, GQA_KV_HEADS = 16, 4
SWA_HEADS, SWA_KV_HEADS, SWA_WINDOW = 32, 8, 128
POOL_WINDOWS = (2, 4, 8, 16)
N_GROUPS, EXPERTS_PER_GROUP, TOP_K = 8, 4, 2
N_EXPERTS = N_GROUPS * EXPERTS_PER_GROUP
LOG2E = 1.4426950408889634

LANES = 128
SUBLANES_F32 = 8
SUBLANES_BF16 = 16
VMEM_LIMIT = 56 * 1024 * 1024
TM = 256
EXPERT_ROWS = 256
COMBINE_ROWS = 32
HALO = SUBLANES_BF16


def _cparams(sem):
    return pltpu.CompilerParams(dimension_semantics=sem, vmem_limit_bytes=VMEM_LIMIT)


class Layout:
    def __init__(self, B, S, C, D):
        assert S % TM == 0 and C % TM == 0
        self.B, self.S, self.C, self.D = B, S, C, D
        self.SB = S + C
        self.T = B * self.SB
        self.LT = S // TM
        self.TPB = self.SB // TM
        self.NT = B * self.TPB

    def split(self, i):
        return i // self.TPB, i % self.TPB

    def mod_row(self, i):
        b, w = self.split(i)
        return jnp.where(w >= self.LT, self.B, b)


def _mod_spec(lay, chunk, tile_of):
    return pl.BlockSpec((None, None, 1, lay.D),
                        lambda *g: (chunk, lay.mod_row(tile_of(*g)), 0, 0))


def _row_spec(D):
    return pl.BlockSpec((1, D), lambda *g: (0, 0))


def _rms(x, gain):
    ms = jnp.mean(x * x, axis=-1, keepdims=True)
    return (x * lax.rsqrt(ms + NORM_EPS)) * gain


def _dot(a, b):
    return jnp.dot(a, b, preferred_element_type=jnp.float32)


def _mod_kernel(c_ref, w_ref, b_ref, o_ref):
    c = c_ref[...]
    a = (c * (1.0 / (1.0 + jnp.exp(-c)))).astype(jnp.bfloat16)
    o_ref[...] = _dot(a, w_ref[...].astype(jnp.bfloat16)) + b_ref[...]


def _modulation(cvec, ada_w, ada_b):
    L, D, N = ada_w.shape
    R = cvec.shape[0]
    tn = 512
    return pl.pallas_call(
        _mod_kernel,
        out_shape=jax.ShapeDtypeStruct((L, R, N), jnp.float32),
        grid=(L, N // tn),
        in_specs=[pl.BlockSpec((R, D), lambda l, j: (0, 0)),
                  pl.BlockSpec((None, D, tn), lambda l, j: (l, 0, j)),
                  pl.BlockSpec((None, 1, tn), lambda l, j: (l, 0, j))],
        out_specs=pl.BlockSpec((None, R, tn), lambda l, j: (l, 0, j)),
        compiler_params=_cparams(("arbitrary", "arbitrary")),
        name="adaln_mod",
    )(cvec, ada_w, ada_b.reshape(L, 1, N))


def _prenorm_kernel(h_ref, gain_ref, sc_ref, sh_ref, u_ref):
    u_ref[...] = (_rms(h_ref[...], gain_ref[...]) * (1.0 + sc_ref[...]) + sh_ref[...]).astype(u_ref.dtype)


def _prenorm(lay, h, gain, mod):
    D = lay.D
    tile = lambda i: i
    return pl.pallas_call(
        _prenorm_kernel,
        out_shape=jax.ShapeDtypeStruct((lay.T, D), jnp.bfloat16),
        grid=(lay.NT,),
        in_specs=[pl.BlockSpec((TM, D), lambda i: (i, 0)), _row_spec(D),
                  _mod_spec(lay, 1, tile), _mod_spec(lay, 0, tile)],
        out_specs=pl.BlockSpec((TM, D), lambda i: (i, 0)),
        compiler_params=_cparams(("arbitrary",)),
        name="prenorm",
    )(h, gain.reshape(1, D), mod, mod)


def _pack_rows(x):
    w = x.shape[1] // 2
    hi = pltpu.bitcast(x[:, :w].astype(jnp.bfloat16).astype(jnp.float32), jnp.uint32)
    lo = pltpu.bitcast(x[:, w:].astype(jnp.bfloat16).astype(jnp.float32), jnp.uint32)
    return hi | (lo >> 16)


def _unpack_rows(words):
    hi = pltpu.bitcast(words & jnp.uint32(0xFFFF0000), jnp.float32)
    lo = pltpu.bitcast(words << 16, jnp.float32)
    return hi, lo


def _mixer_tail(y, h_ref, g1_ref, gain_ref, sc_ref, sh_ref, wr_ref, br_ref, h1_ref, v_ref, lg_ref):
    h1 = h_ref[...] + g1_ref[...] * y
    h1_ref[...] = h1
    v = _rms(h1, gain_ref[...]) * (1.0 + sc_ref[...]) + sh_ref[...]
    v_ref[...] = _pack_rows(v)
    lg_ref[...] = _dot(v.astype(jnp.bfloat16), wr_ref[...]) + br_ref[...]


def _tail_in_specs(lay, tile):
    D = lay.D
    return [pl.BlockSpec((TM, D), lambda *g: (tile(*g), 0)),
            _mod_spec(lay, 2, tile),
            _row_spec(D),
            _mod_spec(lay, 4, tile), _mod_spec(lay, 3, tile),
            pl.BlockSpec((D, LANES), lambda *g: (0, 0)),
            pl.BlockSpec((1, LANES), lambda *g: (0, 0))]


def _tail_out(lay, tile):
    D = lay.D
    shapes = (jax.ShapeDtypeStruct((lay.T, D), jnp.float32),
              jax.ShapeDtypeStruct((lay.T, D // 2), jnp.uint32),
              jax.ShapeDtypeStruct((lay.T, LANES), jnp.float32))
    specs = (pl.BlockSpec((TM, D), lambda *g: (tile(*g), 0)),
             pl.BlockSpec((TM, D // 2), lambda *g: (tile(*g), 0)),
             pl.BlockSpec((TM, LANES), lambda *g: (tile(*g), 0)))
    return shapes, specs


def _seq_flags(lay, i):
    _, w = lay.split(i)
    has_prev = jnp.logical_and(w != 0, w != lay.LT)
    has_next = jnp.logical_and(w != lay.LT - 1, w != lay.TPB - 1)
    return has_prev, has_next


def _halo_specs(lay, width, col=lambda *g: 0, tile=lambda i: i):
    per = TM // HALO
    last = lay.T // HALO - 1
    prev = pl.BlockSpec((HALO, width), lambda *g: (jnp.maximum(tile(*g) * per - 1, 0), col(*g)))
    nxt = pl.BlockSpec((HALO, width), lambda *g: (jnp.minimum((tile(*g) + 1) * per, last), col(*g)))
    return prev, nxt


def _convin_kernel(u_ref, wb_ref, wc_ref, wx_ref, bg_ref, z_ref):
    x = u_ref[...]
    bg_ref[...] = _dot(x, wb_ref[...]).astype(bg_ref.dtype)
    z_ref[...] = (_dot(x, wc_ref[...]) * _dot(x, wx_ref[...])).astype(z_ref.dtype)


def _conv_in(lay, u, w_in):
    D = lay.D
    tn = 1024
    nb = D // tn
    out = jax.ShapeDtypeStruct((lay.T, D), jnp.bfloat16)
    return pl.pallas_call(
        _convin_kernel,
        out_shape=(out, out),
        grid=(nb, lay.NT),
        in_specs=[pl.BlockSpec((TM, D), lambda j, i: (i, 0)),
                  pl.BlockSpec((D, tn), lambda j, i: (0, j)),
                  pl.BlockSpec((D, tn), lambda j, i: (0, nb + j)),
                  pl.BlockSpec((D, tn), lambda j, i: (0, 2 * nb + j))],
        out_specs=(pl.BlockSpec((TM, tn), lambda j, i: (i, j)),
                   pl.BlockSpec((TM, tn), lambda j, i: (i, j))),
        compiler_params=_cparams(("arbitrary", "arbitrary")),
        name="conv_in",
    )(u, w_in, w_in, w_in)


def _convout_kernel(lay, z_ref, zp_ref, zn_ref, bg_ref, ck_ref, wo_ref, *rest):
    i = pl.program_id(0)
    has_prev, has_next = _seq_flags(lay, i)
    z = z_ref[...].astype(jnp.float32)
    row = lax.broadcasted_iota(jnp.int32, z.shape, 0)
    prev_row = jnp.where(has_prev, zp_ref[HALO - 1:HALO, :].astype(jnp.float32), 0.0)
    next_row = jnp.where(has_next, zn_ref[0:1, :].astype(jnp.float32), 0.0)
    z_m1 = jnp.where(row == 0, prev_row, pltpu.roll(z, 1, 0))
    z_p1 = jnp.where(row == TM - 1, next_row, pltpu.roll(z, TM - 1, 0))
    y = z_m1 * ck_ref[0:1, :] + z * ck_ref[1:2, :] + z_p1 * ck_ref[2:3, :]
    g = (bg_ref[...].astype(jnp.float32) * y).astype(jnp.bfloat16)
    _mixer_tail(_dot(g, wo_ref[...]), *rest)


def _conv_out(lay, z, bg, conv_k, w_out, h, mod, gain, wr, br):
    D = lay.D
    tile = lambda i: i
    prev, nxt = _halo_specs(lay, D)
    shapes, specs = _tail_out(lay, tile)
    return pl.pallas_call(
        functools.partial(_convout_kernel, lay),
        out_shape=shapes,
        grid=(lay.NT,),
        in_specs=[pl.BlockSpec((TM, D), lambda i: (i, 0)), prev, nxt,
                  pl.BlockSpec((TM, D), lambda i: (i, 0)),
                  pl.BlockSpec(conv_k.shape, lambda i: (0, 0)),
                  pl.BlockSpec((D, D), lambda i: (0, 0))] + _tail_in_specs(lay, tile),
        out_specs=specs,
        compiler_params=_cparams(("arbitrary",)),
        name="conv_out",
    )(z, z, z, bg, conv_k, w_out, h, mod, gain.reshape(1, D), mod, mod, wr, br)


def _qkv_kernel(head_dim, use_norm, q_scale, u_ref, w_ref, qg_ref, kg_ref, cos_ref, sin_ref, q_ref, k_ref, v_ref):
    y = _dot(u_ref[...], w_ref[...])
    nq, nk = q_ref.shape[1], k_ref.shape[1]
    lane = lax.broadcasted_iota(jnp.int32, (TM, LANES), 1)

    def rotary(yg, gain_ref):
        if use_norm:
            yg = _rms(yg, gain_ref[...])
        if head_dim == LANES:
            rot = pltpu.roll(yg, LANES // 2, 1)
        else:
            q = head_dim // 2
            rot = jnp.where(lane % head_dim < q, pltpu.roll(yg, LANES - q, 1), pltpu.roll(yg, q, 1))
        return yg * cos_ref[...] + rot * sin_ref[...]

    for g in range(nq // LANES):
        cols = slice(g * LANES, (g + 1) * LANES)
        q_ref[:, cols] = (rotary(y[:, cols], qg_ref) * q_scale).astype(q_ref.dtype)
    for g in range(nk // LANES):
        cols = slice(g * LANES, (g + 1) * LANES)
        k_ref[:, cols] = rotary(y[:, nq + g * LANES:nq + (g + 1) * LANES], kg_ref).astype(k_ref.dtype)
    v_ref[...] = y[:, nq + nk:].astype(v_ref.dtype)


def _qkv_project(lay, u, w, nq, nk, head_dim, q_gain, k_gain, cos_t, sin_t, use_norm, q_scale, name):
    D = lay.D
    row = lambda n: pl.BlockSpec((TM, n), lambda i: (i, 0))
    table = pl.BlockSpec((TM, LANES), lambda i: (i % lay.TPB, 0))
    gain = pl.BlockSpec((1, LANES), lambda i: (0, 0))
    bf = jnp.bfloat16
    return pl.pallas_call(
        functools.partial(_qkv_kernel, head_dim, use_norm, q_scale),
        out_shape=(jax.ShapeDtypeStruct((lay.T, nq), bf), jax.ShapeDtypeStruct((lay.T, nk), bf),
                   jax.ShapeDtypeStruct((lay.T, nk), bf)),
        grid=(lay.NT,),
        in_specs=[row(D), pl.BlockSpec((D, nq + 2 * nk), lambda i: (0, 0)), gain, gain, table, table],
        out_specs=(row(nq), row(nk), row(nk)),
        compiler_params=_cparams(("arbitrary",)),
        name=name,
    )(u, w, q_gain, k_gain, cos_t, sin_t)


def _rope_tables(lay, head_dim):
    quarter = head_dim // 4
    rows = lay.S // GRID_W
    row = jnp.repeat(jnp.arange(rows), GRID_W).astype(jnp.float32)
    col = jnp.tile(jnp.arange(GRID_W), rows).astype(jnp.float32)
    inv = ROPE_THETA ** (-jnp.arange(quarter, dtype=jnp.float32) / quarter)
    ang = jnp.concatenate([row[:, None] * inv, col[:, None] * inv], axis=-1)
    cos, sin = jnp.cos(ang), jnp.sin(ang)
    reps = LANES // head_dim
    cos_t = jnp.tile(jnp.concatenate([cos, cos], axis=-1), (1, reps))
    sin_t = jnp.tile(jnp.concatenate([-sin, sin], axis=-1), (1, reps))
    cos_t = jnp.concatenate([cos_t, jnp.ones((lay.C, LANES), jnp.float32)], axis=0)
    sin_t = jnp.concatenate([sin_t, jnp.zeros((lay.C, LANES), jnp.float32)], axis=0)
    return cos_t, sin_t


def _flash_kernel(n_rep, tk, q_ref, k_ref, v_ref, o_ref, vx_ref, s_ref, m_ref, l_ref, acc_ref):
    hd = LANES
    nk = k_ref.shape[0] // tk

    @pl.when(pl.program_id(2) == 0)
    def _():
        vx_ref[:, :hd] = v_ref[...]
        vx_ref[:, hd:] = jnp.ones((vx_ref.shape[0], hd), vx_ref.dtype)

    m_ref[...] = jnp.full(m_ref.shape, NEG_INF, jnp.float32)
    l_ref[...] = jnp.zeros(l_ref.shape, jnp.float32)
    acc_ref[...] = jnp.zeros(acc_ref.shape, jnp.float32)

    def rows(j):
        start = j * tk
        return pl.ds(start if isinstance(start, int) else pl.multiple_of(start, tk), tk)

    def step(j, slot, with_next):
        vx = vx_ref[rows(j), :]
        if with_next:
            k_next = k_ref[rows(j + 1), :]
        for h in range(n_rep):
            if with_next:
                q = q_ref[:, h * hd:(h + 1) * hd]
                s_ref[1 - slot, h] = lax.dot_general(q, k_next, (((1,), (1,)), ((), ())),
                                                     preferred_element_type=jnp.float32)
            s = s_ref[slot, h]
            m_prev = m_ref[h]
            m_next = jnp.maximum(m_prev, jnp.max(s, axis=1, keepdims=True))
            alpha = jnp.exp2(m_prev - m_next)
            p = jnp.exp2(s - m_next[:, :1]).astype(jnp.bfloat16)
            pv = _dot(p, vx)
            m_ref[h] = m_next
            l_ref[h] = alpha * l_ref[h] + pv[:, hd:]
            acc_ref[h] = alpha * acc_ref[h] + pv[:, :hd]

    k0 = k_ref[rows(0), :]
    for h in range(n_rep):
        s_ref[0, h] = lax.dot_general(q_ref[:, h * hd:(h + 1) * hd], k0, (((1,), (1,)), ((), ())),
                                      preferred_element_type=jnp.float32)

    n_pairs = (nk - 1) // 2

    def body(t, _):
        step(2 * t, 0, True)
        step(2 * t + 1, 1, True)
        return 0

    lax.fori_loop(0, n_pairs, body, 0)
    for j in range(2 * n_pairs, nk):
        step(j, j % 2, j + 1 < nk)
    for h in range(n_rep):
        o_ref[:, h * hd:(h + 1) * hd] = (acc_ref[h] / l_ref[h]).astype(o_ref.dtype)


def _flash(lay, q, k, v, n_kv, n_rep, tq, tk, q_blk0, nq, kb, k_blk0, name):
    B = lay.B
    qw = n_rep * LANES
    return pl.pallas_call(
        functools.partial(_flash_kernel, n_rep, tk),
        out_shape=jax.ShapeDtypeStruct((B, nq * tq, n_kv * qw), jnp.bfloat16),
        grid=(B, n_kv, nq),
        in_specs=[pl.BlockSpec((None, tq, qw), lambda b, g, i: (b, q_blk0 + i, g)),
                  pl.BlockSpec((None, kb, LANES), lambda b, g, i: (b, k_blk0, g)),
                  pl.BlockSpec((None, kb, LANES), lambda b, g, i: (b, k_blk0, g))],
        out_specs=pl.BlockSpec((None, tq, qw), lambda b, g, i: (b, i, g)),
        scratch_shapes=[pltpu.VMEM((kb, 2 * LANES), jnp.bfloat16),
                        pltpu.VMEM((2, n_rep, tq, tk), jnp.float32),
                        pltpu.VMEM((n_rep, tq, LANES), jnp.float32),
                        pltpu.VMEM((n_rep, tq, LANES), jnp.float32),
                        pltpu.VMEM((n_rep, tq, LANES), jnp.float32)],
        compiler_params=_cparams(("arbitrary", "arbitrary", "arbitrary")),
        name=name,
    )(q, k, v)


def _swa_kernel(lay, tq, sink_ref, q_ref, kp_ref, kc_ref, kn_ref, kx_ref, vp_ref, vc_ref, vn_ref, vx_ref, o_ref):
    pair = pl.program_id(1)
    qb = pl.program_id(2)
    hd = LANES // 2
    n_rep = SWA_HEADS // SWA_KV_HEADS
    W = SWA_WINDOW
    kk = jnp.concatenate([kp_ref[...], kc_ref[...], kn_ref[...], kx_ref[...]], axis=0)
    vv = jnp.concatenate([vp_ref[...], vc_ref[...], vn_ref[...], vx_ref[...]], axis=0)
    nkeys = kk.shape[0]
    nwin = tq + 2 * W
    lane = lax.broadcasted_iota(jnp.int32, (nkeys, LANES), 1)
    low = lane < hd
    kk_sw = pltpu.roll(kk.astype(jnp.float32), hd, 1).astype(kk.dtype)
    vv_sw = pltpu.roll(vv.astype(jnp.float32), hd, 1).astype(vv.dtype)
    k_dup = (jnp.where(low, kk, kk_sw), jnp.where(low, kk_sw, kk))
    v_dup = (jnp.where(low, vv, vv_sw), jnp.where(low, vv_sw, vv))

    start = qb * tq
    qpos = start + lax.broadcasted_iota(jnp.int32, (tq, nkeys), 0)
    col = lax.broadcasted_iota(jnp.int32, (tq, nkeys), 1)
    kpos = start - W + col
    in_win = jnp.logical_and(jnp.abs(qpos - kpos) <= W, jnp.logical_and(kpos >= 0, kpos < lay.S))
    valid = jnp.logical_or(col >= nwin, in_win)

    qlane = lax.broadcasted_iota(jnp.int32, (tq, LANES), 1)
    qlow = qlane < hd
    for g in range(2 * n_rep * hd // LANES):
        qg = q_ref[:, g * LANES:(g + 1) * LANES]
        kvh = (2 * g) // n_rep
        out = None
        for half in range(2):
            head = pair * 2 * n_rep + 2 * g + half
            qm = jnp.where(qlow if half == 0 else jnp.logical_not(qlow), qg, jnp.zeros_like(qg))
            s = lax.dot_general(qm, k_dup[kvh], (((1,), (1,)), ((), ())), preferred_element_type=jnp.float32)
            s = jnp.where(valid, s, NEG_INF)
            sink = sink_ref[head]
            m = jnp.maximum(jnp.max(s, axis=1, keepdims=True), sink)
            p = jnp.exp2(s - m)
            l = jnp.sum(p, axis=1, keepdims=True) + jnp.exp2(sink - m)
            o = _dot(p.astype(jnp.bfloat16), v_dup[kvh]) / l
            out = o if half == 0 else jnp.where(qlow, out, o)
        o_ref[:, g * LANES:(g + 1) * LANES] = out.astype(o_ref.dtype)


def _swa(lay, q, k, v, sink2):
    B, S = lay.B, lay.S
    tq = TM
    W = SWA_WINDOW
    per = tq // W
    nq = S // tq
    qw = 2 * (SWA_HEADS // SWA_KV_HEADS) * (LANES // 2)
    ctx_blk = S // lay.C
    kv_specs = [pl.BlockSpec((None, W, LANES), lambda b, p, i: (b, jnp.maximum(i * per - 1, 0), p)),
                pl.BlockSpec((None, tq, LANES), lambda b, p, i: (b, i, p)),
                pl.BlockSpec((None, W, LANES), lambda b, p, i: (b, (i + 1) * per, p)),
                pl.BlockSpec((None, lay.C, LANES), lambda b, p, i: (b, ctx_blk, p))]
    return pl.pallas_call(
        functools.partial(_swa_kernel, lay, tq),
        out_shape=jax.ShapeDtypeStruct((B, S, lay.D), jnp.bfloat16),
        grid=(B, SWA_KV_HEADS // 2, nq),
        in_specs=[pl.BlockSpec(memory_space=pltpu.SMEM),
                  pl.BlockSpec((None, tq, qw), lambda b, p, i: (b, i, p))] + kv_specs + kv_specs,
        out_specs=pl.BlockSpec((None, tq, qw), lambda b, p, i: (b, i, p)),
        compiler_params=_cparams(("arbitrary", "arbitrary", "arbitrary")),
        name="swa",
    )(sink2, q, k, k, k, k, v, v, v, v)


def _outproj_kernel(lay, has_ctx, ol_ref, oc_ref, wo_ref, *rest):
    x = ol_ref[...]
    if has_ctx:
        _, w = lay.split(pl.program_id(0))
        x = jnp.where(w >= lay.LT, oc_ref[...], x)
    _mixer_tail(_dot(x, wo_ref[...]), *rest)


def _out_proj(lay, o_lat, o_ctx, w_out, h, mod, gain, wr, br):
    D = lay.D
    tile = lambda i: i
    has_ctx = o_ctx is not None
    if not has_ctx:
        o_ctx = o_lat
    shapes, specs = _tail_out(lay, tile)

    def lat_idx(i):
        b, w = lay.split(i)
        return b, jnp.minimum(w, lay.LT - 1), 0

    def ctx_idx(i):
        b, w = lay.split(i)
        return b, jnp.clip(w - lay.LT, 0, lay.TPB - lay.LT - 1), 0

    return pl.pallas_call(
        functools.partial(_outproj_kernel, lay, has_ctx),
        out_shape=shapes,
        grid=(lay.NT,),
        in_specs=[pl.BlockSpec((None, TM, D), lat_idx),
                  pl.BlockSpec((None, TM, D), ctx_idx if has_ctx else lat_idx),
                  pl.BlockSpec((D, D), lambda i: (0, 0))] + _tail_in_specs(lay, tile),
        out_specs=specs,
        compiler_params=_cparams(("arbitrary",)),
        name="attn_out",
    )(o_lat, o_ctx, w_out, h, mod, gain.reshape(1, D), mod, mod, wr, br)


def _pool_kernel(lay, u_ref, up_ref, un_ref, pw_ref, ps_ref, *rest):
    i = pl.program_id(0)
    _, w = lay.split(i)
    has_prev, has_next = _seq_flags(lay, i)
    in_ctx = w >= lay.LT
    seq_len = jnp.where(in_ctx, lay.C, lay.S)
    pos0 = jnp.where(in_ctx, w - lay.LT, w) * TM
    G = len(POOL_WINDOWS)
    gw = lay.D // G
    E_ROWS = TM + 2 * SUBLANES_F32
    pos = pos0 + lax.broadcasted_iota(jnp.int32, (TM, gw), 0)
    ys = []
    for g, win in enumerate(POOL_WINDOWS):
        sl = slice(g * gw, (g + 1) * gw)
        u = u_ref[:, sl].astype(jnp.float32)
        before = jnp.where(has_prev, up_ref[HALO - SUBLANES_F32:HALO, sl].astype(jnp.float32), 0.0)
        after = jnp.where(has_next, un_ref[0:SUBLANES_F32, sl].astype(jnp.float32), 0.0)
        e = jnp.concatenate([before, u, after], axis=0)
        left = win // 2
        right = win - 1 - left
        assert left == right + 1 and left & (left - 1) == 0
        acc = e
        span = 1
        while span < left:
            acc = acc + pltpu.roll(acc, E_ROWS - span, 0)
            span *= 2
        tot = pltpu.roll(acc, left, 0) + acc
        total = tot[SUBLANES_F32:SUBLANES_F32 + TM]
        cnt = jnp.minimum(pos + right, seq_len - 1) - jnp.maximum(pos - left, 0) + 1
        mean = total / cnt.astype(jnp.float32)
        ys.append(_dot((mean - u).astype(jnp.bfloat16), pw_ref[g]))
    y = jnp.concatenate(ys, axis=1) * ps_ref[...]
    _mixer_tail(y, *rest)


def _pool(lay, u, pool_w, pool_scale, h, mod, gain, wr, br):
    D = lay.D
    tile = lambda i: i
    prev, nxt = _halo_specs(lay, D)
    shapes, specs = _tail_out(lay, tile)
    return pl.pallas_call(
        functools.partial(_pool_kernel, lay),
        out_shape=shapes,
        grid=(lay.NT,),
        in_specs=[pl.BlockSpec((TM, D), lambda i: (i, 0)), prev, nxt,
                  pl.BlockSpec(pool_w.shape, lambda i: (0, 0, 0)),
                  _row_spec(D)] + _tail_in_specs(lay, tile),
        out_specs=specs,
        compiler_params=_cparams(("arbitrary",)),
        name="pool",
    )(u, u, u, pool_w, pool_scale.reshape(1, D), h, mod, gain.reshape(1, D), mod, mod, wr, br)


def _route_kernel(lg_ref, info_ref, gate_ref, cnt_ref, carry_ref):
    i = pl.program_id(0)

    @pl.when(i == 0)
    def _():
        carry_ref[...] = jnp.zeros(carry_ref.shape, jnp.float32)

    lg = lg_ref[...]
    lane = lax.broadcasted_iota(jnp.int32, lg.shape, 1)
    lane_f = lane.astype(jnp.float32)
    big = jnp.float32(4 * LANES)

    def first_lane(mask):
        return jnp.min(jnp.where(mask, lane_f, big), axis=1, keepdims=True).astype(jnp.int32)

    is_grp = lane < N_GROUPS
    gl = jnp.where(is_grp, lg, NEG_INF)
    gmax = jnp.max(gl, axis=1, keepdims=True)
    grp = first_lane(jnp.logical_and(is_grp, gl == gmax))
    p_grp = 1.0 / jnp.sum(jnp.where(is_grp, jnp.exp(gl - gmax), 0.0), axis=1, keepdims=True)
    eid = lane - N_GROUPS
    in_grp = jnp.logical_and(lane >= N_GROUPS + grp * EXPERTS_PER_GROUP,
                             lane < N_GROUPS + (grp + 1) * EXPERTS_PER_GROUP)
    el = jnp.where(in_grp, lg, NEG_INF)
    t1 = jnp.max(el, axis=1, keepdims=True)
    e1 = first_lane(jnp.logical_and(in_grp, el == t1)) - N_GROUPS
    rest = jnp.logical_and(in_grp, eid != e1)
    el2 = jnp.where(rest, lg, NEG_INF)
    t2 = jnp.max(el2, axis=1, keepdims=True)
    e2 = first_lane(jnp.logical_and(rest, el2 == t2)) - N_GROUPS
    d = jnp.exp(t2 - t1)
    g1 = p_grp / (1.0 + d)
    g2 = p_grp * d / (1.0 + d)

    oh1 = lane == e1
    oh2 = lane == e2
    oh = jnp.where(jnp.logical_or(oh1, oh2), 1.0, 0.0)
    r = lax.broadcasted_iota(jnp.int32, (TM, TM), 0)
    c = lax.broadcasted_iota(jnp.int32, (TM, TM), 1)
    tri = jnp.where(c < r, 1.0, 0.0).astype(jnp.bfloat16)
    before = _dot(tri, oh.astype(jnp.bfloat16)) + carry_ref[0:1, :]
    r1 = jnp.sum(jnp.where(oh1, before, 0.0), axis=1, keepdims=True)
    r2 = jnp.sum(jnp.where(oh2, before, 0.0), axis=1, keepdims=True)
    carry = carry_ref[0:1, :] + jnp.sum(oh, axis=0, keepdims=True)
    carry_ref[...] = jnp.broadcast_to(carry, carry_ref.shape)

    info = jnp.where(lane == 0, e1, jnp.where(lane == 1, e2, jnp.where(
        lane == 2, r1.astype(jnp.int32), jnp.where(lane == 3, r2.astype(jnp.int32), 0))))
    info_ref[...] = info
    gate_ref[...] = jnp.where(lane == 0, g1, jnp.where(lane == 1, g2, 0.0))
    cnt_ref[...] = jnp.broadcast_to(carry, cnt_ref.shape).astype(jnp.int32)


def _route(lay, logits):
    T = lay.T
    return pl.pallas_call(
        _route_kernel,
        out_shape=(jax.ShapeDtypeStruct((T, LANES), jnp.int32),
                   jax.ShapeDtypeStruct((T, LANES), jnp.float32),
                   jax.ShapeDtypeStruct((SUBLANES_F32, LANES), jnp.int32)),
        grid=(lay.NT,),
        in_specs=[pl.BlockSpec((TM, LANES), lambda i: (i, 0))],
        out_specs=(pl.BlockSpec((TM, LANES), lambda i: (i, 0)),
                   pl.BlockSpec((TM, LANES), lambda i: (i, 0)),
                   pl.BlockSpec((SUBLANES_F32, LANES), lambda i: (0, 0))),
        scratch_shapes=[pltpu.VMEM((SUBLANES_F32, LANES), jnp.float32)],
        compiler_params=_cparams(("arbitrary",)),
        name="moe_route",
    )(logits)


def _dispatch_kernel(n_blocks, plan_ref, dest_ref, v_ref, xs_hbm, zbuf, sem, zsem):
    @pl.when(pl.program_id(0) == 0)
    def _():
        zbuf[...] = jnp.zeros(zbuf.shape, zbuf.dtype)

        def zero_block(row0):
            rows = pl.ds(pl.multiple_of(row0, EXPERT_ROWS), EXPERT_ROWS)
            return pltpu.make_async_copy(zbuf, xs_hbm.at[rows], zsem)

        def for_each_zero_block(fn):
            def seg(e, _):
                @pl.when(plan_ref[N_EXPERTS + e] > 0)
                def _():
                    fn(zero_block(plan_ref[e] - EXPERT_ROWS))
                return 0

            def tail(b, _):
                fn(zero_block(b * EXPERT_ROWS))
                return 0

            lax.fori_loop(0, N_EXPERTS, seg, 0)
            lax.fori_loop(plan_ref[2 * N_EXPERTS], n_blocks, tail, 0)

        for_each_zero_block(lambda cp: cp.start())
        for_each_zero_block(lambda cp: cp.wait())

    def issue(r, _):
        for k in range(TOP_K):
            pltpu.make_async_copy(v_ref.at[pl.ds(r, 1)], xs_hbm.at[pl.ds(dest_ref[0, k, r], 1)], sem).start()
        return 0

    lax.fori_loop(0, TM, issue, 0, unroll=True)
    for k in range(TOP_K):
        pltpu.make_async_copy(v_ref, xs_hbm.at[pl.ds(0, TM)], sem).wait()


def _dispatch(lay, v, dest, plan, n_blocks):
    W = v.shape[1]
    return pl.pallas_call(
        functools.partial(_dispatch_kernel, n_blocks),
        out_shape=jax.ShapeDtypeStruct((n_blocks * EXPERT_ROWS, W), v.dtype),
        grid=(lay.NT,),
        in_specs=[pl.BlockSpec(memory_space=pltpu.SMEM),
                  pl.BlockSpec((1, TOP_K, TM), lambda i: (i, 0, 0), memory_space=pltpu.SMEM),
                  pl.BlockSpec((TM, W), lambda i: (i, 0))],
        out_specs=pl.BlockSpec(memory_space=pl.ANY),
        scratch_shapes=[pltpu.VMEM((EXPERT_ROWS, W), v.dtype),
                        pltpu.SemaphoreType.DMA(()), pltpu.SemaphoreType.DMA(())],
        compiler_params=pltpu.CompilerParams(dimension_semantics=("arbitrary",), has_side_effects=True),
        name="moe_dispatch",
    )(plan, dest, v)


def _expert_kernel(layer, be_ref, na_ref, nxt_ref, slot_ref, x_ref, wg_hbm, wu_hbm, wd_hbm, y_ref,
                   wg_buf, wu_buf, wd_buf, wgu_s, wd_s, sem):
    i = pl.program_id(0)
    de = wg_buf.shape[2]
    active = i < na_ref[0]
    e = be_ref[i]

    def fetch(expert, slot):
        return (pltpu.make_async_copy(wg_hbm.at[layer, expert], wg_buf.at[slot], sem.at[slot, 0]),
                pltpu.make_async_copy(wu_hbm.at[layer, expert], wu_buf.at[slot], sem.at[slot, 1]),
                pltpu.make_async_copy(wd_hbm.at[layer, expert], wd_buf.at[slot], sem.at[slot, 2]))

    @pl.when(i == 0)
    def _():
        for cp in fetch(e, slot_ref[e]):
            cp.start()

    @pl.when(jnp.logical_and(active, jnp.logical_or(i == 0, e != be_ref[jnp.maximum(i - 1, 0)])))
    def _():
        slot = slot_ref[e]
        for cp in fetch(e, slot):
            cp.wait()

        @pl.when(nxt_ref[e] >= 0)
        def _():
            for cp in fetch(nxt_ref[e], 1 - slot):
                cp.start()

        wgu_s[:, :de] = wg_buf[slot].astype(wgu_s.dtype)
        wgu_s[:, de:] = wu_buf[slot].astype(wgu_s.dtype)
        wd_s[...] = wd_buf[slot].astype(wd_s.dtype)

    @pl.when(active)
    def _():
        hi, lo = _unpack_rows(x_ref[...])
        x = jnp.concatenate([hi.astype(jnp.bfloat16), lo.astype(jnp.bfloat16)], axis=1)
        hgu = _dot(x, wgu_s[...])
        hg, hu = hgu[:, :de], hgu[:, de:]
        a = (hg * (1.0 / (1.0 + jnp.exp(-hg)))) * hu
        y_ref[...] = _pack_rows(_dot(a.astype(jnp.bfloat16), wd_s[...]))

    @pl.when(jnp.logical_not(active))
    def _():
        y_ref[...] = jnp.zeros(y_ref.shape, y_ref.dtype)


def _experts(lay, xs, blk_expert, n_active, nxt, slot, layer, wg, wu, wd):
    D = lay.D
    P, W = xs.shape
    DE = wg.shape[3]
    row_block = pl.BlockSpec((EXPERT_ROWS, W), lambda i, *_: (i, 0))
    hbm = pl.BlockSpec(memory_space=pl.ANY)
    grid_spec = pltpu.PrefetchScalarGridSpec(
        num_scalar_prefetch=4,
        grid=(P // EXPERT_ROWS,),
        in_specs=[row_block, hbm, hbm, hbm],
        out_specs=row_block,
        scratch_shapes=[pltpu.VMEM((2, D, DE), jnp.float32), pltpu.VMEM((2, D, DE), jnp.float32),
                        pltpu.VMEM((2, DE, D), jnp.float32),
                        pltpu.VMEM((D, 2 * DE), jnp.bfloat16), pltpu.VMEM((DE, D), jnp.bfloat16),
                        pltpu.SemaphoreType.DMA((2, 3))],
    )
    return pl.pallas_call(
        functools.partial(_expert_kernel, layer),
        out_shape=jax.ShapeDtypeStruct((P, W), xs.dtype),
        grid_spec=grid_spec,
        compiler_params=_cparams(("arbitrary",)),
        name="moe_experts",
    )(blk_expert, n_active, nxt, slot, xs, wg, wu, wd)


def _combine_kernel(final, dest_ref, destn_ref, ys_hbm, gate_ref, h_ref, g2_ref, gain_ref, sc_ref, sh_ref,
                    *refs):
    *out_refs, ybuf, sem = refs
    i = pl.program_id(0)

    def gather(dref, buf_slot, r0, r1):
        for r in range(r0, r1):
            for k in range(TOP_K):
                pltpu.make_async_copy(ys_hbm.at[pl.ds(dref[0, k, r], 1)], ybuf.at[buf_slot, k, pl.ds(r, 1)],
                                      sem.at[buf_slot]).start()

    def wait_tile(buf_slot):
        for k in range(TOP_K):
            pltpu.make_async_copy(ys_hbm.at[pl.ds(0, TM)], ybuf.at[buf_slot, k], sem.at[buf_slot]).wait()

    @pl.when(i == 0)
    def _():
        gather(dest_ref, 0, 0, TM)

    def step(slot):
        wait_tile(slot)
        for r0 in range(0, TM, COMBINE_ROWS):
            rows = pl.ds(r0, COMBINE_ROWS)
            gate = gate_ref[rows, :]
            hi0, lo0 = _unpack_rows(ybuf[slot, 0, rows, :])
            hi1, lo1 = _unpack_rows(ybuf[slot, 1, rows, :])
            g0, g1 = gate[:, 0:1], gate[:, 1:2]
            f = jnp.concatenate([g0 * hi0 + g1 * hi1, g0 * lo0 + g1 * lo1], axis=1)
            h2 = h_ref[rows, :] + g2_ref[...] * f
            if final:
                (out_ref,) = out_refs
                out_ref[rows, :] = _rms(h2, gain_ref[...])
            else:
                h2_ref, u_ref = out_refs
                h2_ref[rows, :] = h2
                u_ref[rows, :] = (_rms(h2, gain_ref[...]) * (1.0 + sc_ref[...]) + sh_ref[...]).astype(u_ref.dtype)
            gather(destn_ref, 1 - slot, r0, r0 + COMBINE_ROWS)

        @pl.when(i == pl.num_programs(0) - 1)
        def _():
            wait_tile(1 - slot)

    for slot in range(2):
        pl.when(i % 2 == slot)(functools.partial(step, slot))


def _combine(lay, ys, dest, gates, h1, mod, mod_next, gain_next, final):
    D = lay.D
    row_tile = pl.BlockSpec((TM, D), lambda i: (i, 0))
    if final:
        n = lay.B * lay.LT
        tile = lambda i: (i // lay.LT) * lay.TPB + i % lay.LT
        out_shape = (jax.ShapeDtypeStruct((lay.B * lay.S, D), jnp.float32),)
        out_specs = (row_tile,)
    else:
        n = lay.NT
        tile = lambda i: i
        out_shape = (jax.ShapeDtypeStruct((lay.T, D), jnp.float32),
                     jax.ShapeDtypeStruct((lay.T, D), jnp.bfloat16))
        out_specs = (row_tile, row_tile)
    return pl.pallas_call(
        functools.partial(_combine_kernel, final),
        out_shape=out_shape,
        grid=(n,),
        in_specs=[pl.BlockSpec((1, TOP_K, TM), lambda i: (tile(i), 0, 0), memory_space=pltpu.SMEM),
                  pl.BlockSpec((1, TOP_K, TM), lambda i: (tile(jnp.minimum(i + 1, n - 1)), 0, 0),
                               memory_space=pltpu.SMEM),
                  pl.BlockSpec(memory_space=pl.ANY),
                  pl.BlockSpec((TM, LANES), lambda i: (tile(i), 0)),
                  pl.BlockSpec((TM, D), lambda i: (tile(i), 0)),
                  _mod_spec(lay, 5, tile),
                  _row_spec(D),
                  _mod_spec(lay, 1, tile), _mod_spec(lay, 0, tile)],
        out_specs=out_specs,
        scratch_shapes=[pltpu.VMEM((2, TOP_K, TM, ys.shape[1]), ys.dtype), pltpu.SemaphoreType.DMA((2,))],
        compiler_params=_cparams(("arbitrary",)),
        name="moe_combine_final" if final else "moe_combine",
    )(dest, dest, ys, gates, h1, mod, gain_next.reshape(1, D), mod_next, mod_next)


def _moe(lay, v, logits, h1, mod, mod_next, gain_next, layer, wg, wu, wd, final):
    T, NT = lay.T, lay.NT
    info, gates, cnt = _route(lay, logits)
    counts = cnt[0, :N_EXPERTS]
    padded = ((counts + EXPERT_ROWS - 1) // EXPERT_ROWS) * EXPERT_ROWS
    pad_end = jnp.cumsum(padded)
    pad_start = pad_end - padded
    expert = info[:, :TOP_K]
    onehot = expert[:, :, None] == jnp.arange(N_EXPERTS, dtype=jnp.int32)
    dest = info[:, TOP_K:2 * TOP_K] + jnp.sum(jnp.where(onehot, pad_start, 0), axis=-1)
    dest = dest.reshape(NT, TM, TOP_K).transpose(0, 2, 1).astype(jnp.int32)
    n_blocks = -(-(T * TOP_K + N_EXPERTS * (EXPERT_ROWS - 1)) // EXPERT_ROWS)
    blk_start = jnp.arange(n_blocks, dtype=jnp.int32) * EXPERT_ROWS
    blk_expert = jnp.minimum(jnp.sum(pad_end[None, :] <= blk_start[:, None], axis=1),
                             N_EXPERTS - 1).astype(jnp.int32)
    n_active = (pad_end[-1:] // EXPERT_ROWS).astype(jnp.int32)
    plan = jnp.concatenate([pad_end, padded, n_active]).astype(jnp.int32)
    xs = _dispatch(lay, v, dest, plan, n_blocks)
    owns = padded > 0
    ids = jnp.arange(N_EXPERTS, dtype=jnp.int32)
    later = jnp.where(owns[None, :] & (ids[None, :] > ids[:, None]), ids[None, :], N_EXPERTS)
    nxt = jnp.min(later, axis=1)
    nxt = jnp.where(nxt == N_EXPERTS, -1, nxt).astype(jnp.int32)
    slot = ((jnp.cumsum(owns.astype(jnp.int32)) - 1) % 2).astype(jnp.int32)
    ys = _experts(lay, xs, blk_expert, n_active, nxt, slot, layer, wg, wu, wd)
    return _combine(lay, ys, dest, gates, h1, mod, mod_next, gain_next, final)


def kernel(x, c, ctx, c_ctx, ada_w, ada_b, norm_mix, norm_ffn, norm_final, conv_in, conv_k, conv_out, gqa_qkv, gqa_q_gain, gqa_k_gain, gqa_out, pool_w, pool_scale, swa_qkv, swa_sink, swa_out, router_grp_w, router_grp_b, router_exp_w, router_exp_b, exp_gate, exp_up, exp_down):
    B, S, D = x.shape
    C = ctx.shape[1]
    L = ada_w.shape[0]
    lay = Layout(B, S, C, D)
    bf = jnp.bfloat16

    R = SUBLANES_BF16
    cvec = jnp.zeros((R, D), jnp.float32).at[:B].set(c).at[B].set(c_ctx)
    mod_all = _modulation(cvec, ada_w, ada_b)[:, :B + 1].reshape(L, B + 1, 6, 1, D).transpose(0, 2, 1, 3, 4)

    pad = LANES - N_GROUPS - N_EXPERTS
    wr_all = jnp.concatenate([router_grp_w, router_exp_w, jnp.zeros((L, D, pad), jnp.float32)], axis=-1).astype(bf)
    br_all = jnp.concatenate([router_grp_b, router_exp_b, jnp.zeros((L, pad), jnp.float32)], axis=-1)

    h = jnp.concatenate([x, ctx], axis=1).reshape(lay.T, D)
    u = _prenorm(lay, h, norm_mix[0], mod_all[0])
    one = jnp.ones((1, LANES), jnp.float32)

    for i in range(L):
        m, j = i % 4, i // 4
        mod = mod_all[i]
        wr, br = wr_all[i], br_all[i].reshape(1, LANES)
        tail = (h, mod, norm_ffn[i], wr, br)
        if m == 0:
            bg, z = _conv_in(lay, u, conv_in[j].astype(bf))
            h1, v, lg = _conv_out(lay, z, bg, conv_k[j], conv_out[j].astype(bf), *tail)
        elif m == 1:
            hd = D // GQA_HEADS
            w = gqa_qkv[j].astype(bf)
            cos_t, sin_t = _rope_tables(lay, hd)
            nq, nk = GQA_HEADS * hd, GQA_KV_HEADS * hd
            qg = gqa_q_gain[j].reshape(1, hd)
            kg = gqa_k_gain[j].reshape(1, hd)
            q, k, vv = _qkv_project(lay, u, w, nq, nk, hd, qg, kg, cos_t, sin_t, True, hd ** -0.5 * LOG2E, "gqa_qkv")
            q3, k3, v3 = (a.reshape(B, lay.SB, -1) for a in (q, k, vv))
            n_rep = GQA_HEADS // GQA_KV_HEADS
            tq = 512 if S % 512 == 0 else TM
            tk = 768 if lay.SB % 768 == 0 else TM
            o_lat = _flash(lay, q3, k3, v3, GQA_KV_HEADS, n_rep, tq, tk, 0, S // tq, lay.SB, 0, "gqa_flash")
            o_ctx = _flash(lay, q3, k3, v3, GQA_KV_HEADS, n_rep, C, C, S // C, 1, C, S // C, "gqa_flash_ctx")
            h1, v, lg = _out_proj(lay, o_lat, o_ctx, gqa_out[j].astype(bf), *tail)
        elif m == 2:
            h1, v, lg = _pool(lay, u, pool_w[j].astype(bf), pool_scale[j], *tail)
        else:
            hd = D // SWA_HEADS
            w = swa_qkv[j].astype(bf)
            cos_t, sin_t = _rope_tables(lay, hd)
            nq, nk = SWA_HEADS * hd, SWA_KV_HEADS * hd
            q, k, vv = _qkv_project(lay, u, w, nq, nk, hd, one, one, cos_t, sin_t, False, hd ** -0.5 * LOG2E, "swa_qkv")
            q3, k3, v3 = (a.reshape(B, lay.SB, -1) for a in (q, k, vv))
            o_lat = _swa(lay, q3, k3, v3, swa_sink[j] * LOG2E)
            h1, v, lg = _out_proj(lay, o_lat, None, swa_out[j].astype(bf), *tail)
        final = i == L - 1
        mod_next = mod if final else mod_all[i + 1]
        gain_next = norm_final if final else norm_mix[i + 1]
        res = _moe(lay, v, lg, h1, mod, mod_next, gain_next, i, exp_gate, exp_up, exp_down, final)
        if final:
            return res[0].reshape(B, S, D)
        h, u = res
```

```python
import functools

import jax
import jax.numpy as jnp
from jax import lax
from jax.experimental import pallas as pl
from jax.experimental.pallas import tpu as pltpu

GRID_W = 64
NORM_EPS = 1e-6
ROPE_THETA = 10000.0
NEG_INF = -1e30
GQA_HEADS, GQA_KV_HEADS = 16, 4
SWA_HEADS, SWA_KV_HEADS, SWA_WINDOW = 32, 8, 128
POOL_WINDOWS = (2, 4, 8, 16)
N_GROUPS, EXPERTS_PER_GROUP, TOP_K = 8, 4, 2
N_EXPERTS = N_GROUPS * EXPERTS_PER_GROUP
LOG2E = 1.4426950408889634

LANES = 128
SUBLANES_F32 = 8
SUBLANES_BF16 = 16
VMEM_LIMIT = 56 * 1024 * 1024
TM = 256
EXPERT_ROWS = 256
COMBINE_ROWS = 32
HALO = SUBLANES_BF16


def _cparams(sem):
    return pltpu.CompilerParams(dimension_semantics=sem, vmem_limit_bytes=VMEM_LIMIT)


class Layout:
    def __init__(self, B, S, C, D):
        assert S % TM == 0 and C % TM == 0
        self.B, self.S, self.C, self.D = B, S, C, D
        self.SB = S + C
        self.T = B * self.SB
        self.LT = S // TM
        self.TPB = self.SB // TM
        self.NT = B * self.TPB

    def split(self, i):
        return i // self.TPB, i % self.TPB

    def mod_row(self, i):
        b, w = self.split(i)
        return jnp.where(w >= self.LT, self.B, b)


def _mod_spec(lay, chunk, tile_of):
    return pl.BlockSpec((None, None, 1, lay.D),
                        lambda *g: (chunk, lay.mod_row(tile_of(*g)), 0, 0))


def _row_spec(D):
    return pl.BlockSpec((1, D), lambda *g: (0, 0))


def _rms(x, gain):
    ms = jnp.mean(x * x, axis=-1, keepdims=True)
    return (x * lax.rsqrt(ms + NORM_EPS)) * gain


def _dot(a, b):
    return jnp.dot(a, b, preferred_element_type=jnp.float32)


def _mod_kernel(c_ref, w_ref, b_ref, o_ref):
    c = c_ref[...]
    a = (c * (1.0 / (1.0 + jnp.exp(-c)))).astype(jnp.bfloat16)
    o_ref[...] = _dot(a, w_ref[...].astype(jnp.bfloat16)) + b_ref[...]


def _modulation(cvec, ada_w, ada_b):
    L, D, N = ada_w.shape
    R = cvec.shape[0]
    tn = 512
    return pl.pallas_call(
        _mod_kernel,
        out_shape=jax.ShapeDtypeStruct((L, R, N), jnp.float32),
        grid=(L, N // tn),
        in_specs=[pl.BlockSpec((R, D), lambda l, j: (0, 0)),
                  pl.BlockSpec((None, D, tn), lambda l, j: (l, 0, j)),
                  pl.BlockSpec((None, 1, tn), lambda l, j: (l, 0, j))],
        out_specs=pl.BlockSpec((None, R, tn), lambda l, j: (l, 0, j)),
        compiler_params=_cparams(("arbitrary", "arbitrary")),
        name="adaln_mod",
    )(cvec, ada_w, ada_b.reshape(L, 1, N))


def _prenorm_kernel(lay, x_ref, ctx_ref, gain_ref, sc_ref, sh_ref, h_ref, u_ref):
    _, w = lay.split(pl.program_id(0))
    h = jnp.where(w >= lay.LT, ctx_ref[...], x_ref[...])
    h_ref[...] = h
    u_ref[...] = (_rms(h, gain_ref[...]) * (1.0 + sc_ref[...]) + sh_ref[...]).astype(u_ref.dtype)


def _lat_ctx_specs(lay):
    def lat_idx(i):
        b, w = lay.split(i)
        return b, jnp.minimum(w, lay.LT - 1), 0

    def ctx_idx(i):
        b, w = lay.split(i)
        return b, jnp.clip(w - lay.LT, 0, lay.TPB - lay.LT - 1), 0

    return pl.BlockSpec((None, TM, lay.D), lat_idx), pl.BlockSpec((None, TM, lay.D), ctx_idx)


def _prenorm(lay, x, ctx, gain, mod):
    D = lay.D
    tile = lambda i: i
    row = pl.BlockSpec((TM, D), lambda i: (i, 0))
    return pl.pallas_call(
        functools.partial(_prenorm_kernel, lay),
        out_shape=(jax.ShapeDtypeStruct((lay.T, D), jnp.float32),
                   jax.ShapeDtypeStruct((lay.T, D), jnp.bfloat16)),
        grid=(lay.NT,),
        in_specs=[*_lat_ctx_specs(lay), _row_spec(D), _mod_spec(lay, 1, tile), _mod_spec(lay, 0, tile)],
        out_specs=(row, row),
        compiler_params=_cparams(("arbitrary",)),
        name="prenorm",
    )(x, ctx, gain.reshape(1, D), mod, mod)


def _pack_rows(x):
    w = x.shape[1] // 2
    hi = pltpu.bitcast(x[:, :w].astype(jnp.bfloat16).astype(jnp.float32), jnp.uint32)
    lo = pltpu.bitcast(x[:, w:].astype(jnp.bfloat16).astype(jnp.float32), jnp.uint32)
    return hi | (lo >> 16)


def _unpack_rows(words):
    hi = pltpu.bitcast(words & jnp.uint32(0xFFFF0000), jnp.float32)
    lo = pltpu.bitcast(words << 16, jnp.float32)
    return hi, lo


def _mixer_tail(y, h_ref, g1_ref, gain_ref, sc_ref, sh_ref, wr_ref, br_ref,
                h1_ref, v_ref, info_ref, gate_ref, cnt_ref, carry_ref):
    h1 = h_ref[...] + g1_ref[...] * y
    h1_ref[...] = h1
    v = _rms(h1, gain_ref[...]) * (1.0 + sc_ref[...]) + sh_ref[...]
    v_ref[...] = _pack_rows(v)
    logits = _dot(v.astype(jnp.bfloat16), wr_ref[...]) + br_ref[...]
    _route_tile(logits, info_ref, gate_ref, cnt_ref, carry_ref)


def _tail_in_specs(lay, tile):
    D = lay.D
    return [pl.BlockSpec((TM, D), lambda *g: (tile(*g), 0)),
            _mod_spec(lay, 2, tile),
            _row_spec(D),
            _mod_spec(lay, 4, tile), _mod_spec(lay, 3, tile),
            pl.BlockSpec((D, LANES), lambda *g: (0, 0)),
            pl.BlockSpec((1, LANES), lambda *g: (0, 0))]


def _tail_out(lay, tile):
    D = lay.D
    shapes = (jax.ShapeDtypeStruct((lay.T, D), jnp.float32),
              jax.ShapeDtypeStruct((lay.T, D // 2), jnp.uint32),
              jax.ShapeDtypeStruct((lay.T, LANES), jnp.int32),
              jax.ShapeDtypeStruct((lay.T, LANES), jnp.float32),
              jax.ShapeDtypeStruct((SUBLANES_F32, LANES), jnp.int32))
    specs = (pl.BlockSpec((TM, D), lambda *g: (tile(*g), 0)),
             pl.BlockSpec((TM, D // 2), lambda *g: (tile(*g), 0)),
             pl.BlockSpec((TM, LANES), lambda *g: (tile(*g), 0)),
             pl.BlockSpec((TM, LANES), lambda *g: (tile(*g), 0)),
             pl.BlockSpec((SUBLANES_F32, LANES), lambda *g: (0, 0)))
    return shapes, specs


TAIL_SCRATCH = [pltpu.VMEM((SUBLANES_F32, LANES), jnp.float32)]


def _seq_flags(lay, i):
    _, w = lay.split(i)
    has_prev = jnp.logical_and(w != 0, w != lay.LT)
    has_next = jnp.logical_and(w != lay.LT - 1, w != lay.TPB - 1)
    return has_prev, has_next


def _halo_specs(lay, width, col=lambda *g: 0, tile=lambda i: i):
    per = TM // HALO
    last = lay.T // HALO - 1
    prev = pl.BlockSpec((HALO, width), lambda *g: (jnp.maximum(tile(*g) * per - 1, 0), col(*g)))
    nxt = pl.BlockSpec((HALO, width), lambda *g: (jnp.minimum((tile(*g) + 1) * per, last), col(*g)))
    return prev, nxt


def _convin_kernel(u_ref, wb_ref, wc_ref, wx_ref, bg_ref, z_ref):
    x = u_ref[...]
    bg_ref[...] = _dot(x, wb_ref[...]).astype(bg_ref.dtype)
    z_ref[...] = (_dot(x, wc_ref[...]) * _dot(x, wx_ref[...])).astype(z_ref.dtype)


def _conv_in(lay, u, w_in):
    D = lay.D
    tn = 1024
    nb = D // tn
    out = jax.ShapeDtypeStruct((lay.T, D), jnp.bfloat16)
    return pl.pallas_call(
        _convin_kernel,
        out_shape=(out, out),
        grid=(nb, lay.NT),
        in_specs=[pl.BlockSpec((TM, D), lambda j, i: (i, 0)),
                  pl.BlockSpec((D, tn), lambda j, i: (0, j)),
                  pl.BlockSpec((D, tn), lambda j, i: (0, nb + j)),
                  pl.BlockSpec((D, tn), lambda j, i: (0, 2 * nb + j))],
        out_specs=(pl.BlockSpec((TM, tn), lambda j, i: (i, j)),
                   pl.BlockSpec((TM, tn), lambda j, i: (i, j))),
        compiler_params=_cparams(("arbitrary", "arbitrary")),
        name="conv_in",
    )(u, w_in, w_in, w_in)


def _convout_kernel(lay, z_ref, zp_ref, zn_ref, bg_ref, ck_ref, wo_ref, *rest):
    i = pl.program_id(0)
    has_prev, has_next = _seq_flags(lay, i)
    z = z_ref[...].astype(jnp.float32)
    row = lax.broadcasted_iota(jnp.int32, z.shape, 0)
    prev_row = jnp.where(has_prev, zp_ref[HALO - 1:HALO, :].astype(jnp.float32), 0.0)
    next_row = jnp.where(has_next, zn_ref[0:1, :].astype(jnp.float32), 0.0)
    z_m1 = jnp.where(row == 0, prev_row, pltpu.roll(z, 1, 0))
    z_p1 = jnp.where(row == TM - 1, next_row, pltpu.roll(z, TM - 1, 0))
    y = z_m1 * ck_ref[0:1, :] + z * ck_ref[1:2, :] + z_p1 * ck_ref[2:3, :]
    g = (bg_ref[...].astype(jnp.float32) * y).astype(jnp.bfloat16)
    _mixer_tail(_dot(g, wo_ref[...]), *rest)


def _conv_out(lay, z, bg, conv_k, w_out, h, mod, gain, wr, br):
    D = lay.D
    tile = lambda i: i
    prev, nxt = _halo_specs(lay, D)
    shapes, specs = _tail_out(lay, tile)
    return pl.pallas_call(
        functools.partial(_convout_kernel, lay),
        out_shape=shapes,
        grid=(lay.NT,),
        in_specs=[pl.BlockSpec((TM, D), lambda i: (i, 0)), prev, nxt,
                  pl.BlockSpec((TM, D), lambda i: (i, 0)),
                  pl.BlockSpec(conv_k.shape, lambda i: (0, 0)),
                  pl.BlockSpec((D, D), lambda i: (0, 0))] + _tail_in_specs(lay, tile),
        out_specs=specs,
        scratch_shapes=TAIL_SCRATCH,
        compiler_params=_cparams(("arbitrary",)),
        name="conv_out",
    )(z, z, z, bg, conv_k, w_out, h, mod, gain.reshape(1, D), mod, mod, wr, br)


def _qkv_kernel(head_dim, use_norm, q_scale, u_ref, w_ref, qg_ref, kg_ref, cos_ref, sin_ref, q_ref, k_ref, v_ref):
    y = _dot(u_ref[...], w_ref[...])
    nq, nk = q_ref.shape[1], k_ref.shape[1]
    lane = lax.broadcasted_iota(jnp.int32, (TM, LANES), 1)

    def rotary(yg, gain_ref):
        if use_norm:
            yg = _rms(yg, gain_ref[...])
        if head_dim == LANES:
            rot = pltpu.roll(yg, LANES // 2, 1)
        else:
            q = head_dim // 2
            rot = jnp.where(lane % head_dim < q, pltpu.roll(yg, LANES - q, 1), pltpu.roll(yg, q, 1))
        return yg * cos_ref[...] + rot * sin_ref[...]

    for g in range(nq // LANES):
        cols = slice(g * LANES, (g + 1) * LANES)
        q_ref[:, cols] = (rotary(y[:, cols], qg_ref) * q_scale).astype(q_ref.dtype)
    for g in range(nk // LANES):
        cols = slice(g * LANES, (g + 1) * LANES)
        k_ref[:, cols] = rotary(y[:, nq + g * LANES:nq + (g + 1) * LANES], kg_ref).astype(k_ref.dtype)
    v_ref[...] = y[:, nq + nk:].astype(v_ref.dtype)


def _qkv_project(lay, u, w, nq, nk, head_dim, q_gain, k_gain, cos_t, sin_t, use_norm, q_scale, name):
    D = lay.D
    row = lambda n: pl.BlockSpec((TM, n), lambda i: (i, 0))
    table = pl.BlockSpec((TM, LANES), lambda i: (i % lay.TPB, 0))
    gain = pl.BlockSpec((1, LANES), lambda i: (0, 0))
    bf = jnp.bfloat16
    return pl.pallas_call(
        functools.partial(_qkv_kernel, head_dim, use_norm, q_scale),
        out_shape=(jax.ShapeDtypeStruct((lay.T, nq), bf), jax.ShapeDtypeStruct((lay.T, nk), bf),
                   jax.ShapeDtypeStruct((lay.T, nk), bf)),
        grid=(lay.NT,),
        in_specs=[row(D), pl.BlockSpec((D, nq + 2 * nk), lambda i: (0, 0)), gain, gain, table, table],
        out_specs=(row(nq), row(nk), row(nk)),
        compiler_params=_cparams(("arbitrary",)),
        name=name,
    )(u, w, q_gain, k_gain, cos_t, sin_t)


def _rope_tables(lay, head_dim):
    quarter = head_dim // 4
    rows = lay.S // GRID_W
    row = jnp.repeat(jnp.arange(rows), GRID_W).astype(jnp.float32)
    col = jnp.tile(jnp.arange(GRID_W), rows).astype(jnp.float32)
    inv = ROPE_THETA ** (-jnp.arange(quarter, dtype=jnp.float32) / quarter)
    ang = jnp.concatenate([row[:, None] * inv, col[:, None] * inv], axis=-1)
    cos, sin = jnp.cos(ang), jnp.sin(ang)
    reps = LANES // head_dim
    cos_t = jnp.tile(jnp.concatenate([cos, cos], axis=-1), (1, reps))
    sin_t = jnp.tile(jnp.concatenate([-sin, sin], axis=-1), (1, reps))
    cos_t = jnp.concatenate([cos_t, jnp.ones((lay.C, LANES), jnp.float32)], axis=0)
    sin_t = jnp.concatenate([sin_t, jnp.zeros((lay.C, LANES), jnp.float32)], axis=0)
    return cos_t, sin_t


def _flash_kernel(n_rep, tk, q_ref, qn_ref, k_ref, v_ref, o_ref, vx_ref, s_ref, m_ref, l_ref, acc_ref):
    hd = LANES
    nk = k_ref.shape[0] // tk
    first = pl.program_id(2) == 0
    ring = nk >= 3 and nk % 2 == 1

    def rows(j):
        start = j * tk
        return pl.ds(start if isinstance(start, int) else pl.multiple_of(start, tk), tk)

    def scores(qsrc, j, slot):
        k = k_ref[rows(j), :]
        for h in range(n_rep):
            s_ref[slot, h] = lax.dot_general(qsrc[:, h * hd:(h + 1) * hd], k, (((1,), (1,)), ((), ())),
                                             preferred_element_type=jnp.float32)

    def step(j, rd, wr=None, qsrc=None, jn=None):
        vx = vx_ref[rows(j), :]
        if wr is not None:
            k_next = k_ref[rows(jn), :]
        for h in range(n_rep):
            if wr is not None:
                s_ref[wr, h] = lax.dot_general(qsrc[:, h * hd:(h + 1) * hd], k_next, (((1,), (1,)), ((), ())),
                                               preferred_element_type=jnp.float32)
            s = s_ref[rd, h]
            m_prev = m_ref[h]
            m_next = jnp.maximum(m_prev, jnp.max(s, axis=1, keepdims=True))
            alpha = jnp.exp2(m_prev - m_next)
            p = jnp.exp2(s - m_next[:, :1]).astype(jnp.bfloat16)
            pv = _dot(p, vx)
            m_ref[h] = m_next
            l_ref[h] = alpha * l_ref[h] + pv[:, hd:]
            acc_ref[h] = alpha * acc_ref[h] + pv[:, :hd]

    @pl.when(first)
    def _():
        vx_ref[:, :hd] = v_ref[...]
        vx_ref[:, hd:] = jnp.ones((vx_ref.shape[0], hd), vx_ref.dtype)

    m_ref[...] = jnp.full(m_ref.shape, NEG_INF, jnp.float32)
    l_ref[...] = jnp.zeros(l_ref.shape, jnp.float32)
    acc_ref[...] = jnp.zeros(acc_ref.shape, jnp.float32)

    if ring:
        @pl.when(first)
        def _():
            scores(q_ref, 0, 2)

        step(0, 2, 1, q_ref, 1)

        def body(t, _):
            step(2 * t + 1, 1, 0, q_ref, 2 * t + 2)
            step(2 * t + 2, 0, 1, q_ref, 2 * t + 3)
            return 0

        lax.fori_loop(0, (nk - 3) // 2, body, 0)
        step(nk - 2, 1, 0, q_ref, nk - 1)
        step(nk - 1, 0, 2, qn_ref, 0)
    else:
        scores(q_ref, 0, 0)
        for j in range(nk):
            step(j, j % 2, *((1 - j % 2, q_ref, j + 1) if j + 1 < nk else ()))
    for h in range(n_rep):
        o_ref[:, h * hd:(h + 1) * hd] = (acc_ref[h] / l_ref[h]).astype(o_ref.dtype)


def _flash(lay, q, k, v, n_kv, n_rep, tq, tk, q_blk0, nq, kb, k_blk0, name):
    B = lay.B
    qw = n_rep * LANES
    return pl.pallas_call(
        functools.partial(_flash_kernel, n_rep, tk),
        out_shape=jax.ShapeDtypeStruct((B, nq * tq, n_kv * qw), jnp.bfloat16),
        grid=(B, n_kv, nq),
        in_specs=[pl.BlockSpec((None, tq, qw), lambda b, g, i: (b, q_blk0 + i, g)),
                  pl.BlockSpec((None, tq, qw), lambda b, g, i: (b, q_blk0 + jnp.minimum(i + 1, nq - 1), g)),
                  pl.BlockSpec((None, kb, LANES), lambda b, g, i: (b, k_blk0, g)),
                  pl.BlockSpec((None, kb, LANES), lambda b, g, i: (b, k_blk0, g))],
        out_specs=pl.BlockSpec((None, tq, qw), lambda b, g, i: (b, i, g)),
        scratch_shapes=[pltpu.VMEM((kb, 2 * LANES), jnp.bfloat16),
                        pltpu.VMEM((3, n_rep, tq, tk), jnp.float32),
                        pltpu.VMEM((n_rep, tq, LANES), jnp.float32),
                        pltpu.VMEM((n_rep, tq, LANES), jnp.float32),
                        pltpu.VMEM((n_rep, tq, LANES), jnp.float32)],
        compiler_params=_cparams(("arbitrary", "arbitrary", "arbitrary")),
        name=name,
    )(q, q, k, v)


def _swa_kernel(lay, tq, sink_ref, q_ref, kp_ref, kc_ref, kn_ref, kx_ref, vp_ref, vc_ref, vn_ref, vx_ref, o_ref):
    pair = pl.program_id(1)
    qb = pl.program_id(2)
    hd = LANES // 2
    n_rep = SWA_HEADS // SWA_KV_HEADS
    W = SWA_WINDOW
    kk = jnp.concatenate([kp_ref[...], kc_ref[...], kn_ref[...], kx_ref[...]], axis=0)
    vv = jnp.concatenate([vp_ref[...], vc_ref[...], vn_ref[...], vx_ref[...]], axis=0)
    nkeys = kk.shape[0]
    nwin = tq + 2 * W
    lane = lax.broadcasted_iota(jnp.int32, (nkeys, LANES), 1)
    low = lane < hd
    kk_sw = pltpu.roll(kk.astype(jnp.float32), hd, 1).astype(kk.dtype)
    vv_sw = pltpu.roll(vv.astype(jnp.float32), hd, 1).astype(vv.dtype)
    k_dup = (jnp.where(low, kk, kk_sw), jnp.where(low, kk_sw, kk))
    v_dup = (jnp.where(low, vv, vv_sw), jnp.where(low, vv_sw, vv))

    start = qb * tq
    qpos = start + lax.broadcasted_iota(jnp.int32, (tq, nkeys), 0)
    col = lax.broadcasted_iota(jnp.int32, (tq, nkeys), 1)
    kpos = start - W + col
    in_win = jnp.logical_and(jnp.abs(qpos - kpos) <= W, jnp.logical_and(kpos >= 0, kpos < lay.S))
    valid = jnp.logical_or(col >= nwin, in_win)

    qlane = lax.broadcasted_iota(jnp.int32, (tq, LANES), 1)
    qlow = qlane < hd
    for g in range(2 * n_rep * hd // LANES):
        qg = q_ref[:, g * LANES:(g + 1) * LANES]
        kvh = (2 * g) // n_rep
        out = None
        for half in range(2):
            head = pair * 2 * n_rep + 2 * g + half
            qm = jnp.where(qlow if half == 0 else jnp.logical_not(qlow), qg, jnp.zeros_like(qg))
            s = lax.dot_general(qm, k_dup[kvh], (((1,), (1,)), ((), ())), preferred_element_type=jnp.float32)
            s = jnp.where(valid, s, NEG_INF)
            sink = sink_ref[head]
            m = jnp.maximum(jnp.max(s, axis=1, keepdims=True), sink)
            p = jnp.exp2(s - m)
            l = jnp.sum(p, axis=1, keepdims=True) + jnp.exp2(sink - m)
            o = _dot(p.astype(jnp.bfloat16), v_dup[kvh]) / l
            out = o if half == 0 else jnp.where(qlow, out, o)
        o_ref[:, g * LANES:(g + 1) * LANES] = out.astype(o_ref.dtype)


def _swa(lay, q, k, v, sink2):
    B, S = lay.B, lay.S
    tq = TM
    W = SWA_WINDOW
    per = tq // W
    nq = S // tq
    qw = 2 * (SWA_HEADS // SWA_KV_HEADS) * (LANES // 2)
    ctx_blk = S // lay.C
    kv_specs = [pl.BlockSpec((None, W, LANES), lambda b, p, i: (b, jnp.maximum(i * per - 1, 0), p)),
                pl.BlockSpec((None, tq, LANES), lambda b, p, i: (b, i, p)),
                pl.BlockSpec((None, W, LANES), lambda b, p, i: (b, (i + 1) * per, p)),
                pl.BlockSpec((None, lay.C, LANES), lambda b, p, i: (b, ctx_blk, p))]
    return pl.pallas_call(
        functools.partial(_swa_kernel, lay, tq),
        out_shape=jax.ShapeDtypeStruct((B, S, lay.D), jnp.bfloat16),
        grid=(B, SWA_KV_HEADS // 2, nq),
        in_specs=[pl.BlockSpec(memory_space=pltpu.SMEM),
                  pl.BlockSpec((None, tq, qw), lambda b, p, i: (b, i, p))] + kv_specs + kv_specs,
        out_specs=pl.BlockSpec((None, tq, qw), lambda b, p, i: (b, i, p)),
        compiler_params=_cparams(("arbitrary", "arbitrary", "arbitrary")),
        name="swa",
    )(sink2, q, k, k, k, k, v, v, v, v)


def _outproj_kernel(lay, has_ctx, ol_ref, oc_ref, wo_ref, *rest):
    x = ol_ref[...]
    if has_ctx:
        _, w = lay.split(pl.program_id(0))
        x = jnp.where(w >= lay.LT, oc_ref[...], x)
    _mixer_tail(_dot(x, wo_ref[...]), *rest)


def _out_proj(lay, o_lat, o_ctx, w_out, h, mod, gain, wr, br):
    D = lay.D
    tile = lambda i: i
    has_ctx = o_ctx is not None
    if not has_ctx:
        o_ctx = o_lat
    shapes, specs = _tail_out(lay, tile)
    lat_spec, ctx_spec = _lat_ctx_specs(lay)
    return pl.pallas_call(
        functools.partial(_outproj_kernel, lay, has_ctx),
        out_shape=shapes,
        grid=(lay.NT,),
        in_specs=[lat_spec, ctx_spec if has_ctx else lat_spec,
                  pl.BlockSpec((D, D), lambda i: (0, 0))] + _tail_in_specs(lay, tile),
        out_specs=specs,
        scratch_shapes=TAIL_SCRATCH,
        compiler_params=_cparams(("arbitrary",)),
        name="attn_out",
    )(o_lat, o_ctx, w_out, h, mod, gain.reshape(1, D), mod, mod, wr, br)


def _pool_kernel(lay, u_ref, up_ref, un_ref, pw_ref, ps_ref, *rest):
    i = pl.program_id(0)
    _, w = lay.split(i)
    has_prev, has_next = _seq_flags(lay, i)
    in_ctx = w >= lay.LT
    seq_len = jnp.where(in_ctx, lay.C, lay.S)
    pos0 = jnp.where(in_ctx, w - lay.LT, w) * TM
    G = len(POOL_WINDOWS)
    gw = lay.D // G
    E_ROWS = TM + 2 * SUBLANES_F32
    pos = pos0 + lax.broadcasted_iota(jnp.int32, (TM, gw), 0)
    ys = []
    for g, win in enumerate(POOL_WINDOWS):
        sl = slice(g * gw, (g + 1) * gw)
        u = u_ref[:, sl].astype(jnp.float32)
        before = jnp.where(has_prev, up_ref[HALO - SUBLANES_F32:HALO, sl].astype(jnp.float32), 0.0)
        after = jnp.where(has_next, un_ref[0:SUBLANES_F32, sl].astype(jnp.float32), 0.0)
        e = jnp.concatenate([before, u, after], axis=0)
        left = win // 2
        right = win - 1 - left
        assert left == right + 1 and left & (left - 1) == 0
        acc = e
        span = 1
        while span < left:
            acc = acc + pltpu.roll(acc, E_ROWS - span, 0)
            span *= 2
        tot = pltpu.roll(acc, left, 0) + acc
        total = tot[SUBLANES_F32:SUBLANES_F32 + TM]
        cnt = jnp.minimum(pos + right, seq_len - 1) - jnp.maximum(pos - left, 0) + 1
        mean = total / cnt.astype(jnp.float32)
        ys.append(_dot((mean - u).astype(jnp.bfloat16), pw_ref[g]))
    y = jnp.concatenate(ys, axis=1) * ps_ref[...]
    _mixer_tail(y, *rest)


def _pool(lay, u, pool_w, pool_scale, h, mod, gain, wr, br):
    D = lay.D
    tile = lambda i: i
    prev, nxt = _halo_specs(lay, D)
    shapes, specs = _tail_out(lay, tile)
    return pl.pallas_call(
        functools.partial(_pool_kernel, lay),
        out_shape=shapes,
        grid=(lay.NT,),
        in_specs=[pl.BlockSpec((TM, D), lambda i: (i, 0)), prev, nxt,
                  pl.BlockSpec(pool_w.shape, lambda i: (0, 0, 0)),
                  _row_spec(D)] + _tail_in_specs(lay, tile),
        out_specs=specs,
        scratch_shapes=TAIL_SCRATCH,
        compiler_params=_cparams(("arbitrary",)),
        name="pool",
    )(u, u, u, pool_w, pool_scale.reshape(1, D), h, mod, gain.reshape(1, D), mod, mod, wr, br)


def _route_tile(lg, info_ref, gate_ref, cnt_ref, carry_ref):
    @pl.when(pl.program_id(0) == 0)
    def _():
        carry_ref[...] = jnp.zeros(carry_ref.shape, jnp.float32)

    lane = lax.broadcasted_iota(jnp.int32, lg.shape, 1)
    lane_f = lane.astype(jnp.float32)
    big = jnp.float32(4 * LANES)

    def first_lane(mask):
        return jnp.min(jnp.where(mask, lane_f, big), axis=1, keepdims=True).astype(jnp.int32)

    is_grp = lane < N_GROUPS
    gl = jnp.where(is_grp, lg, NEG_INF)
    gmax = jnp.max(gl, axis=1, keepdims=True)
    grp = first_lane(jnp.logical_and(is_grp, gl == gmax))
    p_grp = 1.0 / jnp.sum(jnp.where(is_grp, jnp.exp(gl - gmax), 0.0), axis=1, keepdims=True)
    eid = lane - N_GROUPS
    in_grp = jnp.logical_and(lane >= N_GROUPS + grp * EXPERTS_PER_GROUP,
                             lane < N_GROUPS + (grp + 1) * EXPERTS_PER_GROUP)
    el = jnp.where(in_grp, lg, NEG_INF)
    t1 = jnp.max(el, axis=1, keepdims=True)
    e1 = first_lane(jnp.logical_and(in_grp, el == t1)) - N_GROUPS
    rest = jnp.logical_and(in_grp, eid != e1)
    el2 = jnp.where(rest, lg, NEG_INF)
    t2 = jnp.max(el2, axis=1, keepdims=True)
    e2 = first_lane(jnp.logical_and(rest, el2 == t2)) - N_GROUPS
    d = jnp.exp(t2 - t1)
    g1 = p_grp / (1.0 + d)
    g2 = p_grp * d / (1.0 + d)

    oh1 = lane == e1
    oh2 = lane == e2
    oh = jnp.where(jnp.logical_or(oh1, oh2), 1.0, 0.0)
    r = lax.broadcasted_iota(jnp.int32, (TM, TM), 0)
    c = lax.broadcasted_iota(jnp.int32, (TM, TM), 1)
    tri = jnp.where(c < r, 1.0, 0.0).astype(jnp.bfloat16)
    before = _dot(tri, oh.astype(jnp.bfloat16)) + carry_ref[0:1, :]
    r1 = jnp.sum(jnp.where(oh1, before, 0.0), axis=1, keepdims=True)
    r2 = jnp.sum(jnp.where(oh2, before, 0.0), axis=1, keepdims=True)
    carry = carry_ref[0:1, :] + jnp.sum(oh, axis=0, keepdims=True)
    carry_ref[...] = jnp.broadcast_to(carry, carry_ref.shape)

    info = jnp.where(lane == 0, e1, jnp.where(lane == 1, e2, jnp.where(
        lane == 2, r1.astype(jnp.int32), jnp.where(lane == 3, r2.astype(jnp.int32), 0))))
    info_ref[...] = info
    gate_ref[...] = jnp.where(lane == 0, g1, jnp.where(lane == 1, g2, 0.0))
    cnt_ref[...] = jnp.broadcast_to(carry, cnt_ref.shape).astype(jnp.int32)


def _dispatch_kernel(n_blocks, plan_ref, dest_ref, v_ref, xs_hbm, zbuf, sem, zsem):
    @pl.when(pl.program_id(0) == 0)
    def _():
        zbuf[...] = jnp.zeros(zbuf.shape, zbuf.dtype)

        def zero_block(row0):
            rows = pl.ds(pl.multiple_of(row0, EXPERT_ROWS), EXPERT_ROWS)
            return pltpu.make_async_copy(zbuf, xs_hbm.at[rows], zsem)

        def for_each_zero_block(fn):
            def seg(e, _):
                @pl.when(plan_ref[N_EXPERTS + e] > 0)
                def _():
                    fn(zero_block(plan_ref[e] - EXPERT_ROWS))
                return 0

            def tail(b, _):
                fn(zero_block(b * EXPERT_ROWS))
                return 0

            lax.fori_loop(0, N_EXPERTS, seg, 0)
            lax.fori_loop(plan_ref[2 * N_EXPERTS], n_blocks, tail, 0)

        for_each_zero_block(lambda cp: cp.start())
        for_each_zero_block(lambda cp: cp.wait())

    def issue(r, _):
        for k in range(TOP_K):
            pltpu.make_async_copy(v_ref.at[pl.ds(r, 1)], xs_hbm.at[pl.ds(dest_ref[0, k, r], 1)], sem).start()
        return 0

    lax.fori_loop(0, TM, issue, 0, unroll=True)
    for k in range(TOP_K):
        pltpu.make_async_copy(v_ref, xs_hbm.at[pl.ds(0, TM)], sem).wait()


def _dispatch(lay, v, dest, plan, n_blocks):
    W = v.shape[1]
    return pl.pallas_call(
        functools.partial(_dispatch_kernel, n_blocks),
        out_shape=jax.ShapeDtypeStruct((n_blocks * EXPERT_ROWS, W), v.dtype),
        grid=(lay.NT,),
        in_specs=[pl.BlockSpec(memory_space=pltpu.SMEM),
                  pl.BlockSpec((1, TOP_K, TM), lambda i: (i, 0, 0), memory_space=pltpu.SMEM),
                  pl.BlockSpec((TM, W), lambda i: (i, 0))],
        out_specs=pl.BlockSpec(memory_space=pl.ANY),
        scratch_shapes=[pltpu.VMEM((EXPERT_ROWS, W), v.dtype),
                        pltpu.SemaphoreType.DMA(()), pltpu.SemaphoreType.DMA(())],
        compiler_params=pltpu.CompilerParams(dimension_semantics=("arbitrary",), has_side_effects=True),
        name="moe_dispatch",
    )(plan, dest, v)


def _expert_kernel(layer, be_ref, na_ref, nxt_ref, slot_ref, x_ref, wg_hbm, wu_hbm, wd_hbm, y_ref,
                   wg_buf, wu_buf, wd_buf, wgu_s, wd_s, sem):
    i = pl.program_id(0)
    de = wg_buf.shape[2]
    active = i < na_ref[0]
    e = be_ref[i]

    def fetch(expert, slot):
        return (pltpu.make_async_copy(wg_hbm.at[layer, expert], wg_buf.at[slot], sem.at[slot, 0]),
                pltpu.make_async_copy(wu_hbm.at[layer, expert], wu_buf.at[slot], sem.at[slot, 1]),
                pltpu.make_async_copy(wd_hbm.at[layer, expert], wd_buf.at[slot], sem.at[slot, 2]))

    @pl.when(i == 0)
    def _():
        for cp in fetch(e, slot_ref[e]):
            cp.start()

    @pl.when(jnp.logical_and(active, jnp.logical_or(i == 0, e != be_ref[jnp.maximum(i - 1, 0)])))
    def _():
        slot = slot_ref[e]
        for cp in fetch(e, slot):
            cp.wait()

        @pl.when(nxt_ref[e] >= 0)
        def _():
            for cp in fetch(nxt_ref[e], 1 - slot):
                cp.start()

        wgu_s[:, :de] = wg_buf[slot].astype(wgu_s.dtype)
        wgu_s[:, de:] = wu_buf[slot].astype(wgu_s.dtype)
        wd_s[...] = wd_buf[slot].astype(wd_s.dtype)

    @pl.when(active)
    def _():
        hi, lo = _unpack_rows(x_ref[...])
        x = jnp.concatenate([hi.astype(jnp.bfloat16), lo.astype(jnp.bfloat16)], axis=1)
        hgu = _dot(x, wgu_s[...])
        hg, hu = hgu[:, :de], hgu[:, de:]
        a = (hg * (1.0 / (1.0 + jnp.exp(-hg)))) * hu
        y_ref[...] = _pack_rows(_dot(a.astype(jnp.bfloat16), wd_s[...]))

    @pl.when(jnp.logical_not(active))
    def _():
        y_ref[...] = jnp.zeros(y_ref.shape, y_ref.dtype)


def _experts(lay, xs, blk_expert, n_active, nxt, slot, layer, wg, wu, wd):
    D = lay.D
    P, W = xs.shape
    DE = wg.shape[3]
    row_block = pl.BlockSpec((EXPERT_ROWS, W), lambda i, *_: (i, 0))
    hbm = pl.BlockSpec(memory_space=pl.ANY)
    grid_spec = pltpu.PrefetchScalarGridSpec(
        num_scalar_prefetch=4,
        grid=(P // EXPERT_ROWS,),
        in_specs=[row_block, hbm, hbm, hbm],
        out_specs=row_block,
        scratch_shapes=[pltpu.VMEM((2, D, DE), jnp.float32), pltpu.VMEM((2, D, DE), jnp.float32),
                        pltpu.VMEM((2, DE, D), jnp.float32),
                        pltpu.VMEM((D, 2 * DE), jnp.bfloat16), pltpu.VMEM((DE, D), jnp.bfloat16),
                        pltpu.SemaphoreType.DMA((2, 3))],
    )
    return pl.pallas_call(
        functools.partial(_expert_kernel, layer),
        out_shape=jax.ShapeDtypeStruct((P, W), xs.dtype),
        grid_spec=grid_spec,
        compiler_params=_cparams(("arbitrary",)),
        name="moe_experts",
    )(blk_expert, n_active, nxt, slot, xs, wg, wu, wd)


def _combine_kernel(final, dest_ref, destn_ref, ys_hbm, gate_ref, h_ref, g2_ref, gain_ref, sc_ref, sh_ref,
                    *refs):
    *out_refs, ybuf, sem = refs
    i = pl.program_id(0)

    def gather(dref, buf_slot, r0, r1):
        for r in range(r0, r1):
            for k in range(TOP_K):
                pltpu.make_async_copy(ys_hbm.at[pl.ds(dref[0, k, r], 1)], ybuf.at[buf_slot, k, pl.ds(r, 1)],
                                      sem.at[buf_slot]).start()

    def wait_tile(buf_slot):
        for k in range(TOP_K):
            pltpu.make_async_copy(ys_hbm.at[pl.ds(0, TM)], ybuf.at[buf_slot, k], sem.at[buf_slot]).wait()

    @pl.when(i == 0)
    def _():
        gather(dest_ref, 0, 0, TM)

    def step(slot):
        wait_tile(slot)
        for r0 in range(0, TM, COMBINE_ROWS):
            rows = pl.ds(r0, COMBINE_ROWS)
            gate = gate_ref[rows, :]
            hi0, lo0 = _unpack_rows(ybuf[slot, 0, rows, :])
            hi1, lo1 = _unpack_rows(ybuf[slot, 1, rows, :])
            g0, g1 = gate[:, 0:1], gate[:, 1:2]
            f = jnp.concatenate([g0 * hi0 + g1 * hi1, g0 * lo0 + g1 * lo1], axis=1)
            h2 = h_ref[rows, :] + g2_ref[...] * f
            if final:
                (out_ref,) = out_refs
                out_ref[rows, :] = _rms(h2, gain_ref[...])
            else:
                h2_ref, u_ref = out_refs
                h2_ref[rows, :] = h2
                u_ref[rows, :] = (_rms(h2, gain_ref[...]) * (1.0 + sc_ref[...]) + sh_ref[...]).astype(u_ref.dtype)
            gather(destn_ref, 1 - slot, r0, r0 + COMBINE_ROWS)

        @pl.when(i == pl.num_programs(0) - 1)
        def _():
            wait_tile(1 - slot)

    for slot in range(2):
        pl.when(i % 2 == slot)(functools.partial(step, slot))


def _combine(lay, ys, dest, gates, h1, mod, mod_next, gain_next, final):
    D = lay.D
    row_tile = pl.BlockSpec((TM, D), lambda i: (i, 0))
    if final:
        n = lay.B * lay.LT
        tile = lambda i: (i // lay.LT) * lay.TPB + i % lay.LT
        out_shape = (jax.ShapeDtypeStruct((lay.B * lay.S, D), jnp.float32),)
        out_specs = (row_tile,)
    else:
        n = lay.NT
        tile = lambda i: i
        out_shape = (jax.ShapeDtypeStruct((lay.T, D), jnp.float32),
                     jax.ShapeDtypeStruct((lay.T, D), jnp.bfloat16))
        out_specs = (row_tile, row_tile)
    return pl.pallas_call(
        functools.partial(_combine_kernel, final),
        out_shape=out_shape,
        grid=(n,),
        in_specs=[pl.BlockSpec((1, TOP_K, TM), lambda i: (tile(i), 0, 0), memory_space=pltpu.SMEM),
                  pl.BlockSpec((1, TOP_K, TM), lambda i: (tile(jnp.minimum(i + 1, n - 1)), 0, 0),
                               memory_space=pltpu.SMEM),
                  pl.BlockSpec(memory_space=pl.ANY),
                  pl.BlockSpec((TM, LANES), lambda i: (tile(i), 0)),
                  pl.BlockSpec((TM, D), lambda i: (tile(i), 0)),
                  _mod_spec(lay, 5, tile),
                  _row_spec(D),
                  _mod_spec(lay, 1, tile), _mod_spec(lay, 0, tile)],
        out_specs=out_specs,
        scratch_shapes=[pltpu.VMEM((2, TOP_K, TM, ys.shape[1]), ys.dtype), pltpu.SemaphoreType.DMA((2,))],
        compiler_params=_cparams(("arbitrary",)),
        name="moe_combine_final" if final else "moe_combine",
    )(dest, dest, ys, gates, h1, mod, gain_next.reshape(1, D), mod_next, mod_next)


def _moe(lay, v, info, gates, cnt, h1, mod, mod_next, gain_next, layer, wg, wu, wd, final):
    T, NT = lay.T, lay.NT
    counts = cnt[0, :N_EXPERTS]
    padded = ((counts + EXPERT_ROWS - 1) // EXPERT_ROWS) * EXPERT_ROWS
    pad_end = jnp.cumsum(padded)
    pad_start = pad_end - padded
    expert = info[:, :TOP_K]
    onehot = expert[:, :, None] == jnp.arange(N_EXPERTS, dtype=jnp.int32)
    dest = info[:, TOP_K:2 * TOP_K] + jnp.sum(jnp.where(onehot, pad_start, 0), axis=-1)
    dest = dest.reshape(NT, TM, TOP_K).transpose(0, 2, 1).astype(jnp.int32)
    n_blocks = -(-(T * TOP_K + N_EXPERTS * (EXPERT_ROWS - 1)) // EXPERT_ROWS)
    blk_start = jnp.arange(n_blocks, dtype=jnp.int32) * EXPERT_ROWS
    blk_expert = jnp.minimum(jnp.sum(pad_end[None, :] <= blk_start[:, None], axis=1),
                             N_EXPERTS - 1).astype(jnp.int32)
    n_active = (pad_end[-1:] // EXPERT_ROWS).astype(jnp.int32)
    plan = jnp.concatenate([pad_end, padded, n_active]).astype(jnp.int32)
    xs = _dispatch(lay, v, dest, plan, n_blocks)
    owns = padded > 0
    ids = jnp.arange(N_EXPERTS, dtype=jnp.int32)
    later = jnp.where(owns[None, :] & (ids[None, :] > ids[:, None]), ids[None, :], N_EXPERTS)
    nxt = jnp.min(later, axis=1)
    nxt = jnp.where(nxt == N_EXPERTS, -1, nxt).astype(jnp.int32)
    slot = ((jnp.cumsum(owns.astype(jnp.int32)) - 1) % 2).astype(jnp.int32)
    ys = _experts(lay, xs, blk_expert, n_active, nxt, slot, layer, wg, wu, wd)
    return _combine(lay, ys, dest, gates, h1, mod, mod_next, gain_next, final)


def kernel(x, c, ctx, c_ctx, ada_w, ada_b, norm_mix, norm_ffn, norm_final, conv_in, conv_k, conv_out, gqa_qkv, gqa_q_gain, gqa_k_gain, gqa_out, pool_w, pool_scale, swa_qkv, swa_sink, swa_out, router_grp_w, router_grp_b, router_exp_w, router_exp_b, exp_gate, exp_up, exp_down):
    B, S, D = x.shape
    C = ctx.shape[1]
    L = ada_w.shape[0]
    lay = Layout(B, S, C, D)
    bf = jnp.bfloat16

    R = SUBLANES_BF16
    cvec = jnp.zeros((R, D), jnp.float32).at[:B].set(c).at[B].set(c_ctx)
    mod_all = _modulation(cvec, ada_w, ada_b)[:, :B + 1].reshape(L, B + 1, 6, 1, D).transpose(0, 2, 1, 3, 4)

    pad = LANES - N_GROUPS - N_EXPERTS
    wr_all = jnp.concatenate([router_grp_w, router_exp_w, jnp.zeros((L, D, pad), jnp.float32)], axis=-1).astype(bf)
    br_all = jnp.concatenate([router_grp_b, router_exp_b, jnp.zeros((L, pad), jnp.float32)], axis=-1)

    h, u = _prenorm(lay, x, ctx, norm_mix[0], mod_all[0])
    one = jnp.ones((1, LANES), jnp.float32)

    for i in range(L):
        m, j = i % 4, i // 4
        mod = mod_all[i]
        wr, br = wr_all[i], br_all[i].reshape(1, LANES)
        tail = (h, mod, norm_ffn[i], wr, br)
        if m == 0:
            bg, z = _conv_in(lay, u, conv_in[j].astype(bf))
            routed = _conv_out(lay, z, bg, conv_k[j], conv_out[j].astype(bf), *tail)
        elif m == 1:
            hd = D // GQA_HEADS
            w = gqa_qkv[j].astype(bf)
            cos_t, sin_t = _rope_tables(lay, hd)
            nq, nk = GQA_HEADS * hd, GQA_KV_HEADS * hd
            qg = gqa_q_gain[j].reshape(1, hd)
            kg = gqa_k_gain[j].reshape(1, hd)
            q, k, vv = _qkv_project(lay, u, w, nq, nk, hd, qg, kg, cos_t, sin_t, True, hd ** -0.5 * LOG2E, "gqa_qkv")
            q3, k3, v3 = (a.reshape(B, lay.SB, -1) for a in (q, k, vv))
            n_rep = GQA_HEADS // GQA_KV_HEADS
            tq = 512 if S % 512 == 0 else TM
            tk = 768 if lay.SB % 768 == 0 else TM
            o_lat = _flash(lay, q3, k3, v3, GQA_KV_HEADS, n_rep, tq, tk, 0, S // tq, lay.SB, 0, "gqa_flash")
            o_ctx = _flash(lay, q3, k3, v3, GQA_KV_HEADS, n_rep, C, C, S // C, 1, C, S // C, "gqa_flash_ctx")
            routed = _out_proj(lay, o_lat, o_ctx, gqa_out[j].astype(bf), *tail)
        elif m == 2:
            routed = _pool(lay, u, pool_w[j].astype(bf), pool_scale[j], *tail)
        else:
            hd = D // SWA_HEADS
            w = swa_qkv[j].astype(bf)
            cos_t, sin_t = _rope_tables(lay, hd)
            nq, nk = SWA_HEADS * hd, SWA_KV_HEADS * hd
            q, k, vv = _qkv_project(lay, u, w, nq, nk, hd, one, one, cos_t, sin_t, False, hd ** -0.5 * LOG2E, "swa_qkv")
            q3, k3, v3 = (a.reshape(B, lay.SB, -1) for a in (q, k, vv))
            o_lat = _swa(lay, q3, k3, v3, swa_sink[j] * LOG2E)
            routed = _out_proj(lay, o_lat, None, swa_out[j].astype(bf), *tail)
        final = i == L - 1
        mod_next = mod if final else mod_all[i + 1]
        gain_next = norm_final if final else norm_mix[i + 1]
        h1, v, info, gates, cnt = routed
        res = _moe(lay, v, info, gates, cnt, h1, mod, mod_next, gain_next, i, exp_gate, exp_up, exp_down, final)
        if final:
            return res[0].reshape(B, S, D)
        h, u = res
```

```python
import functools

import jax
import jax.numpy as jnp
from jax import lax
from jax.experimental import pallas as pl
from jax.experimental.pallas import tpu as pltpu

GRID_W = 64
NORM_EPS = 1e-6
ROPE_THETA = 10000.0
NEG_INF = -1e30
GQA_HEADS, GQA_KV_HEADS = 16, 4
SWA_HEADS, SWA_KV_HEADS, SWA_WINDOW = 32, 8, 128
POOL_WINDOWS = (2, 4, 8, 16)
N_GROUPS, EXPERTS_PER_GROUP, TOP_K = 8, 4, 2
N_EXPERTS = N_GROUPS * EXPERTS_PER_GROUP
LOG2E = 1.4426950408889634

LANES = 128
SUBLANES_F32 = 8
SUBLANES_BF16 = 16
VMEM_LIMIT = 56 * 1024 * 1024
TM = 256
EXPERT_ROWS = 256
COMBINE_ROWS = 32
COMBINE_AHEAD = 2
DMA_PRIORITIES = 2
HALO = SUBLANES_BF16


def _cparams(sem):
    return pltpu.CompilerParams(dimension_semantics=sem, vmem_limit_bytes=VMEM_LIMIT)


class Layout:
    def __init__(self, B, S, C, D):
        assert S % TM == 0 and C % TM == 0
        self.B, self.S, self.C, self.D = B, S, C, D
        self.SB = S + C
        self.T = B * self.SB
        self.LT = S // TM
        self.TPB = self.SB // TM
        self.NT = B * self.TPB

    def split(self, i):
        return i // self.TPB, i % self.TPB

    def mod_row(self, i):
        b, w = self.split(i)
        return jnp.where(w >= self.LT, self.B, b)


def _mod_spec(lay, chunk, tile_of):
    return pl.BlockSpec((None, None, 1, lay.D),
                        lambda *g: (chunk, lay.mod_row(tile_of(*g)), 0, 0))


def _row_spec(D):
    return pl.BlockSpec((1, D), lambda *g: (0, 0))


def _rms(x, gain):
    ms = jnp.mean(x * x, axis=-1, keepdims=True)
    return (x * lax.rsqrt(ms + NORM_EPS)) * gain


def _dot(a, b):
    return jnp.dot(a, b, preferred_element_type=jnp.float32)


def _mod_kernel(c_ref, w_ref, b_ref, o_ref):
    c = c_ref[...]
    a = (c * (1.0 / (1.0 + jnp.exp(-c)))).astype(jnp.bfloat16)
    o_ref[...] = _dot(a, w_ref[...].astype(jnp.bfloat16)) + b_ref[...]


def _modulation(cvec, ada_w, ada_b):
    L, D, N = ada_w.shape
    R = cvec.shape[0]
    tn = 512
    return pl.pallas_call(
        _mod_kernel,
        out_shape=jax.ShapeDtypeStruct((L, R, N), jnp.float32),
        grid=(L, N // tn),
        in_specs=[pl.BlockSpec((R, D), lambda l, j: (0, 0)),
                  pl.BlockSpec((None, D, tn), lambda l, j: (l, 0, j)),
                  pl.BlockSpec((None, 1, tn), lambda l, j: (l, 0, j))],
        out_specs=pl.BlockSpec((None, R, tn), lambda l, j: (l, 0, j)),
        compiler_params=_cparams(("arbitrary", "arbitrary")),
        name="adaln_mod",
    )(cvec, ada_w, ada_b.reshape(L, 1, N))


def _prenorm_kernel(lay, x_ref, ctx_ref, gain_ref, sc_ref, sh_ref, h_ref, u_ref):
    _, w = lay.split(pl.program_id(0))
    h = jnp.where(w >= lay.LT, ctx_ref[...], x_ref[...])
    h_ref[...] = h
    u_ref[...] = (_rms(h, gain_ref[...]) * (1.0 + sc_ref[...]) + sh_ref[...]).astype(u_ref.dtype)


def _lat_ctx_specs(lay):
    def lat_idx(i):
        b, w = lay.split(i)
        return b, jnp.minimum(w, lay.LT - 1), 0

    def ctx_idx(i):
        b, w = lay.split(i)
        return b, jnp.clip(w - lay.LT, 0, lay.TPB - lay.LT - 1), 0

    return pl.BlockSpec((None, TM, lay.D), lat_idx), pl.BlockSpec((None, TM, lay.D), ctx_idx)


def _prenorm(lay, x, ctx, gain, mod):
    D = lay.D
    tile = lambda i: i
    row = pl.BlockSpec((TM, D), lambda i: (i, 0))
    return pl.pallas_call(
        functools.partial(_prenorm_kernel, lay),
        out_shape=(jax.ShapeDtypeStruct((lay.T, D), jnp.float32),
                   jax.ShapeDtypeStruct((lay.T, D), jnp.bfloat16)),
        grid=(lay.NT,),
        in_specs=[*_lat_ctx_specs(lay), _row_spec(D), _mod_spec(lay, 1, tile), _mod_spec(lay, 0, tile)],
        out_specs=(row, row),
        compiler_params=_cparams(("arbitrary",)),
        name="prenorm",
    )(x, ctx, gain.reshape(1, D), mod, mod)


def _pack_rows(x):
    w = x.shape[1] // 2
    hi = pltpu.bitcast(x[:, :w].astype(jnp.bfloat16).astype(jnp.float32), jnp.uint32)
    lo = pltpu.bitcast(x[:, w:].astype(jnp.bfloat16).astype(jnp.float32), jnp.uint32)
    return hi | (lo >> 16)


def _unpack_rows(words):
    hi = pltpu.bitcast(words & jnp.uint32(0xFFFF0000), jnp.float32)
    lo = pltpu.bitcast(words << 16, jnp.float32)
    return hi, lo


def _mixer_tail(y, h_ref, g1_ref, gain_ref, sc_ref, sh_ref, wr_ref, br_ref,
                h1_ref, v_ref, info_ref, gate_ref, cnt_ref, carry_ref):
    h1 = h_ref[...] + g1_ref[...] * y
    h1_ref[...] = h1
    v = _rms(h1, gain_ref[...]) * (1.0 + sc_ref[...]) + sh_ref[...]
    v_ref[...] = _pack_rows(v)
    logits = _dot(v.astype(jnp.bfloat16), wr_ref[...]) + br_ref[...]
    _route_tile(logits, info_ref, gate_ref, cnt_ref, carry_ref)


def _tail_in_specs(lay, tile):
    D = lay.D
    return [pl.BlockSpec((TM, D), lambda *g: (tile(*g), 0)),
            _mod_spec(lay, 2, tile),
            _row_spec(D),
            _mod_spec(lay, 4, tile), _mod_spec(lay, 3, tile),
            pl.BlockSpec((D, LANES), lambda *g: (0, 0)),
            pl.BlockSpec((1, LANES), lambda *g: (0, 0))]


def _tail_out(lay, tile):
    D = lay.D
    shapes = (jax.ShapeDtypeStruct((lay.T, D), jnp.float32),
              jax.ShapeDtypeStruct((lay.T, D // 2), jnp.uint32),
              jax.ShapeDtypeStruct((lay.T, LANES), jnp.int32),
              jax.ShapeDtypeStruct((lay.T, LANES), jnp.float32),
              jax.ShapeDtypeStruct((SUBLANES_F32, LANES), jnp.int32))
    specs = (pl.BlockSpec((TM, D), lambda *g: (tile(*g), 0)),
             pl.BlockSpec((TM, D // 2), lambda *g: (tile(*g), 0)),
             pl.BlockSpec((TM, LANES), lambda *g: (tile(*g), 0)),
             pl.BlockSpec((TM, LANES), lambda *g: (tile(*g), 0)),
             pl.BlockSpec((SUBLANES_F32, LANES), lambda *g: (0, 0)))
    return shapes, specs


TAIL_SCRATCH = [pltpu.VMEM((SUBLANES_F32, LANES), jnp.float32)]


def _seq_flags(lay, i):
    _, w = lay.split(i)
    has_prev = jnp.logical_and(w != 0, w != lay.LT)
    has_next = jnp.logical_and(w != lay.LT - 1, w != lay.TPB - 1)
    return has_prev, has_next


def _halo_specs(lay, width, col=lambda *g: 0, tile=lambda i: i):
    per = TM // HALO
    last = lay.T // HALO - 1
    prev = pl.BlockSpec((HALO, width), lambda *g: (jnp.maximum(tile(*g) * per - 1, 0), col(*g)))
    nxt = pl.BlockSpec((HALO, width), lambda *g: (jnp.minimum((tile(*g) + 1) * per, last), col(*g)))
    return prev, nxt


def _convin_kernel(u_ref, wb_ref, wc_ref, wx_ref, bg_ref, z_ref):
    x = u_ref[...]
    bg_ref[...] = _dot(x, wb_ref[...]).astype(bg_ref.dtype)
    z_ref[...] = (_dot(x, wc_ref[...]) * _dot(x, wx_ref[...])).astype(z_ref.dtype)


def _conv_in(lay, u, w_in):
    D = lay.D
    tn = 1024
    nb = D // tn
    out = jax.ShapeDtypeStruct((lay.T, D), jnp.bfloat16)
    return pl.pallas_call(
        _convin_kernel,
        out_shape=(out, out),
        grid=(nb, lay.NT),
        in_specs=[pl.BlockSpec((TM, D), lambda j, i: (i, 0)),
                  pl.BlockSpec((D, tn), lambda j, i: (0, j)),
                  pl.BlockSpec((D, tn), lambda j, i: (0, nb + j)),
                  pl.BlockSpec((D, tn), lambda j, i: (0, 2 * nb + j))],
        out_specs=(pl.BlockSpec((TM, tn), lambda j, i: (i, j)),
                   pl.BlockSpec((TM, tn), lambda j, i: (i, j))),
        compiler_params=_cparams(("arbitrary", "arbitrary")),
        name="conv_in",
    )(u, w_in, w_in, w_in)


def _convout_kernel(lay, z_ref, zp_ref, zn_ref, bg_ref, ck_ref, wo_ref, *rest):
    i = pl.program_id(0)
    has_prev, has_next = _seq_flags(lay, i)
    z = z_ref[...].astype(jnp.float32)
    row = lax.broadcasted_iota(jnp.int32, z.shape, 0)
    prev_row = jnp.where(has_prev, zp_ref[HALO - 1:HALO, :].astype(jnp.float32), 0.0)
    next_row = jnp.where(has_next, zn_ref[0:1, :].astype(jnp.float32), 0.0)
    z_m1 = jnp.where(row == 0, prev_row, pltpu.roll(z, 1, 0))
    z_p1 = jnp.where(row == TM - 1, next_row, pltpu.roll(z, TM - 1, 0))
    y = z_m1 * ck_ref[0:1, :] + z * ck_ref[1:2, :] + z_p1 * ck_ref[2:3, :]
    g = (bg_ref[...].astype(jnp.float32) * y).astype(jnp.bfloat16)
    _mixer_tail(_dot(g, wo_ref[...]), *rest)


def _conv_out(lay, z, bg, conv_k, w_out, h, mod, gain, wr, br):
    D = lay.D
    tile = lambda i: i
    prev, nxt = _halo_specs(lay, D)
    shapes, specs = _tail_out(lay, tile)
    return pl.pallas_call(
        functools.partial(_convout_kernel, lay),
        out_shape=shapes,
        grid=(lay.NT,),
        in_specs=[pl.BlockSpec((TM, D), lambda i: (i, 0)), prev, nxt,
                  pl.BlockSpec((TM, D), lambda i: (i, 0)),
                  pl.BlockSpec(conv_k.shape, lambda i: (0, 0)),
                  pl.BlockSpec((D, D), lambda i: (0, 0))] + _tail_in_specs(lay, tile),
        out_specs=specs,
        scratch_shapes=TAIL_SCRATCH,
        compiler_params=_cparams(("arbitrary",)),
        name="conv_out",
    )(z, z, z, bg, conv_k, w_out, h, mod, gain.reshape(1, D), mod, mod, wr, br)


def _qkv_kernel(head_dim, use_norm, q_scale, u_ref, w_ref, qg_ref, kg_ref, cos_ref, sin_ref, q_ref, k_ref, v_ref):
    y = _dot(u_ref[...], w_ref[...])
    nq, nk = q_ref.shape[1], k_ref.shape[1]
    lane = lax.broadcasted_iota(jnp.int32, (TM, LANES), 1)

    def rotary(yg, gain_ref):
        if use_norm:
            yg = _rms(yg, gain_ref[...])
        if head_dim == LANES:
            rot = pltpu.roll(yg, LANES // 2, 1)
        else:
            q = head_dim // 2
            rot = jnp.where(lane % head_dim < q, pltpu.roll(yg, LANES - q, 1), pltpu.roll(yg, q, 1))
        return yg * cos_ref[...] + rot * sin_ref[...]

    for g in range(nq // LANES):
        cols = slice(g * LANES, (g + 1) * LANES)
        q_ref[:, cols] = (rotary(y[:, cols], qg_ref) * q_scale).astype(q_ref.dtype)
    for g in range(nk // LANES):
        cols = slice(g * LANES, (g + 1) * LANES)
        k_ref[:, cols] = rotary(y[:, nq + g * LANES:nq + (g + 1) * LANES], kg_ref).astype(k_ref.dtype)
    v_ref[...] = y[:, nq + nk:].astype(v_ref.dtype)


def _qkv_project(lay, u, w, nq, nk, head_dim, q_gain, k_gain, cos_t, sin_t, use_norm, q_scale, name):
    D = lay.D
    row = lambda n: pl.BlockSpec((TM, n), lambda i: (i, 0))
    table = pl.BlockSpec((TM, LANES), lambda i: (i % lay.TPB, 0))
    gain = pl.BlockSpec((1, LANES), lambda i: (0, 0))
    bf = jnp.bfloat16
    return pl.pallas_call(
        functools.partial(_qkv_kernel, head_dim, use_norm, q_scale),
        out_shape=(jax.ShapeDtypeStruct((lay.T, nq), bf), jax.ShapeDtypeStruct((lay.T, nk), bf),
                   jax.ShapeDtypeStruct((lay.T, nk), bf)),
        grid=(lay.NT,),
        in_specs=[row(D), pl.BlockSpec((D, nq + 2 * nk), lambda i: (0, 0)), gain, gain, table, table],
        out_specs=(row(nq), row(nk), row(nk)),
        compiler_params=_cparams(("arbitrary",)),
        name=name,
    )(u, w, q_gain, k_gain, cos_t, sin_t)


def _rope_tables(lay, head_dim):
    quarter = head_dim // 4
    rows = lay.S // GRID_W
    row = jnp.repeat(jnp.arange(rows), GRID_W).astype(jnp.float32)
    col = jnp.tile(jnp.arange(GRID_W), rows).astype(jnp.float32)
    inv = ROPE_THETA ** (-jnp.arange(quarter, dtype=jnp.float32) / quarter)
    ang = jnp.concatenate([row[:, None] * inv, col[:, None] * inv], axis=-1)
    cos, sin = jnp.cos(ang), jnp.sin(ang)
    reps = LANES // head_dim
    cos_t = jnp.tile(jnp.concatenate([cos, cos], axis=-1), (1, reps))
    sin_t = jnp.tile(jnp.concatenate([-sin, sin], axis=-1), (1, reps))
    cos_t = jnp.concatenate([cos_t, jnp.ones((lay.C, LANES), jnp.float32)], axis=0)
    sin_t = jnp.concatenate([sin_t, jnp.zeros((lay.C, LANES), jnp.float32)], axis=0)
    return cos_t, sin_t


def _flash_kernel(n_rep, tk, q_ref, qn_ref, k_ref, v_ref, o_ref, vx_ref, s_ref, m_ref, l_ref, acc_ref):
    hd = LANES
    nk = k_ref.shape[0] // tk
    first = pl.program_id(2) == 0
    ring = nk >= 3 and nk % 2 == 1

    def rows(j):
        start = j * tk
        return pl.ds(start if isinstance(start, int) else pl.multiple_of(start, tk), tk)

    def scores(qsrc, j, slot):
        k = k_ref[rows(j), :]
        for h in range(n_rep):
            s_ref[slot, h] = lax.dot_general(qsrc[:, h * hd:(h + 1) * hd], k, (((1,), (1,)), ((), ())),
                                             preferred_element_type=jnp.float32)

    def step(j, rd, wr=None, qsrc=None, jn=None):
        vx = vx_ref[rows(j), :]
        if wr is not None:
            k_next = k_ref[rows(jn), :]
        for h in range(n_rep):
            if wr is not None:
                s_ref[wr, h] = lax.dot_general(qsrc[:, h * hd:(h + 1) * hd], k_next, (((1,), (1,)), ((), ())),
                                               preferred_element_type=jnp.float32)
            s = s_ref[rd, h]
            m_prev = m_ref[h]
            m_next = jnp.maximum(m_prev, jnp.max(s, axis=1, keepdims=True))
            alpha = jnp.exp2(m_prev - m_next)
            p = jnp.exp2(s - m_next[:, :1]).astype(jnp.bfloat16)
            pv = _dot(p, vx)
            m_ref[h] = m_next
            l_ref[h] = alpha * l_ref[h] + pv[:, hd:]
            acc_ref[h] = alpha * acc_ref[h] + pv[:, :hd]

    @pl.when(first)
    def _():
        vx_ref[:, :hd] = v_ref[...]
        vx_ref[:, hd:] = jnp.ones((vx_ref.shape[0], hd), vx_ref.dtype)

    m_ref[...] = jnp.full(m_ref.shape, NEG_INF, jnp.float32)
    l_ref[...] = jnp.zeros(l_ref.shape, jnp.float32)
    acc_ref[...] = jnp.zeros(acc_ref.shape, jnp.float32)

    if ring:
        @pl.when(first)
        def _():
            scores(q_ref, 0, 2)

        step(0, 2, 1, q_ref, 1)

        def body(t, _):
            step(2 * t + 1, 1, 0, q_ref, 2 * t + 2)
            step(2 * t + 2, 0, 1, q_ref, 2 * t + 3)
            return 0

        lax.fori_loop(0, (nk - 3) // 2, body, 0)
        step(nk - 2, 1, 0, q_ref, nk - 1)
        step(nk - 1, 0, 2, qn_ref, 0)
    else:
        scores(q_ref, 0, 0)
        for j in range(nk):
            step(j, j % 2, *((1 - j % 2, q_ref, j + 1) if j + 1 < nk else ()))
    for h in range(n_rep):
        o_ref[:, h * hd:(h + 1) * hd] = (acc_ref[h] / l_ref[h]).astype(o_ref.dtype)


def _flash(lay, q, k, v, n_kv, n_rep, tq, tk, q_blk0, nq, kb, k_blk0, name):
    B = lay.B
    qw = n_rep * LANES
    return pl.pallas_call(
        functools.partial(_flash_kernel, n_rep, tk),
        out_shape=jax.ShapeDtypeStruct((B, nq * tq, n_kv * qw), jnp.bfloat16),
        grid=(B, n_kv, nq),
        in_specs=[pl.BlockSpec((None, tq, qw), lambda b, g, i: (b, q_blk0 + i, g)),
                  pl.BlockSpec((None, tq, qw), lambda b, g, i: (b, q_blk0 + jnp.minimum(i + 1, nq - 1), g)),
                  pl.BlockSpec((None, kb, LANES), lambda b, g, i: (b, k_blk0, g)),
                  pl.BlockSpec((None, kb, LANES), lambda b, g, i: (b, k_blk0, g))],
        out_specs=pl.BlockSpec((None, tq, qw), lambda b, g, i: (b, i, g)),
        scratch_shapes=[pltpu.VMEM((kb, 2 * LANES), jnp.bfloat16),
                        pltpu.VMEM((3, n_rep, tq, tk), jnp.float32),
                        pltpu.VMEM((n_rep, tq, LANES), jnp.float32),
                        pltpu.VMEM((n_rep, tq, LANES), jnp.float32),
                        pltpu.VMEM((n_rep, tq, LANES), jnp.float32)],
        compiler_params=_cparams(("arbitrary", "arbitrary", "arbitrary")),
        name=name,
    )(q, q, k, v)


def _swa_kernel(lay, tq, sink_ref, q_ref, kp_ref, kc_ref, kn_ref, kx_ref, vp_ref, vc_ref, vn_ref, vx_ref, o_ref):
    pair = pl.program_id(1)
    qb = pl.program_id(2)
    hd = LANES // 2
    n_rep = SWA_HEADS // SWA_KV_HEADS
    W = SWA_WINDOW
    kk = jnp.concatenate([kp_ref[...], kc_ref[...], kn_ref[...], kx_ref[...]], axis=0)
    vv = jnp.concatenate([vp_ref[...], vc_ref[...], vn_ref[...], vx_ref[...]], axis=0)
    nkeys = kk.shape[0]
    nwin = tq + 2 * W
    lane = lax.broadcasted_iota(jnp.int32, (nkeys, LANES), 1)
    low = lane < hd
    kk_sw = pltpu.roll(kk.astype(jnp.float32), hd, 1).astype(kk.dtype)
    vv_sw = pltpu.roll(vv.astype(jnp.float32), hd, 1).astype(vv.dtype)
    k_dup = (jnp.where(low, kk, kk_sw), jnp.where(low, kk_sw, kk))
    v_dup = (jnp.where(low, vv, vv_sw), jnp.where(low, vv_sw, vv))

    start = qb * tq
    qpos = start + lax.broadcasted_iota(jnp.int32, (tq, nkeys), 0)
    col = lax.broadcasted_iota(jnp.int32, (tq, nkeys), 1)
    kpos = start - W + col
    in_win = jnp.logical_and(jnp.abs(qpos - kpos) <= W, jnp.logical_and(kpos >= 0, kpos < lay.S))
    valid = jnp.logical_or(col >= nwin, in_win)

    qlane = lax.broadcasted_iota(jnp.int32, (tq, LANES), 1)
    qlow = qlane < hd
    for g in range(2 * n_rep * hd // LANES):
        qg = q_ref[:, g * LANES:(g + 1) * LANES]
        kvh = (2 * g) // n_rep
        out = None
        for half in range(2):
            head = pair * 2 * n_rep + 2 * g + half
            qm = jnp.where(qlow if half == 0 else jnp.logical_not(qlow), qg, jnp.zeros_like(qg))
            s = lax.dot_general(qm, k_dup[kvh], (((1,), (1,)), ((), ())), preferred_element_type=jnp.float32)
            s = jnp.where(valid, s, NEG_INF)
            sink = sink_ref[head]
            m = jnp.maximum(jnp.max(s, axis=1, keepdims=True), sink)
            p = jnp.exp2(s - m)
            l = jnp.sum(p, axis=1, keepdims=True) + jnp.exp2(sink - m)
            o = _dot(p.astype(jnp.bfloat16), v_dup[kvh]) / l
            out = o if half == 0 else jnp.where(qlow, out, o)
        o_ref[:, g * LANES:(g + 1) * LANES] = out.astype(o_ref.dtype)


def _swa(lay, q, k, v, sink2):
    B, S = lay.B, lay.S
    tq = TM
    W = SWA_WINDOW
    per = tq // W
    nq = S // tq
    qw = 2 * (SWA_HEADS // SWA_KV_HEADS) * (LANES // 2)
    ctx_blk = S // lay.C
    kv_specs = [pl.BlockSpec((None, W, LANES), lambda b, p, i: (b, jnp.maximum(i * per - 1, 0), p)),
                pl.BlockSpec((None, tq, LANES), lambda b, p, i: (b, i, p)),
                pl.BlockSpec((None, W, LANES), lambda b, p, i: (b, (i + 1) * per, p)),
                pl.BlockSpec((None, lay.C, LANES), lambda b, p, i: (b, ctx_blk, p))]
    return pl.pallas_call(
        functools.partial(_swa_kernel, lay, tq),
        out_shape=jax.ShapeDtypeStruct((B, S, lay.D), jnp.bfloat16),
        grid=(B, SWA_KV_HEADS // 2, nq),
        in_specs=[pl.BlockSpec(memory_space=pltpu.SMEM),
                  pl.BlockSpec((None, tq, qw), lambda b, p, i: (b, i, p))] + kv_specs + kv_specs,
        out_specs=pl.BlockSpec((None, tq, qw), lambda b, p, i: (b, i, p)),
        compiler_params=_cparams(("arbitrary", "arbitrary", "arbitrary")),
        name="swa",
    )(sink2, q, k, k, k, k, v, v, v, v)


def _outproj_kernel(lay, has_ctx, ol_ref, oc_ref, wo_ref, *rest):
    x = ol_ref[...]
    if has_ctx:
        _, w = lay.split(pl.program_id(0))
        x = jnp.where(w >= lay.LT, oc_ref[...], x)
    _mixer_tail(_dot(x, wo_ref[...]), *rest)


def _out_proj(lay, o_lat, o_ctx, w_out, h, mod, gain, wr, br):
    D = lay.D
    tile = lambda i: i
    has_ctx = o_ctx is not None
    if not has_ctx:
        o_ctx = o_lat
    shapes, specs = _tail_out(lay, tile)
    lat_spec, ctx_spec = _lat_ctx_specs(lay)
    return pl.pallas_call(
        functools.partial(_outproj_kernel, lay, has_ctx),
        out_shape=shapes,
        grid=(lay.NT,),
        in_specs=[lat_spec, ctx_spec if has_ctx else lat_spec,
                  pl.BlockSpec((D, D), lambda i: (0, 0))] + _tail_in_specs(lay, tile),
        out_specs=specs,
        scratch_shapes=TAIL_SCRATCH,
        compiler_params=_cparams(("arbitrary",)),
        name="attn_out",
    )(o_lat, o_ctx, w_out, h, mod, gain.reshape(1, D), mod, mod, wr, br)


def _pool_kernel(lay, u_ref, up_ref, un_ref, pw_ref, ps_ref, *rest):
    i = pl.program_id(0)
    _, w = lay.split(i)
    has_prev, has_next = _seq_flags(lay, i)
    in_ctx = w >= lay.LT
    seq_len = jnp.where(in_ctx, lay.C, lay.S)
    pos0 = jnp.where(in_ctx, w - lay.LT, w) * TM
    G = len(POOL_WINDOWS)
    gw = lay.D // G
    E_ROWS = TM + 2 * SUBLANES_F32
    pos = pos0 + lax.broadcasted_iota(jnp.int32, (TM, gw), 0)
    ys = []
    for g, win in enumerate(POOL_WINDOWS):
        sl = slice(g * gw, (g + 1) * gw)
        u = u_ref[:, sl].astype(jnp.float32)
        before = jnp.where(has_prev, up_ref[HALO - SUBLANES_F32:HALO, sl].astype(jnp.float32), 0.0)
        after = jnp.where(has_next, un_ref[0:SUBLANES_F32, sl].astype(jnp.float32), 0.0)
        e = jnp.concatenate([before, u, after], axis=0)
        left = win // 2
        right = win - 1 - left
        assert left == right + 1 and left & (left - 1) == 0
        acc = e
        span = 1
        while span < left:
            acc = acc + pltpu.roll(acc, E_ROWS - span, 0)
            span *= 2
        tot = pltpu.roll(acc, left, 0) + acc
        total = tot[SUBLANES_F32:SUBLANES_F32 + TM]
        cnt = jnp.minimum(pos + right, seq_len - 1) - jnp.maximum(pos - left, 0) + 1
        mean = total / cnt.astype(jnp.float32)
        ys.append(_dot((mean - u).astype(jnp.bfloat16), pw_ref[g]))
    y = jnp.concatenate(ys, axis=1) * ps_ref[...]
    _mixer_tail(y, *rest)


def _pool(lay, u, pool_w, pool_scale, h, mod, gain, wr, br):
    D = lay.D
    tile = lambda i: i
    prev, nxt = _halo_specs(lay, D)
    shapes, specs = _tail_out(lay, tile)
    return pl.pallas_call(
        functools.partial(_pool_kernel, lay),
        out_shape=shapes,
        grid=(lay.NT,),
        in_specs=[pl.BlockSpec((TM, D), lambda i: (i, 0)), prev, nxt,
                  pl.BlockSpec(pool_w.shape, lambda i: (0, 0, 0)),
                  _row_spec(D)] + _tail_in_specs(lay, tile),
        out_specs=specs,
        scratch_shapes=TAIL_SCRATCH,
        compiler_params=_cparams(("arbitrary",)),
        name="pool",
    )(u, u, u, pool_w, pool_scale.reshape(1, D), h, mod, gain.reshape(1, D), mod, mod, wr, br)


def _route_tile(lg, info_ref, gate_ref, cnt_ref, carry_ref):
    @pl.when(pl.program_id(0) == 0)
    def _():
        carry_ref[...] = jnp.zeros(carry_ref.shape, jnp.float32)

    lane = lax.broadcasted_iota(jnp.int32, lg.shape, 1)
    lane_f = lane.astype(jnp.float32)
    big = jnp.float32(4 * LANES)

    def first_lane(mask):
        return jnp.min(jnp.where(mask, lane_f, big), axis=1, keepdims=True).astype(jnp.int32)

    is_grp = lane < N_GROUPS
    gl = jnp.where(is_grp, lg, NEG_INF)
    gmax = jnp.max(gl, axis=1, keepdims=True)
    grp = first_lane(jnp.logical_and(is_grp, gl == gmax))
    p_grp = 1.0 / jnp.sum(jnp.where(is_grp, jnp.exp(gl - gmax), 0.0), axis=1, keepdims=True)
    eid = lane - N_GROUPS
    in_grp = jnp.logical_and(lane >= N_GROUPS + grp * EXPERTS_PER_GROUP,
                             lane < N_GROUPS + (grp + 1) * EXPERTS_PER_GROUP)
    el = jnp.where(in_grp, lg, NEG_INF)
    t1 = jnp.max(el, axis=1, keepdims=True)
    e1 = first_lane(jnp.logical_and(in_grp, el == t1)) - N_GROUPS
    rest = jnp.logical_and(in_grp, eid != e1)
    el2 = jnp.where(rest, lg, NEG_INF)
    t2 = jnp.max(el2, axis=1, keepdims=True)
    e2 = first_lane(jnp.logical_and(rest, el2 == t2)) - N_GROUPS
    d = jnp.exp(t2 - t1)
    g1 = p_grp / (1.0 + d)
    g2 = p_grp * d / (1.0 + d)

    oh1 = lane == e1
    oh2 = lane == e2
    oh = jnp.where(jnp.logical_or(oh1, oh2), 1.0, 0.0)
    r = lax.broadcasted_iota(jnp.int32, (TM, TM), 0)
    c = lax.broadcasted_iota(jnp.int32, (TM, TM), 1)
    tri = jnp.where(c < r, 1.0, 0.0).astype(jnp.bfloat16)
    before = _dot(tri, oh.astype(jnp.bfloat16)) + carry_ref[0:1, :]
    r1 = jnp.sum(jnp.where(oh1, before, 0.0), axis=1, keepdims=True)
    r2 = jnp.sum(jnp.where(oh2, before, 0.0), axis=1, keepdims=True)
    carry = carry_ref[0:1, :] + jnp.sum(oh, axis=0, keepdims=True)
    carry_ref[...] = jnp.broadcast_to(carry, carry_ref.shape)

    info = jnp.where(lane == 0, e1, jnp.where(lane == 1, e2, jnp.where(
        lane == 2, r1.astype(jnp.int32), jnp.where(lane == 3, r2.astype(jnp.int32), 0))))
    info_ref[...] = info
    gate_ref[...] = jnp.where(lane == 0, g1, jnp.where(lane == 1, g2, 0.0))
    cnt_ref[...] = jnp.broadcast_to(carry, cnt_ref.shape).astype(jnp.int32)


def _dispatch_kernel(n_blocks, plan_ref, dest_ref, v_ref, xs_hbm, zbuf, sem, zsem):
    @pl.when(pl.program_id(0) == 0)
    def _():
        zbuf[...] = jnp.zeros(zbuf.shape, zbuf.dtype)

        def zero_block(row0):
            rows = pl.ds(pl.multiple_of(row0, EXPERT_ROWS), EXPERT_ROWS)
            return pltpu.make_async_copy(zbuf, xs_hbm.at[rows], zsem)

        def for_each_zero_block(fn):
            def seg(e, _):
                @pl.when(plan_ref[N_EXPERTS + e] > 0)
                def _():
                    fn(zero_block(plan_ref[e] - EXPERT_ROWS))
                return 0

            def tail(b, _):
                fn(zero_block(b * EXPERT_ROWS))
                return 0

            lax.fori_loop(0, N_EXPERTS, seg, 0)
            lax.fori_loop(plan_ref[2 * N_EXPERTS], n_blocks, tail, 0)

        for_each_zero_block(lambda cp: cp.start())
        for_each_zero_block(lambda cp: cp.wait())

    def issue(r, _):
        for k in range(TOP_K):
            pltpu.make_async_copy(v_ref.at[pl.ds(r, 1)], xs_hbm.at[pl.ds(dest_ref[0, k, r], 1)],
                                  sem).start(priority=k % DMA_PRIORITIES)
        return 0

    lax.fori_loop(0, TM, issue, 0, unroll=True)
    for k in range(TOP_K):
        pltpu.make_async_copy(v_ref, xs_hbm.at[pl.ds(0, TM)], sem).wait()


def _dispatch(lay, v, dest, plan, n_blocks):
    W = v.shape[1]
    return pl.pallas_call(
        functools.partial(_dispatch_kernel, n_blocks),
        out_shape=jax.ShapeDtypeStruct((n_blocks * EXPERT_ROWS, W), v.dtype),
        grid=(lay.NT,),
        in_specs=[pl.BlockSpec(memory_space=pltpu.SMEM),
                  pl.BlockSpec((1, TOP_K, TM), lambda i: (i, 0, 0), memory_space=pltpu.SMEM),
                  pl.BlockSpec((TM, W), lambda i: (i, 0))],
        out_specs=pl.BlockSpec(memory_space=pl.ANY),
        scratch_shapes=[pltpu.VMEM((EXPERT_ROWS, W), v.dtype),
                        pltpu.SemaphoreType.DMA(()), pltpu.SemaphoreType.DMA(())],
        compiler_params=pltpu.CompilerParams(dimension_semantics=("arbitrary",), has_side_effects=True),
        name="moe_dispatch",
    )(plan, dest, v)


def _expert_kernel(layer, be_ref, na_ref, nxt_ref, slot_ref, x_ref, wg_hbm, wu_hbm, wd_hbm, y_ref,
                   wg_buf, wu_buf, wd_buf, wgu_s, wd_s, sem):
    i = pl.program_id(0)
    de = wg_buf.shape[2]
    active = i < na_ref[0]
    e = be_ref[i]

    def fetch(expert, slot):
        return (pltpu.make_async_copy(wg_hbm.at[layer, expert], wg_buf.at[slot], sem.at[slot, 0]),
                pltpu.make_async_copy(wu_hbm.at[layer, expert], wu_buf.at[slot], sem.at[slot, 1]),
                pltpu.make_async_copy(wd_hbm.at[layer, expert], wd_buf.at[slot], sem.at[slot, 2]))

    @pl.when(i == 0)
    def _():
        for cp in fetch(e, slot_ref[e]):
            cp.start()

    @pl.when(jnp.logical_and(active, jnp.logical_or(i == 0, e != be_ref[jnp.maximum(i - 1, 0)])))
    def _():
        slot = slot_ref[e]
        for cp in fetch(e, slot):
            cp.wait()

        @pl.when(nxt_ref[e] >= 0)
        def _():
            for cp in fetch(nxt_ref[e], 1 - slot):
                cp.start()

        wgu_s[:, :de] = wg_buf[slot].astype(wgu_s.dtype)
        wgu_s[:, de:] = wu_buf[slot].astype(wgu_s.dtype)
        wd_s[...] = wd_buf[slot].astype(wd_s.dtype)

    @pl.when(active)
    def _():
        hi, lo = _unpack_rows(x_ref[...])
        x = jnp.concatenate([hi.astype(jnp.bfloat16), lo.astype(jnp.bfloat16)], axis=1)
        hgu = _dot(x, wgu_s[...])
        hg, hu = hgu[:, :de], hgu[:, de:]
        a = (hg * (1.0 / (1.0 + jnp.exp(-hg)))) * hu
        y_ref[...] = _pack_rows(_dot(a.astype(jnp.bfloat16), wd_s[...]))

    @pl.when(jnp.logical_not(active))
    def _():
        y_ref[...] = jnp.zeros(y_ref.shape, y_ref.dtype)


def _experts(lay, xs, blk_expert, n_active, nxt, slot, layer, wg, wu, wd):
    D = lay.D
    P, W = xs.shape
    DE = wg.shape[3]
    row_block = pl.BlockSpec((EXPERT_ROWS, W), lambda i, *_: (i, 0))
    hbm = pl.BlockSpec(memory_space=pl.ANY)
    grid_spec = pltpu.PrefetchScalarGridSpec(
        num_scalar_prefetch=4,
        grid=(P // EXPERT_ROWS,),
        in_specs=[row_block, hbm, hbm, hbm],
        out_specs=row_block,
        scratch_shapes=[pltpu.VMEM((2, D, DE), jnp.float32), pltpu.VMEM((2, D, DE), jnp.float32),
                        pltpu.VMEM((2, DE, D), jnp.float32),
                        pltpu.VMEM((D, 2 * DE), jnp.bfloat16), pltpu.VMEM((DE, D), jnp.bfloat16),
                        pltpu.SemaphoreType.DMA((2, 3))],
    )
    return pl.pallas_call(
        functools.partial(_expert_kernel, layer),
        out_shape=jax.ShapeDtypeStruct((P, W), xs.dtype),
        grid_spec=grid_spec,
        compiler_params=_cparams(("arbitrary",)),
        name="moe_experts",
    )(blk_expert, n_active, nxt, slot, xs, wg, wu, wd)


def _combine_kernel(final, *refs):
    dests, refs = refs[:COMBINE_AHEAD + 1], refs[COMBINE_AHEAD + 1:]
    ys_hbm, gate_ref, h_ref, g2_ref, gain_ref, sc_ref, sh_ref, *out_refs, ybuf, sem = refs
    i = pl.program_id(0)
    n_slots = COMBINE_AHEAD + 1

    def gather(dref, buf_slot, r0, r1):
        for r in range(r0, r1):
            for k in range(TOP_K):
                pltpu.make_async_copy(ys_hbm.at[pl.ds(dref[0, k, r], 1)], ybuf.at[buf_slot, k, pl.ds(r, 1)],
                                      sem.at[buf_slot]).start(priority=k % DMA_PRIORITIES)

    def wait_tile(buf_slot):
        for k in range(TOP_K):
            pltpu.make_async_copy(ys_hbm.at[pl.ds(0, TM)], ybuf.at[buf_slot, k], sem.at[buf_slot]).wait()

    @pl.when(i == 0)
    def _():
        for a in range(COMBINE_AHEAD):
            gather(dests[a], a, 0, TM)

    def step(slot):
        wait_tile(slot)
        for r0 in range(0, TM, COMBINE_ROWS):
            rows = pl.ds(r0, COMBINE_ROWS)
            gate = gate_ref[rows, :]
            hi0, lo0 = _unpack_rows(ybuf[slot, 0, rows, :])
            hi1, lo1 = _unpack_rows(ybuf[slot, 1, rows, :])
            g0, g1 = gate[:, 0:1], gate[:, 1:2]
            f = jnp.concatenate([g0 * hi0 + g1 * hi1, g0 * lo0 + g1 * lo1], axis=1)
            h2 = h_ref[rows, :] + g2_ref[...] * f
            if final:
                (out_ref,) = out_refs
                out_ref[rows, :] = _rms(h2, gain_ref[...])
            else:
                h2_ref, u_ref = out_refs
                h2_ref[rows, :] = h2
                u_ref[rows, :] = (_rms(h2, gain_ref[...]) * (1.0 + sc_ref[...]) + sh_ref[...]).astype(u_ref.dtype)
            gather(dests[COMBINE_AHEAD], (slot + COMBINE_AHEAD) % n_slots, r0, r0 + COMBINE_ROWS)

        @pl.when(i == pl.num_programs(0) - 1)
        def _():
            for a in range(1, n_slots):
                wait_tile((slot + a) % n_slots)

    for slot in range(n_slots):
        pl.when(i % n_slots == slot)(functools.partial(step, slot))


def _combine(lay, ys, dest, gates, h1, mod, mod_next, gain_next, final):
    D = lay.D
    row_tile = pl.BlockSpec((TM, D), lambda i: (i, 0))
    if final:
        n = lay.B * lay.LT
        tile = lambda i: (i // lay.LT) * lay.TPB + i % lay.LT
        out_shape = (jax.ShapeDtypeStruct((lay.B * lay.S, D), jnp.float32),)
        out_specs = (row_tile,)
    else:
        n = lay.NT
        tile = lambda i: i
        out_shape = (jax.ShapeDtypeStruct((lay.T, D), jnp.float32),
                     jax.ShapeDtypeStruct((lay.T, D), jnp.bfloat16))
        out_specs = (row_tile, row_tile)
    return pl.pallas_call(
        functools.partial(_combine_kernel, final),
        out_shape=out_shape,
        grid=(n,),
        in_specs=[pl.BlockSpec((1, TOP_K, TM), lambda i, a=a: (tile(jnp.minimum(i + a, n - 1)), 0, 0),
                               memory_space=pltpu.SMEM) for a in range(COMBINE_AHEAD + 1)] + [
                  pl.BlockSpec(memory_space=pl.ANY),
                  pl.BlockSpec((TM, LANES), lambda i: (tile(i), 0)),
                  pl.BlockSpec((TM, D), lambda i: (tile(i), 0)),
                  _mod_spec(lay, 5, tile),
                  _row_spec(D),
                  _mod_spec(lay, 1, tile), _mod_spec(lay, 0, tile)],
        out_specs=out_specs,
        scratch_shapes=[pltpu.VMEM((COMBINE_AHEAD + 1, TOP_K, TM, ys.shape[1]), ys.dtype),
                        pltpu.SemaphoreType.DMA((COMBINE_AHEAD + 1,))],
        compiler_params=_cparams(("arbitrary",)),
        name="moe_combine_final" if final else "moe_combine",
    )(*([dest] * (COMBINE_AHEAD + 1)), ys, gates, h1, mod, gain_next.reshape(1, D), mod_next, mod_next)


def _moe(lay, v, info, gates, cnt, h1, mod, mod_next, gain_next, layer, wg, wu, wd, final):
    T, NT = lay.T, lay.NT
    counts = cnt[0, :N_EXPERTS]
    padded = ((counts + EXPERT_ROWS - 1) // EXPERT_ROWS) * EXPERT_ROWS
    pad_end = jnp.cumsum(padded)
    pad_start = pad_end - padded
    expert = info[:, :TOP_K]
    onehot = expert[:, :, None] == jnp.arange(N_EXPERTS, dtype=jnp.int32)
    dest = info[:, TOP_K:2 * TOP_K] + jnp.sum(jnp.where(onehot, pad_start, 0), axis=-1)
    dest = dest.reshape(NT, TM, TOP_K).transpose(0, 2, 1).astype(jnp.int32)
    n_blocks = -(-(T * TOP_K + N_EXPERTS * (EXPERT_ROWS - 1)) // EXPERT_ROWS)
    blk_start = jnp.arange(n_blocks, dtype=jnp.int32) * EXPERT_ROWS
    blk_expert = jnp.minimum(jnp.sum(pad_end[None, :] <= blk_start[:, None], axis=1),
                             N_EXPERTS - 1).astype(jnp.int32)
    n_active = (pad_end[-1:] // EXPERT_ROWS).astype(jnp.int32)
    plan = jnp.concatenate([pad_end, padded, n_active]).astype(jnp.int32)
    xs = _dispatch(lay, v, dest, plan, n_blocks)
    owns = padded > 0
    ids = jnp.arange(N_EXPERTS, dtype=jnp.int32)
    later = jnp.where(owns[None, :] & (ids[None, :] > ids[:, None]), ids[None, :], N_EXPERTS)
    nxt = jnp.min(later, axis=1)
    nxt = jnp.where(nxt == N_EXPERTS, -1, nxt).astype(jnp.int32)
    slot = ((jnp.cumsum(owns.astype(jnp.int32)) - 1) % 2).astype(jnp.int32)
    ys = _experts(lay, xs, blk_expert, n_active, nxt, slot, layer, wg, wu, wd)
    return _combine(lay, ys, dest, gates, h1, mod, mod_next, gain_next, final)


def kernel(x, c, ctx, c_ctx, ada_w, ada_b, norm_mix, norm_ffn, norm_final, conv_in, conv_k, conv_out, gqa_qkv, gqa_q_gain, gqa_k_gain, gqa_out, pool_w, pool_scale, swa_qkv, swa_sink, swa_out, router_grp_w, router_grp_b, router_exp_w, router_exp_b, exp_gate, exp_up, exp_down):
    B, S, D = x.shape
    C = ctx.shape[1]
    L = ada_w.shape[0]
    lay = Layout(B, S, C, D)
    bf = jnp.bfloat16

    R = SUBLANES_BF16
    cvec = jnp.zeros((R, D), jnp.float32).at[:B].set(c).at[B].set(c_ctx)
    mod_all = _modulation(cvec, ada_w, ada_b)[:, :B + 1].reshape(L, B + 1, 6, 1, D).transpose(0, 2, 1, 3, 4)

    pad = LANES - N_GROUPS - N_EXPERTS
    wr_all = jnp.concatenate([router_grp_w, router_exp_w, jnp.zeros((L, D, pad), jnp.float32)], axis=-1).astype(bf)
    br_all = jnp.concatenate([router_grp_b, router_exp_b, jnp.zeros((L, pad), jnp.float32)], axis=-1)

    h, u = _prenorm(lay, x, ctx, norm_mix[0], mod_all[0])
    one = jnp.ones((1, LANES), jnp.float32)

    for i in range(L):
        m, j = i % 4, i // 4
        mod = mod_all[i]
        wr, br = wr_all[i], br_all[i].reshape(1, LANES)
        tail = (h, mod, norm_ffn[i], wr, br)
        if m == 0:
            bg, z = _conv_in(lay, u, conv_in[j].astype(bf))
            routed = _conv_out(lay, z, bg, conv_k[j], conv_out[j].astype(bf), *tail)
        elif m == 1:
            hd = D // GQA_HEADS
            w = gqa_qkv[j].astype(bf)
            cos_t, sin_t = _rope_tables(lay, hd)
            nq, nk = GQA_HEADS * hd, GQA_KV_HEADS * hd
            qg = gqa_q_gain[j].reshape(1, hd)
            kg = gqa_k_gain[j].reshape(1, hd)
            q, k, vv = _qkv_project(lay, u, w, nq, nk, hd, qg, kg, cos_t, sin_t, True, hd ** -0.5 * LOG2E, "gqa_qkv")
            q3, k3, v3 = (a.reshape(B, lay.SB, -1) for a in (q, k, vv))
            n_rep = GQA_HEADS // GQA_KV_HEADS
            tq = 512 if S % 512 == 0 else TM
            tk = 768 if lay.SB % 768 == 0 else TM
            o_lat = _flash(lay, q3, k3, v3, GQA_KV_HEADS, n_rep, tq, tk, 0, S // tq, lay.SB, 0, "gqa_flash")
            o_ctx = _flash(lay, q3, k3, v3, GQA_KV_HEADS, n_rep, C, C, S // C, 1, C, S // C, "gqa_flash_ctx")
            routed = _out_proj(lay, o_lat, o_ctx, gqa_out[j].astype(bf), *tail)
        elif m == 2:
            routed = _pool(lay, u, pool_w[j].astype(bf), pool_scale[j], *tail)
        else:
            hd = D // SWA_HEADS
            w = swa_qkv[j].astype(bf)
            cos_t, sin_t = _rope_tables(lay, hd)
            nq, nk = SWA_HEADS * hd, SWA_KV_HEADS * hd
            q, k, vv = _qkv_project(lay, u, w, nq, nk, hd, one, one, cos_t, sin_t, False, hd ** -0.5 * LOG2E, "swa_qkv")
            q3, k3, v3 = (a.reshape(B, lay.SB, -1) for a in (q, k, vv))
            o_lat = _swa(lay, q3, k3, v3, swa_sink[j] * LOG2E)
            routed = _out_proj(lay, o_lat, None, swa_out[j].astype(bf), *tail)
        final = i == L - 1
        mod_next = mod if final else mod_all[i + 1]
        gain_next = norm_final if final else norm_mix[i + 1]
        h1, v, info, gates, cnt = routed
        res = _moe(lay, v, info, gates, cnt, h1, mod, mod_next, gain_next, i, exp_gate, exp_up, exp_down, final)
        if final:
            return res[0].reshape(B, S, D)
        h, u = res
```

```python
import functools

import jax
import jax.numpy as jnp
from jax import lax
from jax.experimental import pallas as pl
from jax.experimental.pallas import tpu as pltpu

GRID_W = 64
NORM_EPS = 1e-6
ROPE_THETA = 10000.0
NEG_INF = -1e30
GQA_HEADS, GQA_KV_HEADS = 16, 4
SWA_HEADS, SWA_KV_HEADS, SWA_WINDOW = 32, 8, 128
POOL_WINDOWS = (2, 4, 8, 16)
N_GROUPS, EXPERTS_PER_GROUP, TOP_K = 8, 4, 2
N_EXPERTS = N_GROUPS * EXPERTS_PER_GROUP
LOG2E = 1.4426950408889634

LANES = 128
SUBLANES_F32 = 8
SUBLANES_BF16 = 16
VMEM_LIMIT = 56 * 1024 * 1024
TM = 256
EXPERT_ROWS = 256
COMBINE_ROWS = 32
COMBINE_AHEAD = 2
DMA_PRIORITIES = 2
HALO = SUBLANES_BF16


def _cparams(sem):
    return pltpu.CompilerParams(dimension_semantics=sem, vmem_limit_bytes=VMEM_LIMIT)


class Layout:
    def __init__(self, B, S, C, D):
        assert S % TM == 0 and C % TM == 0
        self.B, self.S, self.C, self.D = B, S, C, D
        self.SB = S + C
        self.T = B * self.SB
        self.LT = S // TM
        self.TPB = self.SB // TM
        self.NT = B * self.TPB

    def split(self, i):
        return i // self.TPB, i % self.TPB

    def mod_row(self, i):
        b, w = self.split(i)
        return jnp.where(w >= self.LT, self.B, b)


def _mod_spec(lay, chunk, tile_of):
    return pl.BlockSpec((None, None, 1, lay.D),
                        lambda *g: (chunk, lay.mod_row(tile_of(*g)), 0, 0))


def _row_spec(D):
    return pl.BlockSpec((1, D), lambda *g: (0, 0))


def _rms(x, gain):
    ms = jnp.mean(x * x, axis=-1, keepdims=True)
    return (x * lax.rsqrt(ms + NORM_EPS)) * gain


def _dot(a, b):
    return jnp.dot(a, b, preferred_element_type=jnp.float32)


def _mod_kernel(c_ref, w_ref, b_ref, o_ref):
    c = c_ref[...]
    a = (c * (1.0 / (1.0 + jnp.exp(-c)))).astype(jnp.bfloat16)
    o_ref[...] = _dot(a, w_ref[...].astype(jnp.bfloat16)) + b_ref[...]


def _modulation(cvec, ada_w, ada_b):
    L, D, N = ada_w.shape
    R = cvec.shape[0]
    tn = 512
    return pl.pallas_call(
        _mod_kernel,
        out_shape=jax.ShapeDtypeStruct((L, R, N), jnp.float32),
        grid=(L, N // tn),
        in_specs=[pl.BlockSpec((R, D), lambda l, j: (0, 0)),
                  pl.BlockSpec((None, D, tn), lambda l, j: (l, 0, j)),
                  pl.BlockSpec((None, 1, tn), lambda l, j: (l, 0, j))],
        out_specs=pl.BlockSpec((None, R, tn), lambda l, j: (l, 0, j)),
        compiler_params=_cparams(("arbitrary", "arbitrary")),
        name="adaln_mod",
    )(cvec, ada_w, ada_b.reshape(L, 1, N))


def _prenorm_kernel(lay, x_ref, ctx_ref, gain_ref, sc_ref, sh_ref, h_ref, u_ref):
    _, w = lay.split(pl.program_id(0))
    h = jnp.where(w >= lay.LT, ctx_ref[...], x_ref[...])
    h_ref[...] = h
    u_ref[...] = (_rms(h, gain_ref[...]) * (1.0 + sc_ref[...]) + sh_ref[...]).astype(u_ref.dtype)


def _lat_ctx_specs(lay):
    def lat_idx(i):
        b, w = lay.split(i)
        return b, jnp.minimum(w, lay.LT - 1), 0

    def ctx_idx(i):
        b, w = lay.split(i)
        return b, jnp.clip(w - lay.LT, 0, lay.TPB - lay.LT - 1), 0

    return pl.BlockSpec((None, TM, lay.D), lat_idx), pl.BlockSpec((None, TM, lay.D), ctx_idx)


def _prenorm(lay, x, ctx, gain, mod):
    D = lay.D
    tile = lambda i: i
    row = pl.BlockSpec((TM, D), lambda i: (i, 0))
    return pl.pallas_call(
        functools.partial(_prenorm_kernel, lay),
        out_shape=(jax.ShapeDtypeStruct((lay.T, D), jnp.float32),
                   jax.ShapeDtypeStruct((lay.T, D), jnp.bfloat16)),
        grid=(lay.NT,),
        in_specs=[*_lat_ctx_specs(lay), _row_spec(D), _mod_spec(lay, 1, tile), _mod_spec(lay, 0, tile)],
        out_specs=(row, row),
        compiler_params=_cparams(("arbitrary",)),
        name="prenorm",
    )(x, ctx, gain.reshape(1, D), mod, mod)


def _pack_rows(x):
    w = x.shape[1] // 2
    hi = pltpu.bitcast(x[:, :w].astype(jnp.bfloat16).astype(jnp.float32), jnp.uint32)
    lo = pltpu.bitcast(x[:, w:].astype(jnp.bfloat16).astype(jnp.float32), jnp.uint32)
    return hi | (lo >> 16)


def _unpack_rows(words):
    hi = pltpu.bitcast(words & jnp.uint32(0xFFFF0000), jnp.float32)
    lo = pltpu.bitcast(words << 16, jnp.float32)
    return hi, lo


def _mixer_tail(y, h_ref, g1_ref, gain_ref, sc_ref, sh_ref, wr_ref, br_ref,
                h1_ref, v_ref, info_ref, gate_ref, cnt_ref, carry_ref):
    h1 = h_ref[...] + g1_ref[...] * y
    h1_ref[...] = h1
    v = _rms(h1, gain_ref[...]) * (1.0 + sc_ref[...]) + sh_ref[...]
    v_ref[...] = _pack_rows(v)
    logits = _dot(v.astype(jnp.bfloat16), wr_ref[...]) + br_ref[...]
    _route_tile(logits, info_ref, gate_ref, cnt_ref, carry_ref)


def _tail_in_specs(lay, tile):
    D = lay.D
    return [pl.BlockSpec((TM, D), lambda *g: (tile(*g), 0)),
            _mod_spec(lay, 2, tile),
            _row_spec(D),
            _mod_spec(lay, 4, tile), _mod_spec(lay, 3, tile),
            pl.BlockSpec((D, LANES), lambda *g: (0, 0)),
            pl.BlockSpec((1, LANES), lambda *g: (0, 0))]


def _tail_out(lay, tile):
    D = lay.D
    shapes = (jax.ShapeDtypeStruct((lay.T, D), jnp.float32),
              jax.ShapeDtypeStruct((lay.T, D // 2), jnp.uint32),
              jax.ShapeDtypeStruct((lay.T, LANES), jnp.int32),
              jax.ShapeDtypeStruct((lay.T, LANES), jnp.float32),
              jax.ShapeDtypeStruct((SUBLANES_F32, LANES), jnp.int32))
    specs = (pl.BlockSpec((TM, D), lambda *g: (tile(*g), 0)),
             pl.BlockSpec((TM, D // 2), lambda *g: (tile(*g), 0)),
             pl.BlockSpec((TM, LANES), lambda *g: (tile(*g), 0)),
             pl.BlockSpec((TM, LANES), lambda *g: (tile(*g), 0)),
             pl.BlockSpec((SUBLANES_F32, LANES), lambda *g: (0, 0)))
    return shapes, specs


TAIL_SCRATCH = [pltpu.VMEM((SUBLANES_F32, LANES), jnp.float32)]


def _seq_flags(lay, i):
    _, w = lay.split(i)
    has_prev = jnp.logical_and(w != 0, w != lay.LT)
    has_next = jnp.logical_and(w != lay.LT - 1, w != lay.TPB - 1)
    return has_prev, has_next


def _halo_specs(lay, width, col=lambda *g: 0, tile=lambda i: i):
    per = TM // HALO
    last = lay.T // HALO - 1
    prev = pl.BlockSpec((HALO, width), lambda *g: (jnp.maximum(tile(*g) * per - 1, 0), col(*g)))
    nxt = pl.BlockSpec((HALO, width), lambda *g: (jnp.minimum((tile(*g) + 1) * per, last), col(*g)))
    return prev, nxt


def _convin_kernel(u_ref, wb_ref, wc_ref, wx_ref, bg_ref, z_ref):
    x = u_ref[...]
    bg_ref[...] = _dot(x, wb_ref[...]).astype(bg_ref.dtype)
    z_ref[...] = (_dot(x, wc_ref[...]) * _dot(x, wx_ref[...])).astype(z_ref.dtype)


def _conv_in(lay, u, w_in):
    D = lay.D
    tn = 1024
    nb = D // tn
    out = jax.ShapeDtypeStruct((lay.T, D), jnp.bfloat16)
    return pl.pallas_call(
        _convin_kernel,
        out_shape=(out, out),
        grid=(nb, lay.NT),
        in_specs=[pl.BlockSpec((TM, D), lambda j, i: (i, 0)),
                  pl.BlockSpec((D, tn), lambda j, i: (0, j)),
                  pl.BlockSpec((D, tn), lambda j, i: (0, nb + j)),
                  pl.BlockSpec((D, tn), lambda j, i: (0, 2 * nb + j))],
        out_specs=(pl.BlockSpec((TM, tn), lambda j, i: (i, j)),
                   pl.BlockSpec((TM, tn), lambda j, i: (i, j))),
        compiler_params=_cparams(("arbitrary", "arbitrary")),
        name="conv_in",
    )(u, w_in, w_in, w_in)


def _convout_kernel(lay, z_ref, zp_ref, zn_ref, bg_ref, ck_ref, wo_ref, *rest):
    i = pl.program_id(0)
    has_prev, has_next = _seq_flags(lay, i)
    z = z_ref[...].astype(jnp.float32)
    row = lax.broadcasted_iota(jnp.int32, z.shape, 0)
    prev_row = jnp.where(has_prev, zp_ref[HALO - 1:HALO, :].astype(jnp.float32), 0.0)
    next_row = jnp.where(has_next, zn_ref[0:1, :].astype(jnp.float32), 0.0)
    z_m1 = jnp.where(row == 0, prev_row, pltpu.roll(z, 1, 0))
    z_p1 = jnp.where(row == TM - 1, next_row, pltpu.roll(z, TM - 1, 0))
    y = z_m1 * ck_ref[0:1, :] + z * ck_ref[1:2, :] + z_p1 * ck_ref[2:3, :]
    g = (bg_ref[...].astype(jnp.float32) * y).astype(jnp.bfloat16)
    _mixer_tail(_dot(g, wo_ref[...]), *rest)


def _conv_out(lay, z, bg, conv_k, w_out, h, mod, gain, wr, br):
    D = lay.D
    tile = lambda i: i
    prev, nxt = _halo_specs(lay, D)
    shapes, specs = _tail_out(lay, tile)
    return pl.pallas_call(
        functools.partial(_convout_kernel, lay),
        out_shape=shapes,
        grid=(lay.NT,),
        in_specs=[pl.BlockSpec((TM, D), lambda i: (i, 0)), prev, nxt,
                  pl.BlockSpec((TM, D), lambda i: (i, 0)),
                  pl.BlockSpec(conv_k.shape, lambda i: (0, 0)),
                  pl.BlockSpec((D, D), lambda i: (0, 0))] + _tail_in_specs(lay, tile),
        out_specs=specs,
        scratch_shapes=TAIL_SCRATCH,
        compiler_params=_cparams(("arbitrary",)),
        name="conv_out",
    )(z, z, z, bg, conv_k, w_out, h, mod, gain.reshape(1, D), mod, mod, wr, br)


def _qkv_kernel(head_dim, use_norm, q_scale, u_ref, w_ref, qg_ref, kg_ref, cos_ref, sin_ref, q_ref, k_ref, v_ref):
    y = _dot(u_ref[...], w_ref[...])
    nq, nk = q_ref.shape[1], k_ref.shape[1]
    lane = lax.broadcasted_iota(jnp.int32, (TM, LANES), 1)

    def rotary(yg, gain_ref):
        if use_norm:
            yg = _rms(yg, gain_ref[...])
        if head_dim == LANES:
            rot = pltpu.roll(yg, LANES // 2, 1)
        else:
            q = head_dim // 2
            rot = jnp.where(lane % head_dim < q, pltpu.roll(yg, LANES - q, 1), pltpu.roll(yg, q, 1))
        return yg * cos_ref[...] + rot * sin_ref[...]

    for g in range(nq // LANES):
        cols = slice(g * LANES, (g + 1) * LANES)
        q_ref[:, cols] = (rotary(y[:, cols], qg_ref) * q_scale).astype(q_ref.dtype)
    for g in range(nk // LANES):
        cols = slice(g * LANES, (g + 1) * LANES)
        k_ref[:, cols] = rotary(y[:, nq + g * LANES:nq + (g + 1) * LANES], kg_ref).astype(k_ref.dtype)
    v_ref[...] = y[:, nq + nk:].astype(v_ref.dtype)


def _qkv_project(lay, u, w, nq, nk, head_dim, q_gain, k_gain, cos_t, sin_t, use_norm, q_scale, name):
    D = lay.D
    row = lambda n: pl.BlockSpec((TM, n), lambda i: (i, 0))
    table = pl.BlockSpec((TM, LANES), lambda i: (i % lay.TPB, 0))
    gain = pl.BlockSpec((1, LANES), lambda i: (0, 0))
    bf = jnp.bfloat16
    return pl.pallas_call(
        functools.partial(_qkv_kernel, head_dim, use_norm, q_scale),
        out_shape=(jax.ShapeDtypeStruct((lay.T, nq), bf), jax.ShapeDtypeStruct((lay.T, nk), bf),
                   jax.ShapeDtypeStruct((lay.T, nk), bf)),
        grid=(lay.NT,),
        in_specs=[row(D), pl.BlockSpec((D, nq + 2 * nk), lambda i: (0, 0)), gain, gain, table, table],
        out_specs=(row(nq), row(nk), row(nk)),
        compiler_params=_cparams(("arbitrary",)),
        name=name,
    )(u, w, q_gain, k_gain, cos_t, sin_t)


def _rope_tables(lay, head_dim):
    quarter = head_dim // 4
    rows = lay.S // GRID_W
    row = jnp.repeat(jnp.arange(rows), GRID_W).astype(jnp.float32)
    col = jnp.tile(jnp.arange(GRID_W), rows).astype(jnp.float32)
    inv = ROPE_THETA ** (-jnp.arange(quarter, dtype=jnp.float32) / quarter)
    ang = jnp.concatenate([row[:, None] * inv, col[:, None] * inv], axis=-1)
    cos, sin = jnp.cos(ang), jnp.sin(ang)
    reps = LANES // head_dim
    cos_t = jnp.tile(jnp.concatenate([cos, cos], axis=-1), (1, reps))
    sin_t = jnp.tile(jnp.concatenate([-sin, sin], axis=-1), (1, reps))
    cos_t = jnp.concatenate([cos_t, jnp.ones((lay.C, LANES), jnp.float32)], axis=0)
    sin_t = jnp.concatenate([sin_t, jnp.zeros((lay.C, LANES), jnp.float32)], axis=0)
    return cos_t, sin_t


def _flash_kernel(n_rep, tk, q_ref, qn_ref, k_ref, v_ref, o_ref, vx_ref, kt_ref, s_ref, m_ref, l_ref, acc_ref):
    hd = LANES
    nk = k_ref.shape[0] // tk
    first = pl.program_id(2) == 0
    ring = nk >= 3 and nk % 2 == 1

    def rows(j):
        start = j * tk
        return pl.ds(start if isinstance(start, int) else pl.multiple_of(start, tk), tk)

    def scores(qsrc, j, slot):
        kt = kt_ref[:, rows(j)]
        for h in range(n_rep):
            s_ref[slot, h] = _dot(qsrc[:, h * hd:(h + 1) * hd], kt)

    def step(j, rd, wr=None, qsrc=None, jn=None):
        vx = vx_ref[rows(j), :]
        if wr is not None:
            kt_next = kt_ref[:, rows(jn)]
        for h in range(n_rep):
            if wr is not None:
                s_ref[wr, h] = _dot(qsrc[:, h * hd:(h + 1) * hd], kt_next)
            s = s_ref[rd, h]
            m_prev = m_ref[h]
            m_next = jnp.maximum(m_prev, jnp.max(s, axis=1, keepdims=True))
            alpha = jnp.exp2(m_prev - m_next)
            p = jnp.exp2(s - m_next[:, :1]).astype(jnp.bfloat16)
            pv = _dot(p, vx)
            m_ref[h] = m_next
            l_ref[h] = alpha * l_ref[h] + pv[:, hd:]
            acc_ref[h] = alpha * acc_ref[h] + pv[:, :hd]

    @pl.when(first)
    def _():
        vx_ref[:, :hd] = v_ref[...]
        vx_ref[:, hd:] = jnp.ones((vx_ref.shape[0], hd), vx_ref.dtype)
        for c in range(nk):
            kt_ref[:, rows(c)] = k_ref[rows(c), :].astype(jnp.float32).T.astype(kt_ref.dtype)

    m_ref[...] = jnp.full(m_ref.shape, NEG_INF, jnp.float32)
    l_ref[...] = jnp.zeros(l_ref.shape, jnp.float32)
    acc_ref[...] = jnp.zeros(acc_ref.shape, jnp.float32)

    if ring:
        @pl.when(first)
        def _():
            scores(q_ref, 0, 2)

        step(0, 2, 1, q_ref, 1)

        def body(t, _):
            step(2 * t + 1, 1, 0, q_ref, 2 * t + 2)
            step(2 * t + 2, 0, 1, q_ref, 2 * t + 3)
            return 0

        lax.fori_loop(0, (nk - 3) // 2, body, 0)
        step(nk - 2, 1, 0, q_ref, nk - 1)
        step(nk - 1, 0, 2, qn_ref, 0)
    else:
        scores(q_ref, 0, 0)
        for j in range(nk):
            step(j, j % 2, *((1 - j % 2, q_ref, j + 1) if j + 1 < nk else ()))
    for h in range(n_rep):
        o_ref[:, h * hd:(h + 1) * hd] = (acc_ref[h] / l_ref[h]).astype(o_ref.dtype)


def _flash(lay, q, k, v, n_kv, n_rep, tq, tk, q_blk0, nq, kb, k_blk0, name):
    B = lay.B
    qw = n_rep * LANES
    return pl.pallas_call(
        functools.partial(_flash_kernel, n_rep, tk),
        out_shape=jax.ShapeDtypeStruct((B, nq * tq, n_kv * qw), jnp.bfloat16),
        grid=(B, n_kv, nq),
        in_specs=[pl.BlockSpec((None, tq, qw), lambda b, g, i: (b, q_blk0 + i, g)),
                  pl.BlockSpec((None, tq, qw), lambda b, g, i: (b, q_blk0 + jnp.minimum(i + 1, nq - 1), g)),
                  pl.BlockSpec((None, kb, LANES), lambda b, g, i: (b, k_blk0, g)),
                  pl.BlockSpec((None, kb, LANES), lambda b, g, i: (b, k_blk0, g))],
        out_specs=pl.BlockSpec((None, tq, qw), lambda b, g, i: (b, i, g)),
        scratch_shapes=[pltpu.VMEM((kb, 2 * LANES), jnp.bfloat16),
                        pltpu.VMEM((LANES, kb), jnp.bfloat16),
                        pltpu.VMEM((3, n_rep, tq, tk), jnp.float32),
                        pltpu.VMEM((n_rep, tq, LANES), jnp.float32),
                        pltpu.VMEM((n_rep, tq, LANES), jnp.float32),
                        pltpu.VMEM((n_rep, tq, LANES), jnp.float32)],
        compiler_params=_cparams(("arbitrary", "arbitrary", "arbitrary")),
        name=name,
    )(q, q, k, v)


def _swa_kernel(lay, tq, sink_ref, q_ref, kp_ref, kc_ref, kn_ref, kx_ref, vp_ref, vc_ref, vn_ref, vx_ref, o_ref,
                q_scr, p_scr):
    pair = pl.program_id(1)
    qb = pl.program_id(2)
    hd = LANES // 2
    n_rep = SWA_HEADS // SWA_KV_HEADS
    W = SWA_WINDOW
    kk = jnp.concatenate([kp_ref[...], kc_ref[...], kn_ref[...], kx_ref[...]], axis=0)
    vv = jnp.concatenate([vp_ref[...], vc_ref[...], vn_ref[...], vx_ref[...]], axis=0)
    nkeys = kk.shape[0]
    nwin = tq + 2 * W
    lane = lax.broadcasted_iota(jnp.int32, (nkeys, LANES), 1)
    low = lane < hd
    kk_sw = pltpu.roll(kk.astype(jnp.float32), hd, 1).astype(kk.dtype)
    vv_sw = pltpu.roll(vv.astype(jnp.float32), hd, 1).astype(vv.dtype)
    k_dup = (jnp.where(low, kk, kk_sw), jnp.where(low, kk_sw, kk))
    v_dup = (jnp.where(low, vv, vv_sw), jnp.where(low, vv_sw, vv))

    start = qb * tq
    qpos = start + lax.broadcasted_iota(jnp.int32, (tq, nkeys), 0)
    col = lax.broadcasted_iota(jnp.int32, (tq, nkeys), 1)
    kpos = start - W + col
    in_win = jnp.logical_and(jnp.abs(qpos - kpos) <= W, jnp.logical_and(kpos >= 0, kpos < lay.S))
    valid = jnp.logical_or(col >= nwin, in_win)

    qlane = lax.broadcasted_iota(jnp.int32, (tq, LANES), 1)
    qlow = qlane < hd
    for kvh in range(2):
        for j in range(n_rep):
            g = (kvh * n_rep + j) // 2
            qg = q_ref[:, g * LANES:(g + 1) * LANES]
            q_scr[j * tq:(j + 1) * tq, :] = jnp.where(qlow if j % 2 == 0 else jnp.logical_not(qlow),
                                                     qg, jnp.zeros_like(qg))
        s_all = lax.dot_general(q_scr[...], k_dup[kvh], (((1,), (1,)), ((), ())),
                                preferred_element_type=jnp.float32)
        denom = []
        for j in range(n_rep):
            sink = sink_ref[(pair * 2 + kvh) * n_rep + j]
            s = jnp.where(valid, s_all[j * tq:(j + 1) * tq, :], NEG_INF)
            m = jnp.maximum(jnp.max(s, axis=1, keepdims=True), sink)
            p = jnp.exp2(s - m)
            denom.append(jnp.sum(p, axis=1, keepdims=True) + jnp.exp2(sink - m))
            p_scr[j * tq:(j + 1) * tq, :] = p.astype(p_scr.dtype)
        pv = _dot(p_scr[...], v_dup[kvh])
        for jj in range(n_rep // 2):
            even = pv[(2 * jj) * tq:(2 * jj + 1) * tq, :] / denom[2 * jj]
            odd = pv[(2 * jj + 1) * tq:(2 * jj + 2) * tq, :] / denom[2 * jj + 1]
            g = (kvh * n_rep) // 2 + jj
            o_ref[:, g * LANES:(g + 1) * LANES] = jnp.where(qlow, even, odd).astype(o_ref.dtype)


def _swa(lay, q, k, v, sink2):
    B, S = lay.B, lay.S
    tq = TM
    W = SWA_WINDOW
    per = tq // W
    nq = S // tq
    n_rep = SWA_HEADS // SWA_KV_HEADS
    qw = 2 * n_rep * (LANES // 2)
    ctx_blk = S // lay.C
    kv_specs = [pl.BlockSpec((None, W, LANES), lambda b, p, i: (b, jnp.maximum(i * per - 1, 0), p)),
                pl.BlockSpec((None, tq, LANES), lambda b, p, i: (b, i, p)),
                pl.BlockSpec((None, W, LANES), lambda b, p, i: (b, (i + 1) * per, p)),
                pl.BlockSpec((None, lay.C, LANES), lambda b, p, i: (b, ctx_blk, p))]
    return pl.pallas_call(
        functools.partial(_swa_kernel, lay, tq),
        out_shape=jax.ShapeDtypeStruct((B, S, lay.D), jnp.bfloat16),
        grid=(B, SWA_KV_HEADS // 2, nq),
        in_specs=[pl.BlockSpec(memory_space=pltpu.SMEM),
                  pl.BlockSpec((None, tq, qw), lambda b, p, i: (b, i, p))] + kv_specs + kv_specs,
        out_specs=pl.BlockSpec((None, tq, qw), lambda b, p, i: (b, i, p)),
        scratch_shapes=[pltpu.VMEM((n_rep * tq, LANES), jnp.bfloat16),
                        pltpu.VMEM((n_rep * tq, tq + 2 * W + lay.C), jnp.bfloat16)],
        compiler_params=_cparams(("arbitrary", "arbitrary", "arbitrary")),
        name="swa",
    )(sink2, q, k, k, k, k, v, v, v, v)


def _outproj_kernel(lay, has_ctx, ol_ref, oc_ref, wo_ref, *rest):
    x = ol_ref[...]
    if has_ctx:
        _, w = lay.split(pl.program_id(0))
        x = jnp.where(w >= lay.LT, oc_ref[...], x)
    _mixer_tail(_dot(x, wo_ref[...]), *rest)


def _out_proj(lay, o_lat, o_ctx, w_out, h, mod, gain, wr, br):
    D = lay.D
    tile = lambda i: i
    has_ctx = o_ctx is not None
    if not has_ctx:
        o_ctx = o_lat
    shapes, specs = _tail_out(lay, tile)
    lat_spec, ctx_spec = _lat_ctx_specs(lay)
    return pl.pallas_call(
        functools.partial(_outproj_kernel, lay, has_ctx),
        out_shape=shapes,
        grid=(lay.NT,),
        in_specs=[lat_spec, ctx_spec if has_ctx else lat_spec,
                  pl.BlockSpec((D, D), lambda i: (0, 0))] + _tail_in_specs(lay, tile),
        out_specs=specs,
        scratch_shapes=TAIL_SCRATCH,
        compiler_params=_cparams(("arbitrary",)),
        name="attn_out",
    )(o_lat, o_ctx, w_out, h, mod, gain.reshape(1, D), mod, mod, wr, br)


def _pool_kernel(lay, u_ref, up_ref, un_ref, pw_ref, ps_ref, *rest):
    i = pl.program_id(0)
    _, w = lay.split(i)
    has_prev, has_next = _seq_flags(lay, i)
    in_ctx = w >= lay.LT
    seq_len = jnp.where(in_ctx, lay.C, lay.S)
    pos0 = jnp.where(in_ctx, w - lay.LT, w) * TM
    G = len(POOL_WINDOWS)
    gw = lay.D // G
    E_ROWS = TM + 2 * SUBLANES_F32
    pos = pos0 + lax.broadcasted_iota(jnp.int32, (TM, gw), 0)
    ys = []
    for g, win in enumerate(POOL_WINDOWS):
        sl = slice(g * gw, (g + 1) * gw)
        u = u_ref[:, sl].astype(jnp.float32)
        before = jnp.where(has_prev, up_ref[HALO - SUBLANES_F32:HALO, sl].astype(jnp.float32), 0.0)
        after = jnp.where(has_next, un_ref[0:SUBLANES_F32, sl].astype(jnp.float32), 0.0)
        e = jnp.concatenate([before, u, after], axis=0)
        left = win // 2
        right = win - 1 - left
        assert left == right + 1 and left & (left - 1) == 0
        acc = e
        span = 1
        while span < left:
            acc = acc + pltpu.roll(acc, E_ROWS - span, 0)
            span *= 2
        tot = pltpu.roll(acc, left, 0) + acc
        total = tot[SUBLANES_F32:SUBLANES_F32 + TM]
        cnt = jnp.minimum(pos + right, seq_len - 1) - jnp.maximum(pos - left, 0) + 1
        mean = total / cnt.astype(jnp.float32)
        ys.append(_dot((mean - u).astype(jnp.bfloat16), pw_ref[g]))
    y = jnp.concatenate(ys, axis=1) * ps_ref[...]
    _mixer_tail(y, *rest)


def _pool(lay, u, pool_w, pool_scale, h, mod, gain, wr, br):
    D = lay.D
    tile = lambda i: i
    prev, nxt = _halo_specs(lay, D)
    shapes, specs = _tail_out(lay, tile)
    return pl.pallas_call(
        functools.partial(_pool_kernel, lay),
        out_shape=shapes,
        grid=(lay.NT,),
        in_specs=[pl.BlockSpec((TM, D), lambda i: (i, 0)), prev, nxt,
                  pl.BlockSpec(pool_w.shape, lambda i: (0, 0, 0)),
                  _row_spec(D)] + _tail_in_specs(lay, tile),
        out_specs=specs,
        scratch_shapes=TAIL_SCRATCH,
        compiler_params=_cparams(("arbitrary",)),
        name="pool",
    )(u, u, u, pool_w, pool_scale.reshape(1, D), h, mod, gain.reshape(1, D), mod, mod, wr, br)


def _route_tile(lg, info_ref, gate_ref, cnt_ref, carry_ref):
    @pl.when(pl.program_id(0) == 0)
    def _():
        carry_ref[...] = jnp.zeros(carry_ref.shape, jnp.float32)

    lane = lax.broadcasted_iota(jnp.int32, lg.shape, 1)
    lane_f = lane.astype(jnp.float32)
    big = jnp.float32(4 * LANES)

    def first_lane(mask):
        return jnp.min(jnp.where(mask, lane_f, big), axis=1, keepdims=True).astype(jnp.int32)

    is_grp = lane < N_GROUPS
    gl = jnp.where(is_grp, lg, NEG_INF)
    gmax = jnp.max(gl, axis=1, keepdims=True)
    grp = first_lane(jnp.logical_and(is_grp, gl == gmax))
    p_grp = 1.0 / jnp.sum(jnp.where(is_grp, jnp.exp(gl - gmax), 0.0), axis=1, keepdims=True)
    eid = lane - N_GROUPS
    in_grp = jnp.logical_and(lane >= N_GROUPS + grp * EXPERTS_PER_GROUP,
                             lane < N_GROUPS + (grp + 1) * EXPERTS_PER_GROUP)
    el = jnp.where(in_grp, lg, NEG_INF)
    t1 = jnp.max(el, axis=1, keepdims=True)
    e1 = first_lane(jnp.logical_and(in_grp, el == t1)) - N_GROUPS
    rest = jnp.logical_and(in_grp, eid != e1)
    el2 = jnp.where(rest, lg, NEG_INF)
    t2 = jnp.max(el2, axis=1, keepdims=True)
    e2 = first_lane(jnp.logical_and(rest, el2 == t2)) - N_GROUPS
    d = jnp.exp(t2 - t1)
    g1 = p_grp / (1.0 + d)
    g2 = p_grp * d / (1.0 + d)

    oh1 = lane == e1
    oh2 = lane == e2
    oh = jnp.where(jnp.logical_or(oh1, oh2), 1.0, 0.0)
    r = lax.broadcasted_iota(jnp.int32, (TM, TM), 0)
    c = lax.broadcasted_iota(jnp.int32, (TM, TM), 1)
    tri = jnp.where(c < r, 1.0, 0.0).astype(jnp.bfloat16)
    before = _dot(tri, oh.astype(jnp.bfloat16)) + carry_ref[0:1, :]
    r1 = jnp.sum(jnp.where(oh1, before, 0.0), axis=1, keepdims=True)
    r2 = jnp.sum(jnp.where(oh2, before, 0.0), axis=1, keepdims=True)
    carry = carry_ref[0:1, :] + jnp.sum(oh, axis=0, keepdims=True)
    carry_ref[...] = jnp.broadcast_to(carry, carry_ref.shape)

    info = jnp.where(lane == 0, e1, jnp.where(lane == 1, e2, jnp.where(
        lane == 2, r1.astype(jnp.int32), jnp.where(lane == 3, r2.astype(jnp.int32), 0))))
    info_ref[...] = info
    gate_ref[...] = jnp.where(lane == 0, g1, jnp.where(lane == 1, g2, 0.0))
    cnt_ref[...] = jnp.broadcast_to(carry, cnt_ref.shape).astype(jnp.int32)


def _dispatch_kernel(n_blocks, plan_ref, dest_ref, v_ref, xs_hbm, zbuf, sem, zsem):
    @pl.when(pl.program_id(0) == 0)
    def _():
        zbuf[...] = jnp.zeros(zbuf.shape, zbuf.dtype)

        def zero_block(row0):
            rows = pl.ds(pl.multiple_of(row0, EXPERT_ROWS), EXPERT_ROWS)
            return pltpu.make_async_copy(zbuf, xs_hbm.at[rows], zsem)

        def for_each_zero_block(fn):
            def seg(e, _):
                @pl.when(plan_ref[N_EXPERTS + e] > 0)
                def _():
                    fn(zero_block(plan_ref[e] - EXPERT_ROWS))
                return 0

            def tail(b, _):
                fn(zero_block(b * EXPERT_ROWS))
                return 0

            lax.fori_loop(0, N_EXPERTS, seg, 0)
            lax.fori_loop(plan_ref[2 * N_EXPERTS], n_blocks, tail, 0)

        for_each_zero_block(lambda cp: cp.start())
        for_each_zero_block(lambda cp: cp.wait())

    def issue(r, _):
        for k in range(TOP_K):
            pltpu.make_async_copy(v_ref.at[pl.ds(r, 1)], xs_hbm.at[pl.ds(dest_ref[0, k, r], 1)],
                                  sem).start(priority=k % DMA_PRIORITIES)
        return 0

    lax.fori_loop(0, TM, issue, 0, unroll=True)
    for k in range(TOP_K):
        pltpu.make_async_copy(v_ref, xs_hbm.at[pl.ds(0, TM)], sem).wait()


def _dispatch(lay, v, dest, plan, n_blocks):
    W = v.shape[1]
    return pl.pallas_call(
        functools.partial(_dispatch_kernel, n_blocks),
        out_shape=jax.ShapeDtypeStruct((n_blocks * EXPERT_ROWS, W), v.dtype),
        grid=(lay.NT,),
        in_specs=[pl.BlockSpec(memory_space=pltpu.SMEM),
                  pl.BlockSpec((1, TOP_K, TM), lambda i: (i, 0, 0), memory_space=pltpu.SMEM),
                  pl.BlockSpec((TM, W), lambda i: (i, 0))],
        out_specs=pl.BlockSpec(memory_space=pl.ANY),
        scratch_shapes=[pltpu.VMEM((EXPERT_ROWS, W), v.dtype),
                        pltpu.SemaphoreType.DMA(()), pltpu.SemaphoreType.DMA(())],
        compiler_params=pltpu.CompilerParams(dimension_semantics=("arbitrary",), has_side_effects=True),
        name="moe_dispatch",
    )(plan, dest, v)


def _expert_kernel(layer, be_ref, na_ref, nxt_ref, slot_ref, x_ref, wg_hbm, wu_hbm, wd_hbm, y_ref,
                   wg_buf, wu_buf, wd_buf, wgu_s, wd_s, sem):
    i = pl.program_id(0)
    de = wg_buf.shape[2]
    active = i < na_ref[0]
    e = be_ref[i]

    def fetch(expert, slot):
        return (pltpu.make_async_copy(wg_hbm.at[layer, expert], wg_buf.at[slot], sem.at[slot, 0]),
                pltpu.make_async_copy(wu_hbm.at[layer, expert], wu_buf.at[slot], sem.at[slot, 1]),
                pltpu.make_async_copy(wd_hbm.at[layer, expert], wd_buf.at[slot], sem.at[slot, 2]))

    @pl.when(i == 0)
    def _():
        for cp in fetch(e, slot_ref[e]):
            cp.start()

    @pl.when(jnp.logical_and(active, jnp.logical_or(i == 0, e != be_ref[jnp.maximum(i - 1, 0)])))
    def _():
        slot = slot_ref[e]
        for cp in fetch(e, slot):
            cp.wait()

        @pl.when(nxt_ref[e] >= 0)
        def _():
            for cp in fetch(nxt_ref[e], 1 - slot):
                cp.start()

        wgu_s[:, :de] = wg_buf[slot].astype(wgu_s.dtype)
        wgu_s[:, de:] = wu_buf[slot].astype(wgu_s.dtype)
        wd_s[...] = wd_buf[slot].astype(wd_s.dtype)

    @pl.when(active)
    def _():
        hi, lo = _unpack_rows(x_ref[...])
        x = jnp.concatenate([hi.astype(jnp.bfloat16), lo.astype(jnp.bfloat16)], axis=1)
        hgu = _dot(x, wgu_s[...])
        hg, hu = hgu[:, :de], hgu[:, de:]
        a = (hg * (1.0 / (1.0 + jnp.exp(-hg)))) * hu
        y_ref[...] = _pack_rows(_dot(a.astype(jnp.bfloat16), wd_s[...]))

    @pl.when(jnp.logical_not(active))
    def _():
        y_ref[...] = jnp.zeros(y_ref.shape, y_ref.dtype)


def _experts(lay, xs, blk_expert, n_active, nxt, slot, layer, wg, wu, wd):
    D = lay.D
    P, W = xs.shape
    DE = wg.shape[3]
    row_block = pl.BlockSpec((EXPERT_ROWS, W), lambda i, *_: (i, 0))
    hbm = pl.BlockSpec(memory_space=pl.ANY)
    grid_spec = pltpu.PrefetchScalarGridSpec(
        num_scalar_prefetch=4,
        grid=(P // EXPERT_ROWS,),
        in_specs=[row_block, hbm, hbm, hbm],
        out_specs=row_block,
        scratch_shapes=[pltpu.VMEM((2, D, DE), jnp.float32), pltpu.VMEM((2, D, DE), jnp.float32),
                        pltpu.VMEM((2, DE, D), jnp.float32),
                        pltpu.VMEM((D, 2 * DE), jnp.bfloat16), pltpu.VMEM((DE, D), jnp.bfloat16),
                        pltpu.SemaphoreType.DMA((2, 3))],
    )
    return pl.pallas_call(
        functools.partial(_expert_kernel, layer),
        out_shape=jax.ShapeDtypeStruct((P, W), xs.dtype),
        grid_spec=grid_spec,
        compiler_params=_cparams(("arbitrary",)),
        name="moe_experts",
    )(blk_expert, n_active, nxt, slot, xs, wg, wu, wd)


def _combine_kernel(final, *refs):
    dests, refs = refs[:COMBINE_AHEAD + 1], refs[COMBINE_AHEAD + 1:]
    ys_hbm, gate_ref, h_ref, g2_ref, gain_ref, sc_ref, sh_ref, *out_refs, ybuf, sem = refs
    i = pl.program_id(0)
    n_slots = COMBINE_AHEAD + 1

    def gather(dref, buf_slot, r0, r1):
        for r in range(r0, r1):
            for k in range(TOP_K):
                pltpu.make_async_copy(ys_hbm.at[pl.ds(dref[0, k, r], 1)], ybuf.at[buf_slot, k, pl.ds(r, 1)],
                                      sem.at[buf_slot]).start(priority=k % DMA_PRIORITIES)

    def wait_tile(buf_slot):
        for k in range(TOP_K):
            pltpu.make_async_copy(ys_hbm.at[pl.ds(0, TM)], ybuf.at[buf_slot, k], sem.at[buf_slot]).wait()

    @pl.when(i == 0)
    def _():
        for a in range(COMBINE_AHEAD):
            gather(dests[a], a, 0, TM)

    def step(slot):
        wait_tile(slot)
        for r0 in range(0, TM, COMBINE_ROWS):
            rows = pl.ds(r0, COMBINE_ROWS)
            gate = gate_ref[rows, :]
            hi0, lo0 = _unpack_rows(ybuf[slot, 0, rows, :])
            hi1, lo1 = _unpack_rows(ybuf[slot, 1, rows, :])
            g0, g1 = gate[:, 0:1], gate[:, 1:2]
            f = jnp.concatenate([g0 * hi0 + g1 * hi1, g0 * lo0 + g1 * lo1], axis=1)
            h2 = h_ref[rows, :] + g2_ref[...] * f
            if final:
                (out_ref,) = out_refs
                out_ref[rows, :] = _rms(h2, gain_ref[...])
            else:
                h2_ref, u_ref = out_refs
                h2_ref[rows, :] = h2
                u_ref[rows, :] = (_rms(h2, gain_ref[...]) * (1.0 + sc_ref[...]) + sh_ref[...]).astype(u_ref.dtype)
            gather(dests[COMBINE_AHEAD], (slot + COMBINE_AHEAD) % n_slots, r0, r0 + COMBINE_ROWS)

        @pl.when(i == pl.num_programs(0) - 1)
        def _():
            for a in range(1, n_slots):
                wait_tile((slot + a) % n_slots)

    for slot in range(n_slots):
        pl.when(i % n_slots == slot)(functools.partial(step, slot))


def _combine(lay, ys, dest, gates, h1, mod, mod_next, gain_next, final):
    D = lay.D
    row_tile = pl.BlockSpec((TM, D), lambda i: (i, 0))
    if final:
        n = lay.B * lay.LT
        tile = lambda i: (i // lay.LT) * lay.TPB + i % lay.LT
        out_shape = (jax.ShapeDtypeStruct((lay.B * lay.S, D), jnp.float32),)
        out_specs = (row_tile,)
    else:
        n = lay.NT
        tile = lambda i: i
        out_shape = (jax.ShapeDtypeStruct((lay.T, D), jnp.float32),
                     jax.ShapeDtypeStruct((lay.T, D), jnp.bfloat16))
        out_specs = (row_tile, row_tile)
    return pl.pallas_call(
        functools.partial(_combine_kernel, final),
        out_shape=out_shape,
        grid=(n,),
        in_specs=[pl.BlockSpec((1, TOP_K, TM), lambda i, a=a: (tile(jnp.minimum(i + a, n - 1)), 0, 0),
                               memory_space=pltpu.SMEM) for a in range(COMBINE_AHEAD + 1)] + [
                  pl.BlockSpec(memory_space=pl.ANY),
                  pl.BlockSpec((TM, LANES), lambda i: (tile(i), 0)),
                  pl.BlockSpec((TM, D), lambda i: (tile(i), 0)),
                  _mod_spec(lay, 5, tile),
                  _row_spec(D),
                  _mod_spec(lay, 1, tile), _mod_spec(lay, 0, tile)],
        out_specs=out_specs,
        scratch_shapes=[pltpu.VMEM((COMBINE_AHEAD + 1, TOP_K, TM, ys.shape[1]), ys.dtype),
                        pltpu.SemaphoreType.DMA((COMBINE_AHEAD + 1,))],
        compiler_params=_cparams(("arbitrary",)),
        name="moe_combine_final" if final else "moe_combine",
    )(*([dest] * (COMBINE_AHEAD + 1)), ys, gates, h1, mod, gain_next.reshape(1, D), mod_next, mod_next)


def _moe(lay, v, info, gates, cnt, h1, mod, mod_next, gain_next, layer, wg, wu, wd, final):
    T, NT = lay.T, lay.NT
    counts = cnt[0, :N_EXPERTS]
    padded = ((counts + EXPERT_ROWS - 1) // EXPERT_ROWS) * EXPERT_ROWS
    pad_end = jnp.cumsum(padded)
    pad_start = pad_end - padded
    expert = info[:, :TOP_K]
    onehot = expert[:, :, None] == jnp.arange(N_EXPERTS, dtype=jnp.int32)
    dest = info[:, TOP_K:2 * TOP_K] + jnp.sum(jnp.where(onehot, pad_start, 0), axis=-1)
    dest = dest.reshape(NT, TM, TOP_K).transpose(0, 2, 1).astype(jnp.int32)
    n_blocks = -(-(T * TOP_K + N_EXPERTS * (EXPERT_ROWS - 1)) // EXPERT_ROWS)
    blk_start = jnp.arange(n_blocks, dtype=jnp.int32) * EXPERT_ROWS
    blk_expert = jnp.minimum(jnp.sum(pad_end[None, :] <= blk_start[:, None], axis=1),
                             N_EXPERTS - 1).astype(jnp.int32)
    n_active = (pad_end[-1:] // EXPERT_ROWS).astype(jnp.int32)
    plan = jnp.concatenate([pad_end, padded, n_active]).astype(jnp.int32)
    xs = _dispatch(lay, v, dest, plan, n_blocks)
    owns = padded > 0
    ids = jnp.arange(N_EXPERTS, dtype=jnp.int32)
    later = jnp.where(owns[None, :] & (ids[None, :] > ids[:, None]), ids[None, :], N_EXPERTS)
    nxt = jnp.min(later, axis=1)
    nxt = jnp.where(nxt == N_EXPERTS, -1, nxt).astype(jnp.int32)
    slot = ((jnp.cumsum(owns.astype(jnp.int32)) - 1) % 2).astype(jnp.int32)
    ys = _experts(lay, xs, blk_expert, n_active, nxt, slot, layer, wg, wu, wd)
    return _combine(lay, ys, dest, gates, h1, mod, mod_next, gain_next, final)


def kernel(x, c, ctx, c_ctx, ada_w, ada_b, norm_mix, norm_ffn, norm_final, conv_in, conv_k, conv_out, gqa_qkv, gqa_q_gain, gqa_k_gain, gqa_out, pool_w, pool_scale, swa_qkv, swa_sink, swa_out, router_grp_w, router_grp_b, router_exp_w, router_exp_b, exp_gate, exp_up, exp_down):
    B, S, D = x.shape
    C = ctx.shape[1]
    L = ada_w.shape[0]
    lay = Layout(B, S, C, D)
    bf = jnp.bfloat16

    R = SUBLANES_BF16
    cvec = jnp.zeros((R, D), jnp.float32).at[:B].set(c).at[B].set(c_ctx)
    mod_all = _modulation(cvec, ada_w, ada_b)[:, :B + 1].reshape(L, B + 1, 6, 1, D).transpose(0, 2, 1, 3, 4)

    pad = LANES - N_GROUPS - N_EXPERTS
    wr_all = jnp.concatenate([router_grp_w, router_exp_w, jnp.zeros((L, D, pad), jnp.float32)], axis=-1).astype(bf)
    br_all = jnp.concatenate([router_grp_b, router_exp_b, jnp.zeros((L, pad), jnp.float32)], axis=-1)

    h, u = _prenorm(lay, x, ctx, norm_mix[0], mod_all[0])
    one = jnp.ones((1, LANES), jnp.float32)

    for i in range(L):
        m, j = i % 4, i // 4
        mod = mod_all[i]
        wr, br = wr_all[i], br_all[i].reshape(1, LANES)
        tail = (h, mod, norm_ffn[i], wr, br)
        if m == 0:
            bg, z = _conv_in(lay, u, conv_in[j].astype(bf))
            routed = _conv_out(lay, z, bg, conv_k[j], conv_out[j].astype(bf), *tail)
        elif m == 1:
            hd = D // GQA_HEADS
            w = gqa_qkv[j].astype(bf)
            cos_t, sin_t = _rope_tables(lay, hd)
            nq, nk = GQA_HEADS * hd, GQA_KV_HEADS * hd
            qg = gqa_q_gain[j].reshape(1, hd)
            kg = gqa_k_gain[j].reshape(1, hd)
            q, k, vv = _qkv_project(lay, u, w, nq, nk, hd, qg, kg, cos_t, sin_t, True, hd ** -0.5 * LOG2E, "gqa_qkv")
            q3, k3, v3 = (a.reshape(B, lay.SB, -1) for a in (q, k, vv))
            n_rep = GQA_HEADS // GQA_KV_HEADS
            tq = 512 if S % 512 == 0 else TM
            tk = 768 if lay.SB % 768 == 0 else TM
            o_lat = _flash(lay, q3, k3, v3, GQA_KV_HEADS, n_rep, tq, tk, 0, S // tq, lay.SB, 0, "gqa_flash")
            o_ctx = _flash(lay, q3, k3, v3, GQA_KV_HEADS, n_rep, C, C, S // C, 1, C, S // C, "gqa_flash_ctx")
            routed = _out_proj(lay, o_lat, o_ctx, gqa_out[j].astype(bf), *tail)
        elif m == 2:
            routed = _pool(lay, u, pool_w[j].astype(bf), pool_scale[j], *tail)
        else:
            hd = D // SWA_HEADS
            w = swa_qkv[j].astype(bf)
            cos_t, sin_t = _rope_tables(lay, hd)
            nq, nk = SWA_HEADS * hd, SWA_KV_HEADS * hd
            q, k, vv = _qkv_project(lay, u, w, nq, nk, hd, one, one, cos_t, sin_t, False, hd ** -0.5 * LOG2E, "swa_qkv")
            q3, k3, v3 = (a.reshape(B, lay.SB, -1) for a in (q, k, vv))
            o_lat = _swa(lay, q3, k3, v3, swa_sink[j] * LOG2E)
            routed = _out_proj(lay, o_lat, None, swa_out[j].astype(bf), *tail)
        final = i == L - 1
        mod_next = mod if final else mod_all[i + 1]
        gain_next = norm_final if final else norm_mix[i + 1]
        h1, v, info, gates, cnt = routed
        res = _moe(lay, v, info, gates, cnt, h1, mod, mod_next, gain_next, i, exp_gate, exp_up, exp_down, final)
        if final:
            return res[0].reshape(B, S, D)
        h, u = res
```

```python
import functools

import jax
import jax.numpy as jnp
from jax import lax
from jax.experimental import pallas as pl
from jax.experimental.pallas import tpu as pltpu

GRID_W = 64
NORM_EPS = 1e-6
ROPE_THETA = 10000.0
NEG_INF = -1e30
GQA_HEADS, GQA_KV_HEADS = 16, 4
SWA_HEADS, SWA_KV_HEADS, SWA_WINDOW = 32, 8, 128
POOL_WINDOWS = (2, 4, 8, 16)
N_GROUPS, EXPERTS_PER_GROUP, TOP_K = 8, 4, 2
N_EXPERTS = N_GROUPS * EXPERTS_PER_GROUP
LOG2E = 1.4426950408889634

LANES = 128
SUBLANES_F32 = 8
SUBLANES_BF16 = 16
VMEM_LIMIT = 56 * 1024 * 1024
TM = 256
EXPERT_ROWS = 256
COMBINE_ROWS = 32
COMBINE_AHEAD = 2
DMA_PRIORITIES = 2
DISPATCH_SLOTS = 3
HALO = SUBLANES_BF16


def _cparams(sem):
    return pltpu.CompilerParams(dimension_semantics=sem, vmem_limit_bytes=VMEM_LIMIT)


class Layout:
    def __init__(self, B, S, C, D):
        assert S % TM == 0 and C % TM == 0
        self.B, self.S, self.C, self.D = B, S, C, D
        self.SB = S + C
        self.T = B * self.SB
        self.LT = S // TM
        self.TPB = self.SB // TM
        self.NT = B * self.TPB

    def split(self, i):
        return i // self.TPB, i % self.TPB

    def mod_row(self, i):
        b, w = self.split(i)
        return jnp.where(w >= self.LT, self.B, b)


def _mod_spec(lay, chunk, tile_of):
    return pl.BlockSpec((None, None, 1, lay.D),
                        lambda *g: (chunk, lay.mod_row(tile_of(*g)), 0, 0))


def _row_spec(D):
    return pl.BlockSpec((1, D), lambda *g: (0, 0))


def _rms(x, gain):
    ms = jnp.mean(x * x, axis=-1, keepdims=True)
    return (x * lax.rsqrt(ms + NORM_EPS)) * gain


def _dot(a, b):
    return jnp.dot(a, b, preferred_element_type=jnp.float32)


def _mod_kernel(c_ref, w_ref, b_ref, o_ref):
    c = c_ref[...]
    a = (c * (1.0 / (1.0 + jnp.exp(-c)))).astype(jnp.bfloat16)
    o_ref[...] = _dot(a, w_ref[...].astype(jnp.bfloat16)) + b_ref[...]


def _modulation(cvec, ada_w, ada_b):
    L, D, N = ada_w.shape
    R = cvec.shape[0]
    tn = 512
    return pl.pallas_call(
        _mod_kernel,
        out_shape=jax.ShapeDtypeStruct((L, R, N), jnp.float32),
        grid=(L, N // tn),
        in_specs=[pl.BlockSpec((R, D), lambda l, j: (0, 0)),
                  pl.BlockSpec((None, D, tn), lambda l, j: (l, 0, j)),
                  pl.BlockSpec((None, 1, tn), lambda l, j: (l, 0, j))],
        out_specs=pl.BlockSpec((None, R, tn), lambda l, j: (l, 0, j)),
        compiler_params=_cparams(("arbitrary", "arbitrary")),
        name="adaln_mod",
    )(cvec, ada_w, ada_b.reshape(L, 1, N))


def _prenorm_kernel(lay, x_ref, ctx_ref, gain_ref, sc_ref, sh_ref, h_ref, u_ref):
    _, w = lay.split(pl.program_id(0))
    h = jnp.where(w >= lay.LT, ctx_ref[...], x_ref[...])
    h_ref[...] = h
    u_ref[...] = (_rms(h, gain_ref[...]) * (1.0 + sc_ref[...]) + sh_ref[...]).astype(u_ref.dtype)


def _lat_ctx_specs(lay):
    def lat_idx(i):
        b, w = lay.split(i)
        return b, jnp.minimum(w, lay.LT - 1), 0

    def ctx_idx(i):
        b, w = lay.split(i)
        return b, jnp.clip(w - lay.LT, 0, lay.TPB - lay.LT - 1), 0

    return pl.BlockSpec((None, TM, lay.D), lat_idx), pl.BlockSpec((None, TM, lay.D), ctx_idx)


def _prenorm(lay, x, ctx, gain, mod):
    D = lay.D
    tile = lambda i: i
    row = pl.BlockSpec((TM, D), lambda i: (i, 0))
    return pl.pallas_call(
        functools.partial(_prenorm_kernel, lay),
        out_shape=(jax.ShapeDtypeStruct((lay.T, D), jnp.float32),
                   jax.ShapeDtypeStruct((lay.T, D), jnp.bfloat16)),
        grid=(lay.NT,),
        in_specs=[*_lat_ctx_specs(lay), _row_spec(D), _mod_spec(lay, 1, tile), _mod_spec(lay, 0, tile)],
        out_specs=(row, row),
        compiler_params=_cparams(("arbitrary",)),
        name="prenorm",
    )(x, ctx, gain.reshape(1, D), mod, mod)


def _pack_rows(x):
    w = x.shape[1] // 2
    hi = pltpu.bitcast(x[:, :w].astype(jnp.bfloat16).astype(jnp.float32), jnp.uint32)
    lo = pltpu.bitcast(x[:, w:].astype(jnp.bfloat16).astype(jnp.float32), jnp.uint32)
    return hi | (lo >> 16)


def _unpack_rows(words):
    hi = pltpu.bitcast(words & jnp.uint32(0xFFFF0000), jnp.float32)
    lo = pltpu.bitcast(words << 16, jnp.float32)
    return hi, lo


def _mixer_tail(y, h_ref, g1_ref, gain_ref, sc_ref, sh_ref, wr_ref, br_ref,
                h1_ref, v_ref, info_ref, gate_ref, cnt_ref, carry_ref):
    h1 = h_ref[...] + g1_ref[...] * y
    h1_ref[...] = h1
    v = _rms(h1, gain_ref[...]) * (1.0 + sc_ref[...]) + sh_ref[...]
    v_ref[...] = _pack_rows(v)
    logits = _dot(v.astype(jnp.bfloat16), wr_ref[...]) + br_ref[...]
    _route_tile(logits, info_ref, gate_ref, cnt_ref, carry_ref)


def _tail_in_specs(lay, tile):
    D = lay.D
    return [pl.BlockSpec((TM, D), lambda *g: (tile(*g), 0)),
            _mod_spec(lay, 2, tile),
            _row_spec(D),
            _mod_spec(lay, 4, tile), _mod_spec(lay, 3, tile),
            pl.BlockSpec((D, LANES), lambda *g: (0, 0)),
            pl.BlockSpec((1, LANES), lambda *g: (0, 0))]


def _tail_out(lay, tile):
    D = lay.D
    shapes = (jax.ShapeDtypeStruct((lay.T, D), jnp.float32),
              jax.ShapeDtypeStruct((lay.T, D // 2), jnp.uint32),
              jax.ShapeDtypeStruct((lay.T, LANES), jnp.int32),
              jax.ShapeDtypeStruct((lay.T, LANES), jnp.float32),
              jax.ShapeDtypeStruct((SUBLANES_F32, LANES), jnp.int32))
    specs = (pl.BlockSpec((TM, D), lambda *g: (tile(*g), 0)),
             pl.BlockSpec((TM, D // 2), lambda *g: (tile(*g), 0)),
             pl.BlockSpec((TM, LANES), lambda *g: (tile(*g), 0)),
             pl.BlockSpec((TM, LANES), lambda *g: (tile(*g), 0)),
             pl.BlockSpec((SUBLANES_F32, LANES), lambda *g: (0, 0)))
    return shapes, specs


TAIL_SCRATCH = [pltpu.VMEM((SUBLANES_F32, LANES), jnp.float32)]


def _seq_flags(lay, i):
    _, w = lay.split(i)
    has_prev = jnp.logical_and(w != 0, w != lay.LT)
    has_next = jnp.logical_and(w != lay.LT - 1, w != lay.TPB - 1)
    return has_prev, has_next


def _halo_specs(lay, width, col=lambda *g: 0, tile=lambda i: i):
    per = TM // HALO
    last = lay.T // HALO - 1
    prev = pl.BlockSpec((HALO, width), lambda *g: (jnp.maximum(tile(*g) * per - 1, 0), col(*g)))
    nxt = pl.BlockSpec((HALO, width), lambda *g: (jnp.minimum((tile(*g) + 1) * per, last), col(*g)))
    return prev, nxt


def _convin_kernel(u_ref, wb_ref, wc_ref, wx_ref, bg_ref, z_ref):
    x = u_ref[...]
    bg_ref[...] = _dot(x, wb_ref[...]).astype(bg_ref.dtype)
    z_ref[...] = (_dot(x, wc_ref[...]) * _dot(x, wx_ref[...])).astype(z_ref.dtype)


def _conv_in(lay, u, w_in):
    D = lay.D
    tn = 1024
    nb = D // tn
    out = jax.ShapeDtypeStruct((lay.T, D), jnp.bfloat16)
    return pl.pallas_call(
        _convin_kernel,
        out_shape=(out, out),
        grid=(nb, lay.NT),
        in_specs=[pl.BlockSpec((TM, D), lambda j, i: (i, 0)),
                  pl.BlockSpec((D, tn), lambda j, i: (0, j)),
                  pl.BlockSpec((D, tn), lambda j, i: (0, nb + j)),
                  pl.BlockSpec((D, tn), lambda j, i: (0, 2 * nb + j))],
        out_specs=(pl.BlockSpec((TM, tn), lambda j, i: (i, j)),
                   pl.BlockSpec((TM, tn), lambda j, i: (i, j))),
        compiler_params=_cparams(("arbitrary", "arbitrary")),
        name="conv_in",
    )(u, w_in, w_in, w_in)


def _convout_kernel(lay, z_ref, zp_ref, zn_ref, bg_ref, ck_ref, wo_ref, *rest):
    i = pl.program_id(0)
    has_prev, has_next = _seq_flags(lay, i)
    z = z_ref[...].astype(jnp.float32)
    row = lax.broadcasted_iota(jnp.int32, z.shape, 0)
    prev_row = jnp.where(has_prev, zp_ref[HALO - 1:HALO, :].astype(jnp.float32), 0.0)
    next_row = jnp.where(has_next, zn_ref[0:1, :].astype(jnp.float32), 0.0)
    z_m1 = jnp.where(row == 0, prev_row, pltpu.roll(z, 1, 0))
    z_p1 = jnp.where(row == TM - 1, next_row, pltpu.roll(z, TM - 1, 0))
    y = z_m1 * ck_ref[0:1, :] + z * ck_ref[1:2, :] + z_p1 * ck_ref[2:3, :]
    g = (bg_ref[...].astype(jnp.float32) * y).astype(jnp.bfloat16)
    _mixer_tail(_dot(g, wo_ref[...]), *rest)


def _conv_out(lay, z, bg, conv_k, w_out, h, mod, gain, wr, br):
    D = lay.D
    tile = lambda i: i
    prev, nxt = _halo_specs(lay, D)
    shapes, specs = _tail_out(lay, tile)
    return pl.pallas_call(
        functools.partial(_convout_kernel, lay),
        out_shape=shapes,
        grid=(lay.NT,),
        in_specs=[pl.BlockSpec((TM, D), lambda i: (i, 0)), prev, nxt,
                  pl.BlockSpec((TM, D), lambda i: (i, 0)),
                  pl.BlockSpec(conv_k.shape, lambda i: (0, 0)),
                  pl.BlockSpec((D, D), lambda i: (0, 0))] + _tail_in_specs(lay, tile),
        out_specs=specs,
        scratch_shapes=TAIL_SCRATCH,
        compiler_params=_cparams(("arbitrary",)),
        name="conv_out",
    )(z, z, z, bg, conv_k, w_out, h, mod, gain.reshape(1, D), mod, mod, wr, br)


def _qkv_kernel(head_dim, use_norm, q_scale, u_ref, w_ref, qg_ref, kg_ref, cos_ref, sin_ref, q_ref, k_ref, v_ref):
    y = _dot(u_ref[...], w_ref[...])
    nq, nk = q_ref.shape[1], k_ref.shape[1]
    lane = lax.broadcasted_iota(jnp.int32, (TM, LANES), 1)

    def rotary(yg, gain_ref):
        if use_norm:
            yg = _rms(yg, gain_ref[...])
        if head_dim == LANES:
            rot = pltpu.roll(yg, LANES // 2, 1)
        else:
            q = head_dim // 2
            rot = jnp.where(lane % head_dim < q, pltpu.roll(yg, LANES - q, 1), pltpu.roll(yg, q, 1))
        return yg * cos_ref[...] + rot * sin_ref[...]

    for g in range(nq // LANES):
        cols = slice(g * LANES, (g + 1) * LANES)
        q_ref[:, cols] = (rotary(y[:, cols], qg_ref) * q_scale).astype(q_ref.dtype)
    for g in range(nk // LANES):
        cols = slice(g * LANES, (g + 1) * LANES)
        k_ref[:, cols] = rotary(y[:, nq + g * LANES:nq + (g + 1) * LANES], kg_ref).astype(k_ref.dtype)
    v_ref[...] = y[:, nq + nk:].astype(v_ref.dtype)


def _qkv_project(lay, u, w, nq, nk, head_dim, q_gain, k_gain, cos_t, sin_t, use_norm, q_scale, name):
    D = lay.D
    row = lambda n: pl.BlockSpec((TM, n), lambda i: (i, 0))
    table = pl.BlockSpec((TM, LANES), lambda i: (i % lay.TPB, 0))
    gain = pl.BlockSpec((1, LANES), lambda i: (0, 0))
    bf = jnp.bfloat16
    return pl.pallas_call(
        functools.partial(_qkv_kernel, head_dim, use_norm, q_scale),
        out_shape=(jax.ShapeDtypeStruct((lay.T, nq), bf), jax.ShapeDtypeStruct((lay.T, nk), bf),
                   jax.ShapeDtypeStruct((lay.T, nk), bf)),
        grid=(lay.NT,),
        in_specs=[row(D), pl.BlockSpec((D, nq + 2 * nk), lambda i: (0, 0)), gain, gain, table, table],
        out_specs=(row(nq), row(nk), row(nk)),
        compiler_params=_cparams(("arbitrary",)),
        name=name,
    )(u, w, q_gain, k_gain, cos_t, sin_t)


def _rope_tables(lay, head_dim):
    quarter = head_dim // 4
    rows = lay.S // GRID_W
    row = jnp.repeat(jnp.arange(rows), GRID_W).astype(jnp.float32)
    col = jnp.tile(jnp.arange(GRID_W), rows).astype(jnp.float32)
    inv = ROPE_THETA ** (-jnp.arange(quarter, dtype=jnp.float32) / quarter)
    ang = jnp.concatenate([row[:, None] * inv, col[:, None] * inv], axis=-1)
    cos, sin = jnp.cos(ang), jnp.sin(ang)
    reps = LANES // head_dim
    cos_t = jnp.tile(jnp.concatenate([cos, cos], axis=-1), (1, reps))
    sin_t = jnp.tile(jnp.concatenate([-sin, sin], axis=-1), (1, reps))
    cos_t = jnp.concatenate([cos_t, jnp.ones((lay.C, LANES), jnp.float32)], axis=0)
    sin_t = jnp.concatenate([sin_t, jnp.zeros((lay.C, LANES), jnp.float32)], axis=0)
    return cos_t, sin_t


def _flash_kernel(n_rep, tk, q_ref, qn_ref, k_ref, v_ref, o_ref, vx_ref, kt_ref, s_ref, m_ref, l_ref, acc_ref):
    hd = LANES
    nk = k_ref.shape[0] // tk
    first = pl.program_id(2) == 0
    ring = nk >= 3 and nk % 2 == 1

    def rows(j):
        start = j * tk
        return pl.ds(start if isinstance(start, int) else pl.multiple_of(start, tk), tk)

    def scores(qsrc, j, slot):
        kt = kt_ref[:, rows(j)]
        for h in range(n_rep):
            s_ref[slot, h] = _dot(qsrc[:, h * hd:(h + 1) * hd], kt)

    def step(j, rd, wr=None, qsrc=None, jn=None):
        vx = vx_ref[rows(j), :]
        if wr is not None:
            kt_next = kt_ref[:, rows(jn)]
        for h in range(n_rep):
            if wr is not None:
                s_ref[wr, h] = _dot(qsrc[:, h * hd:(h + 1) * hd], kt_next)
            s = s_ref[rd, h]
            m_prev = m_ref[h]
            m_next = jnp.maximum(m_prev, jnp.max(s, axis=1, keepdims=True))
            alpha = jnp.exp2(m_prev - m_next)
            p = jnp.exp2(s - m_next[:, :1]).astype(jnp.bfloat16)
            pv = _dot(p, vx)
            m_ref[h] = m_next
            l_ref[h] = alpha * l_ref[h] + pv[:, hd:]
            acc_ref[h] = alpha * acc_ref[h] + pv[:, :hd]

    @pl.when(first)
    def _():
        vx_ref[:, :hd] = v_ref[...]
        vx_ref[:, hd:] = jnp.ones((vx_ref.shape[0], hd), vx_ref.dtype)
        for c in range(nk):
            kt_ref[:, rows(c)] = k_ref[rows(c), :].astype(jnp.float32).T.astype(kt_ref.dtype)

    m_ref[...] = jnp.full(m_ref.shape, NEG_INF, jnp.float32)
    l_ref[...] = jnp.zeros(l_ref.shape, jnp.float32)
    acc_ref[...] = jnp.zeros(acc_ref.shape, jnp.float32)

    if ring:
        @pl.when(first)
        def _():
            scores(q_ref, 0, 2)

        step(0, 2, 1, q_ref, 1)

        def body(t, _):
            step(2 * t + 1, 1, 0, q_ref, 2 * t + 2)
            step(2 * t + 2, 0, 1, q_ref, 2 * t + 3)
            return 0

        lax.fori_loop(0, (nk - 3) // 2, body, 0)
        step(nk - 2, 1, 0, q_ref, nk - 1)
        step(nk - 1, 0, 2, qn_ref, 0)
    else:
        scores(q_ref, 0, 0)
        for j in range(nk):
            step(j, j % 2, *((1 - j % 2, q_ref, j + 1) if j + 1 < nk else ()))
    for h in range(n_rep):
        o_ref[:, h * hd:(h + 1) * hd] = (acc_ref[h] / l_ref[h]).astype(o_ref.dtype)


def _flash(lay, q, k, v, n_kv, n_rep, tq, tk, q_blk0, nq, kb, k_blk0, name):
    B = lay.B
    qw = n_rep * LANES
    return pl.pallas_call(
        functools.partial(_flash_kernel, n_rep, tk),
        out_shape=jax.ShapeDtypeStruct((B, nq * tq, n_kv * qw), jnp.bfloat16),
        grid=(B, n_kv, nq),
        in_specs=[pl.BlockSpec((None, tq, qw), lambda b, g, i: (b, q_blk0 + i, g)),
                  pl.BlockSpec((None, tq, qw), lambda b, g, i: (b, q_blk0 + jnp.minimum(i + 1, nq - 1), g)),
                  pl.BlockSpec((None, kb, LANES), lambda b, g, i: (b, k_blk0, g)),
                  pl.BlockSpec((None, kb, LANES), lambda b, g, i: (b, k_blk0, g))],
        out_specs=pl.BlockSpec((None, tq, qw), lambda b, g, i: (b, i, g)),
        scratch_shapes=[pltpu.VMEM((kb, 2 * LANES), jnp.bfloat16),
                        pltpu.VMEM((LANES, kb), jnp.bfloat16),
                        pltpu.VMEM((3, n_rep, tq, tk), jnp.float32),
                        pltpu.VMEM((n_rep, tq, LANES), jnp.float32),
                        pltpu.VMEM((n_rep, tq, LANES), jnp.float32),
                        pltpu.VMEM((n_rep, tq, LANES), jnp.float32)],
        compiler_params=_cparams(("arbitrary", "arbitrary", "arbitrary")),
        name=name,
    )(q, q, k, v)


def _swa_kernel(lay, tq, sink_ref, q_ref, kp_ref, kc_ref, kn_ref, kx_ref, vp_ref, vc_ref, vn_ref, vx_ref, o_ref,
                q_scr, p_scr):
    pair = pl.program_id(1)
    qb = pl.program_id(2)
    hd = LANES // 2
    n_rep = SWA_HEADS // SWA_KV_HEADS
    W = SWA_WINDOW
    kk = jnp.concatenate([kp_ref[...], kc_ref[...], kn_ref[...], kx_ref[...]], axis=0)
    vv = jnp.concatenate([vp_ref[...], vc_ref[...], vn_ref[...], vx_ref[...]], axis=0)
    nkeys = kk.shape[0]
    nwin = tq + 2 * W
    lane = lax.broadcasted_iota(jnp.int32, (nkeys, LANES), 1)
    low = lane < hd
    kk_sw = pltpu.roll(kk.astype(jnp.float32), hd, 1).astype(kk.dtype)
    vv_sw = pltpu.roll(vv.astype(jnp.float32), hd, 1).astype(vv.dtype)
    k_dup = (jnp.where(low, kk, kk_sw), jnp.where(low, kk_sw, kk))
    v_dup = (jnp.where(low, vv, vv_sw), jnp.where(low, vv_sw, vv))

    start = qb * tq
    qpos = start + lax.broadcasted_iota(jnp.int32, (tq, nkeys), 0)
    col = lax.broadcasted_iota(jnp.int32, (tq, nkeys), 1)
    kpos = start - W + col
    in_win = jnp.logical_and(jnp.abs(qpos - kpos) <= W, jnp.logical_and(kpos >= 0, kpos < lay.S))
    valid = jnp.logical_or(col >= nwin, in_win)

    qlane = lax.broadcasted_iota(jnp.int32, (tq, LANES), 1)
    qlow = qlane < hd
    for kvh in range(2):
        for j in range(n_rep):
            g = (kvh * n_rep + j) // 2
            qg = q_ref[:, g * LANES:(g + 1) * LANES]
            q_scr[j * tq:(j + 1) * tq, :] = jnp.where(qlow if j % 2 == 0 else jnp.logical_not(qlow),
                                                     qg, jnp.zeros_like(qg))
        s_all = lax.dot_general(q_scr[...], k_dup[kvh], (((1,), (1,)), ((), ())),
                                preferred_element_type=jnp.float32)
        denom = []
        for j in range(n_rep):
            sink = sink_ref[(pair * 2 + kvh) * n_rep + j]
            s = jnp.where(valid, s_all[j * tq:(j + 1) * tq, :], NEG_INF)
            m = jnp.maximum(jnp.max(s, axis=1, keepdims=True), sink)
            p = jnp.exp2(s - m)
            denom.append(jnp.sum(p, axis=1, keepdims=True) + jnp.exp2(sink - m))
            p_scr[j * tq:(j + 1) * tq, :] = p.astype(p_scr.dtype)
        pv = _dot(p_scr[...], v_dup[kvh])
        for jj in range(n_rep // 2):
            even = pv[(2 * jj) * tq:(2 * jj + 1) * tq, :] / denom[2 * jj]
            odd = pv[(2 * jj + 1) * tq:(2 * jj + 2) * tq, :] / denom[2 * jj + 1]
            g = (kvh * n_rep) // 2 + jj
            o_ref[:, g * LANES:(g + 1) * LANES] = jnp.where(qlow, even, odd).astype(o_ref.dtype)


def _swa(lay, q, k, v, sink2):
    B, S = lay.B, lay.S
    tq = TM
    W = SWA_WINDOW
    per = tq // W
    nq = S // tq
    n_rep = SWA_HEADS // SWA_KV_HEADS
    qw = 2 * n_rep * (LANES // 2)
    ctx_blk = S // lay.C
    kv_specs = [pl.BlockSpec((None, W, LANES), lambda b, p, i: (b, jnp.maximum(i * per - 1, 0), p)),
                pl.BlockSpec((None, tq, LANES), lambda b, p, i: (b, i, p)),
                pl.BlockSpec((None, W, LANES), lambda b, p, i: (b, (i + 1) * per, p)),
                pl.BlockSpec((None, lay.C, LANES), lambda b, p, i: (b, ctx_blk, p))]
    return pl.pallas_call(
        functools.partial(_swa_kernel, lay, tq),
        out_shape=jax.ShapeDtypeStruct((B, S, lay.D), jnp.bfloat16),
        grid=(B, SWA_KV_HEADS // 2, nq),
        in_specs=[pl.BlockSpec(memory_space=pltpu.SMEM),
                  pl.BlockSpec((None, tq, qw), lambda b, p, i: (b, i, p))] + kv_specs + kv_specs,
        out_specs=pl.BlockSpec((None, tq, qw), lambda b, p, i: (b, i, p)),
        scratch_shapes=[pltpu.VMEM((n_rep * tq, LANES), jnp.bfloat16),
                        pltpu.VMEM((n_rep * tq, tq + 2 * W + lay.C), jnp.bfloat16)],
        compiler_params=_cparams(("arbitrary", "arbitrary", "arbitrary")),
        name="swa",
    )(sink2, q, k, k, k, k, v, v, v, v)


def _outproj_kernel(lay, has_ctx, ol_ref, oc_ref, wo_ref, *rest):
    x = ol_ref[...]
    if has_ctx:
        _, w = lay.split(pl.program_id(0))
        x = jnp.where(w >= lay.LT, oc_ref[...], x)
    _mixer_tail(_dot(x, wo_ref[...]), *rest)


def _out_proj(lay, o_lat, o_ctx, w_out, h, mod, gain, wr, br):
    D = lay.D
    tile = lambda i: i
    has_ctx = o_ctx is not None
    if not has_ctx:
        o_ctx = o_lat
    shapes, specs = _tail_out(lay, tile)
    lat_spec, ctx_spec = _lat_ctx_specs(lay)
    return pl.pallas_call(
        functools.partial(_outproj_kernel, lay, has_ctx),
        out_shape=shapes,
        grid=(lay.NT,),
        in_specs=[lat_spec, ctx_spec if has_ctx else lat_spec,
                  pl.BlockSpec((D, D), lambda i: (0, 0))] + _tail_in_specs(lay, tile),
        out_specs=specs,
        scratch_shapes=TAIL_SCRATCH,
        compiler_params=_cparams(("arbitrary",)),
        name="attn_out",
    )(o_lat, o_ctx, w_out, h, mod, gain.reshape(1, D), mod, mod, wr, br)


def _pool_kernel(lay, u_ref, up_ref, un_ref, pw_ref, ps_ref, *rest):
    i = pl.program_id(0)
    _, w = lay.split(i)
    has_prev, has_next = _seq_flags(lay, i)
    in_ctx = w >= lay.LT
    seq_len = jnp.where(in_ctx, lay.C, lay.S)
    pos0 = jnp.where(in_ctx, w - lay.LT, w) * TM
    G = len(POOL_WINDOWS)
    gw = lay.D // G
    E_ROWS = TM + 2 * SUBLANES_F32
    pos = pos0 + lax.broadcasted_iota(jnp.int32, (TM, gw), 0)
    ys = []
    for g, win in enumerate(POOL_WINDOWS):
        sl = slice(g * gw, (g + 1) * gw)
        u = u_ref[:, sl].astype(jnp.float32)
        before = jnp.where(has_prev, up_ref[HALO - SUBLANES_F32:HALO, sl].astype(jnp.float32), 0.0)
        after = jnp.where(has_next, un_ref[0:SUBLANES_F32, sl].astype(jnp.float32), 0.0)
        e = jnp.concatenate([before, u, after], axis=0)
        left = win // 2
        right = win - 1 - left
        assert left == right + 1 and left & (left - 1) == 0
        acc = e
        span = 1
        while span < left:
            acc = acc + pltpu.roll(acc, E_ROWS - span, 0)
            span *= 2
        tot = pltpu.roll(acc, left, 0) + acc
        total = tot[SUBLANES_F32:SUBLANES_F32 + TM]
        cnt = jnp.minimum(pos + right, seq_len - 1) - jnp.maximum(pos - left, 0) + 1
        mean = total / cnt.astype(jnp.float32)
        ys.append(_dot((mean - u).astype(jnp.bfloat16), pw_ref[g]))
    y = jnp.concatenate(ys, axis=1) * ps_ref[...]
    _mixer_tail(y, *rest)


def _pool(lay, u, pool_w, pool_scale, h, mod, gain, wr, br):
    D = lay.D
    tile = lambda i: i
    prev, nxt = _halo_specs(lay, D)
    shapes, specs = _tail_out(lay, tile)
    return pl.pallas_call(
        functools.partial(_pool_kernel, lay),
        out_shape=shapes,
        grid=(lay.NT,),
        in_specs=[pl.BlockSpec((TM, D), lambda i: (i, 0)), prev, nxt,
                  pl.BlockSpec(pool_w.shape, lambda i: (0, 0, 0)),
                  _row_spec(D)] + _tail_in_specs(lay, tile),
        out_specs=specs,
        scratch_shapes=TAIL_SCRATCH,
        compiler_params=_cparams(("arbitrary",)),
        name="pool",
    )(u, u, u, pool_w, pool_scale.reshape(1, D), h, mod, gain.reshape(1, D), mod, mod, wr, br)


def _route_tile(lg, info_ref, gate_ref, cnt_ref, carry_ref):
    @pl.when(pl.program_id(0) == 0)
    def _():
        carry_ref[...] = jnp.zeros(carry_ref.shape, jnp.float32)

    lane = lax.broadcasted_iota(jnp.int32, lg.shape, 1)
    lane_f = lane.astype(jnp.float32)
    big = jnp.float32(4 * LANES)

    def first_lane(mask):
        return jnp.min(jnp.where(mask, lane_f, big), axis=1, keepdims=True).astype(jnp.int32)

    is_grp = lane < N_GROUPS
    gl = jnp.where(is_grp, lg, NEG_INF)
    gmax = jnp.max(gl, axis=1, keepdims=True)
    grp = first_lane(jnp.logical_and(is_grp, gl == gmax))
    p_grp = 1.0 / jnp.sum(jnp.where(is_grp, jnp.exp(gl - gmax), 0.0), axis=1, keepdims=True)
    eid = lane - N_GROUPS
    in_grp = jnp.logical_and(lane >= N_GROUPS + grp * EXPERTS_PER_GROUP,
                             lane < N_GROUPS + (grp + 1) * EXPERTS_PER_GROUP)
    el = jnp.where(in_grp, lg, NEG_INF)
    t1 = jnp.max(el, axis=1, keepdims=True)
    e1 = first_lane(jnp.logical_and(in_grp, el == t1)) - N_GROUPS
    rest = jnp.logical_and(in_grp, eid != e1)
    el2 = jnp.where(rest, lg, NEG_INF)
    t2 = jnp.max(el2, axis=1, keepdims=True)
    e2 = first_lane(jnp.logical_and(rest, el2 == t2)) - N_GROUPS
    d = jnp.exp(t2 - t1)
    g1 = p_grp / (1.0 + d)
    g2 = p_grp * d / (1.0 + d)

    oh1 = lane == e1
    oh2 = lane == e2
    oh = jnp.where(jnp.logical_or(oh1, oh2), 1.0, 0.0)
    r = lax.broadcasted_iota(jnp.int32, (TM, TM), 0)
    c = lax.broadcasted_iota(jnp.int32, (TM, TM), 1)
    tri = jnp.where(c < r, 1.0, 0.0).astype(jnp.bfloat16)
    before = _dot(tri, oh.astype(jnp.bfloat16)) + carry_ref[0:1, :]
    r1 = jnp.sum(jnp.where(oh1, before, 0.0), axis=1, keepdims=True)
    r2 = jnp.sum(jnp.where(oh2, before, 0.0), axis=1, keepdims=True)
    carry = carry_ref[0:1, :] + jnp.sum(oh, axis=0, keepdims=True)
    carry_ref[...] = jnp.broadcast_to(carry, carry_ref.shape)

    info = jnp.where(lane == 0, e1, jnp.where(lane == 1, e2, jnp.where(
        lane == 2, r1.astype(jnp.int32), jnp.where(lane == 3, r2.astype(jnp.int32), 0))))
    info_ref[...] = info
    gate_ref[...] = jnp.where(lane == 0, g1, jnp.where(lane == 1, g2, 0.0))
    cnt_ref[...] = jnp.broadcast_to(carry, cnt_ref.shape).astype(jnp.int32)


def _dispatch_kernel(n_blocks, plan_ref, dest_ref, v_hbm, xs_hbm, zbuf, vbuf, lsem, rsem, zsem):
    i = pl.program_id(0)
    n = pl.num_programs(0)

    def load(tile, slot):
        row0 = tile * TM
        rows = pl.ds(row0 if isinstance(row0, int) else pl.multiple_of(row0, TM), TM)
        return pltpu.make_async_copy(v_hbm.at[rows], vbuf.at[slot], lsem.at[slot])

    def wait_rows(slot):
        for k in range(TOP_K):
            pltpu.make_async_copy(vbuf.at[slot], xs_hbm.at[pl.ds(0, TM)], rsem.at[slot]).wait()

    @pl.when(i == 0)
    def _():
        zbuf[...] = jnp.zeros(zbuf.shape, zbuf.dtype)

        def zero_block(row0):
            rows = pl.ds(pl.multiple_of(row0, EXPERT_ROWS), EXPERT_ROWS)
            return pltpu.make_async_copy(zbuf, xs_hbm.at[rows], zsem)

        def for_each_zero_block(fn):
            def seg(e, _):
                @pl.when(plan_ref[N_EXPERTS + e] > 0)
                def _():
                    fn(zero_block(plan_ref[e] - EXPERT_ROWS))
                return 0

            def tail(b, _):
                fn(zero_block(b * EXPERT_ROWS))
                return 0

            lax.fori_loop(0, N_EXPERTS, seg, 0)
            lax.fori_loop(plan_ref[2 * N_EXPERTS], n_blocks, tail, 0)

        for_each_zero_block(lambda cp: cp.start())
        for_each_zero_block(lambda cp: cp.wait())
        load(0, 0).start()

    def step(slot):
        free = (slot + 1) % DISPATCH_SLOTS

        @pl.when(i >= DISPATCH_SLOTS - 1)
        def _():
            wait_rows(free)

        @pl.when(i + 1 < n)
        def _():
            load(i + 1, free).start()

        load(i, slot).wait()
        for r in range(TM):
            for k in range(TOP_K):
                pltpu.make_async_copy(vbuf.at[slot, pl.ds(r, 1)], xs_hbm.at[pl.ds(dest_ref[0, k, r], 1)],
                                      rsem.at[slot]).start(priority=k % DMA_PRIORITIES)

        @pl.when(i == n - 1)
        def _():
            @pl.when(i >= 1)
            def _():
                wait_rows((slot + DISPATCH_SLOTS - 1) % DISPATCH_SLOTS)
            wait_rows(slot)

    for slot in range(DISPATCH_SLOTS):
        pl.when(i % DISPATCH_SLOTS == slot)(functools.partial(step, slot))


def _dispatch(lay, v, dest, plan, n_blocks):
    W = v.shape[1]
    return pl.pallas_call(
        functools.partial(_dispatch_kernel, n_blocks),
        out_shape=jax.ShapeDtypeStruct((n_blocks * EXPERT_ROWS, W), v.dtype),
        grid=(lay.NT,),
        in_specs=[pl.BlockSpec(memory_space=pltpu.SMEM),
                  pl.BlockSpec((1, TOP_K, TM), lambda i: (i, 0, 0), memory_space=pltpu.SMEM),
                  pl.BlockSpec(memory_space=pl.ANY)],
        out_specs=pl.BlockSpec(memory_space=pl.ANY),
        scratch_shapes=[pltpu.VMEM((EXPERT_ROWS, W), v.dtype),
                        pltpu.VMEM((DISPATCH_SLOTS, TM, W), v.dtype),
                        pltpu.SemaphoreType.DMA((DISPATCH_SLOTS,)),
                        pltpu.SemaphoreType.DMA((DISPATCH_SLOTS,)), pltpu.SemaphoreType.DMA(())],
        compiler_params=pltpu.CompilerParams(dimension_semantics=("arbitrary",), has_side_effects=True),
        name="moe_dispatch",
    )(plan, dest, v)


def _expert_kernel(layer, be_ref, na_ref, nxt_ref, slot_ref, x_ref, wg_hbm, wu_hbm, wd_hbm, y_ref,
                   wg_buf, wu_buf, wd_buf, wgu_s, wd_s, sem):
    i = pl.program_id(0)
    de = wg_buf.shape[2]
    active = i < na_ref[0]
    e = be_ref[i]

    def fetch(expert, slot):
        return (pltpu.make_async_copy(wg_hbm.at[layer, expert], wg_buf.at[slot], sem.at[slot, 0]),
                pltpu.make_async_copy(wu_hbm.at[layer, expert], wu_buf.at[slot], sem.at[slot, 1]),
                pltpu.make_async_copy(wd_hbm.at[layer, expert], wd_buf.at[slot], sem.at[slot, 2]))

    @pl.when(i == 0)
    def _():
        for cp in fetch(e, slot_ref[e]):
            cp.start()

    @pl.when(jnp.logical_and(active, jnp.logical_or(i == 0, e != be_ref[jnp.maximum(i - 1, 0)])))
    def _():
        slot = slot_ref[e]
        for cp in fetch(e, slot):
            cp.wait()

        @pl.when(nxt_ref[e] >= 0)
        def _():
            for cp in fetch(nxt_ref[e], 1 - slot):
                cp.start()

        wgu_s[:, :de] = wg_buf[slot].astype(wgu_s.dtype)
        wgu_s[:, de:] = wu_buf[slot].astype(wgu_s.dtype)
        wd_s[...] = wd_buf[slot].astype(wd_s.dtype)

    @pl.when(active)
    def _():
        hi, lo = _unpack_rows(x_ref[...])
        x = jnp.concatenate([hi.astype(jnp.bfloat16), lo.astype(jnp.bfloat16)], axis=1)
        hgu = _dot(x, wgu_s[...])
        hg, hu = hgu[:, :de], hgu[:, de:]
        a = (hg * (1.0 / (1.0 + jnp.exp(-hg)))) * hu
        y_ref[...] = _pack_rows(_dot(a.astype(jnp.bfloat16), wd_s[...]))

    @pl.when(jnp.logical_not(active))
    def _():
        y_ref[...] = jnp.zeros(y_ref.shape, y_ref.dtype)


def _experts(lay, xs, blk_expert, n_active, nxt, slot, layer, wg, wu, wd):
    D = lay.D
    P, W = xs.shape
    DE = wg.shape[3]
    row_block = pl.BlockSpec((EXPERT_ROWS, W), lambda i, *_: (i, 0))
    hbm = pl.BlockSpec(memory_space=pl.ANY)
    grid_spec = pltpu.PrefetchScalarGridSpec(
        num_scalar_prefetch=4,
        grid=(P // EXPERT_ROWS,),
        in_specs=[row_block, hbm, hbm, hbm],
        out_specs=row_block,
        scratch_shapes=[pltpu.VMEM((2, D, DE), jnp.float32), pltpu.VMEM((2, D, DE), jnp.float32),
                        pltpu.VMEM((2, DE, D), jnp.float32),
                        pltpu.VMEM((D, 2 * DE), jnp.bfloat16), pltpu.VMEM((DE, D), jnp.bfloat16),
                        pltpu.SemaphoreType.DMA((2, 3))],
    )
    return pl.pallas_call(
        functools.partial(_expert_kernel, layer),
        out_shape=jax.ShapeDtypeStruct((P, W), xs.dtype),
        grid_spec=grid_spec,
        compiler_params=_cparams(("arbitrary",)),
        name="moe_experts",
    )(blk_expert, n_active, nxt, slot, xs, wg, wu, wd)


def _combine_kernel(final, *refs):
    dests, refs = refs[:COMBINE_AHEAD + 1], refs[COMBINE_AHEAD + 1:]
    ys_hbm, gate_ref, h_ref, g2_ref, gain_ref, sc_ref, sh_ref, *out_refs, ybuf, sem = refs
    i = pl.program_id(0)
    n_slots = COMBINE_AHEAD + 1

    def gather(dref, buf_slot, r0, r1):
        for r in range(r0, r1):
            for k in range(TOP_K):
                pltpu.make_async_copy(ys_hbm.at[pl.ds(dref[0, k, r], 1)], ybuf.at[buf_slot, k, pl.ds(r, 1)],
                                      sem.at[buf_slot]).start(priority=k % DMA_PRIORITIES)

    def wait_tile(buf_slot):
        for k in range(TOP_K):
            pltpu.make_async_copy(ys_hbm.at[pl.ds(0, TM)], ybuf.at[buf_slot, k], sem.at[buf_slot]).wait()

    @pl.when(i == 0)
    def _():
        for a in range(COMBINE_AHEAD):
            gather(dests[a], a, 0, TM)

    def step(slot):
        wait_tile(slot)
        for r0 in range(0, TM, COMBINE_ROWS):
            rows = pl.ds(r0, COMBINE_ROWS)
            gate = gate_ref[rows, :]
            hi0, lo0 = _unpack_rows(ybuf[slot, 0, rows, :])
            hi1, lo1 = _unpack_rows(ybuf[slot, 1, rows, :])
            g0, g1 = gate[:, 0:1], gate[:, 1:2]
            f = jnp.concatenate([g0 * hi0 + g1 * hi1, g0 * lo0 + g1 * lo1], axis=1)
            h2 = h_ref[rows, :] + g2_ref[...] * f
            if final:
                (out_ref,) = out_refs
                out_ref[rows, :] = _rms(h2, gain_ref[...])
            else:
                h2_ref, u_ref = out_refs
                h2_ref[rows, :] = h2
                u_ref[rows, :] = (_rms(h2, gain_ref[...]) * (1.0 + sc_ref[...]) + sh_ref[...]).astype(u_ref.dtype)
            gather(dests[COMBINE_AHEAD], (slot + COMBINE_AHEAD) % n_slots, r0, r0 + COMBINE_ROWS)

        @pl.when(i == pl.num_programs(0) - 1)
        def _():
            for a in range(1, n_slots):
                wait_tile((slot + a) % n_slots)

    for slot in range(n_slots):
        pl.when(i % n_slots == slot)(functools.partial(step, slot))


def _combine(lay, ys, dest, gates, h1, mod, mod_next, gain_next, final):
    D = lay.D
    row_tile = pl.BlockSpec((TM, D), lambda i: (i, 0))
    if final:
        n = lay.B * lay.LT
        tile = lambda i: (i // lay.LT) * lay.TPB + i % lay.LT
        out_shape = (jax.ShapeDtypeStruct((lay.B * lay.S, D), jnp.float32),)
        out_specs = (row_tile,)
    else:
        n = lay.NT
        tile = lambda i: i
        out_shape = (jax.ShapeDtypeStruct((lay.T, D), jnp.float32),
                     jax.ShapeDtypeStruct((lay.T, D), jnp.bfloat16))
        out_specs = (row_tile, row_tile)
    return pl.pallas_call(
        functools.partial(_combine_kernel, final),
        out_shape=out_shape,
        grid=(n,),
        in_specs=[pl.BlockSpec((1, TOP_K, TM), lambda i, a=a: (tile(jnp.minimum(i + a, n - 1)), 0, 0),
                               memory_space=pltpu.SMEM) for a in range(COMBINE_AHEAD + 1)] + [
                  pl.BlockSpec(memory_space=pl.ANY),
                  pl.BlockSpec((TM, LANES), lambda i: (tile(i), 0)),
                  pl.BlockSpec((TM, D), lambda i: (tile(i), 0)),
                  _mod_spec(lay, 5, tile),
                  _row_spec(D),
                  _mod_spec(lay, 1, tile), _mod_spec(lay, 0, tile)],
        out_specs=out_specs,
        scratch_shapes=[pltpu.VMEM((COMBINE_AHEAD + 1, TOP_K, TM, ys.shape[1]), ys.dtype),
                        pltpu.SemaphoreType.DMA((COMBINE_AHEAD + 1,))],
        compiler_params=_cparams(("arbitrary",)),
        name="moe_combine_final" if final else "moe_combine",
    )(*([dest] * (COMBINE_AHEAD + 1)), ys, gates, h1, mod, gain_next.reshape(1, D), mod_next, mod_next)


def _moe(lay, v, info, gates, cnt, h1, mod, mod_next, gain_next, layer, wg, wu, wd, final):
    T, NT = lay.T, lay.NT
    counts = cnt[0, :N_EXPERTS]
    padded = ((counts + EXPERT_ROWS - 1) // EXPERT_ROWS) * EXPERT_ROWS
    pad_end = jnp.cumsum(padded)
    pad_start = pad_end - padded
    expert = info[:, :TOP_K]
    onehot = expert[:, :, None] == jnp.arange(N_EXPERTS, dtype=jnp.int32)
    dest = info[:, TOP_K:2 * TOP_K] + jnp.sum(jnp.where(onehot, pad_start, 0), axis=-1)
    dest = dest.reshape(NT, TM, TOP_K).transpose(0, 2, 1).astype(jnp.int32)
    n_blocks = -(-(T * TOP_K + N_EXPERTS * (EXPERT_ROWS - 1)) // EXPERT_ROWS)
    blk_start = jnp.arange(n_blocks, dtype=jnp.int32) * EXPERT_ROWS
    blk_expert = jnp.minimum(jnp.sum(pad_end[None, :] <= blk_start[:, None], axis=1),
                             N_EXPERTS - 1).astype(jnp.int32)
    n_active = (pad_end[-1:] // EXPERT_ROWS).astype(jnp.int32)
    plan = jnp.concatenate([pad_end, padded, n_active]).astype(jnp.int32)
    xs = _dispatch(lay, v, dest, plan, n_blocks)
    owns = padded > 0
    ids = jnp.arange(N_EXPERTS, dtype=jnp.int32)
    later = jnp.where(owns[None, :] & (ids[None, :] > ids[:, None]), ids[None, :], N_EXPERTS)
    nxt = jnp.min(later, axis=1)
    nxt = jnp.where(nxt == N_EXPERTS, -1, nxt).astype(jnp.int32)
    slot = ((jnp.cumsum(owns.astype(jnp.int32)) - 1) % 2).astype(jnp.int32)
    ys = _experts(lay, xs, blk_expert, n_active, nxt, slot, layer, wg, wu, wd)
    return _combine(lay, ys, dest, gates, h1, mod, mod_next, gain_next, final)


def kernel(x, c, ctx, c_ctx, ada_w, ada_b, norm_mix, norm_ffn, norm_final, conv_in, conv_k, conv_out, gqa_qkv, gqa_q_gain, gqa_k_gain, gqa_out, pool_w, pool_scale, swa_qkv, swa_sink, swa_out, router_grp_w, router_grp_b, router_exp_w, router_exp_b, exp_gate, exp_up, exp_down):
    B, S, D = x.shape
    C = ctx.shape[1]
    L = ada_w.shape[0]
    lay = Layout(B, S, C, D)
    bf = jnp.bfloat16

    R = SUBLANES_BF16
    cvec = jnp.zeros((R, D), jnp.float32).at[:B].set(c).at[B].set(c_ctx)
    mod_all = _modulation(cvec, ada_w, ada_b)[:, :B + 1].reshape(L, B + 1, 6, 1, D).transpose(0, 2, 1, 3, 4)

    pad = LANES - N_GROUPS - N_EXPERTS
    wr_all = jnp.concatenate([router_grp_w, router_exp_w, jnp.zeros((L, D, pad), jnp.float32)], axis=-1).astype(bf)
    br_all = jnp.concatenate([router_grp_b, router_exp_b, jnp.zeros((L, pad), jnp.float32)], axis=-1)

    h, u = _prenorm(lay, x, ctx, norm_mix[0], mod_all[0])
    one = jnp.ones((1, LANES), jnp.float32)

    for i in range(L):
        m, j = i % 4, i // 4
        mod = mod_all[i]
        wr, br = wr_all[i], br_all[i].reshape(1, LANES)
        tail = (h, mod, norm_ffn[i], wr, br)
        if m == 0:
            bg, z = _conv_in(lay, u, conv_in[j].astype(bf))
            routed = _conv_out(lay, z, bg, conv_k[j], conv_out[j].astype(bf), *tail)
        elif m == 1:
            hd = D // GQA_HEADS
            w = gqa_qkv[j].astype(bf)
            cos_t, sin_t = _rope_tables(lay, hd)
            nq, nk = GQA_HEADS * hd, GQA_KV_HEADS * hd
            qg = gqa_q_gain[j].reshape(1, hd)
            kg = gqa_k_gain[j].reshape(1, hd)
            q, k, vv = _qkv_project(lay, u, w, nq, nk, hd, qg, kg, cos_t, sin_t, True, hd ** -0.5 * LOG2E, "gqa_qkv")
            q3, k3, v3 = (a.reshape(B, lay.SB, -1) for a in (q, k, vv))
            n_rep = GQA_HEADS // GQA_KV_HEADS
            tq = 512 if S % 512 == 0 else TM
            tk = 768 if lay.SB % 768 == 0 else TM
            o_lat = _flash(lay, q3, k3, v3, GQA_KV_HEADS, n_rep, tq, tk, 0, S // tq, lay.SB, 0, "gqa_flash")
            o_ctx = _flash(lay, q3, k3, v3, GQA_KV_HEADS, n_rep, C, C, S // C, 1, C, S // C, "gqa_flash_ctx")
            routed = _out_proj(lay, o_lat, o_ctx, gqa_out[j].astype(bf), *tail)
        elif m == 2:
            routed = _pool(lay, u, pool_w[j].astype(bf), pool_scale[j], *tail)
        else:
            hd = D // SWA_HEADS
            w = swa_qkv[j].astype(bf)
            cos_t, sin_t = _rope_tables(lay, hd)
            nq, nk = SWA_HEADS * hd, SWA_KV_HEADS * hd
            q, k, vv = _qkv_project(lay, u, w, nq, nk, hd, one, one, cos_t, sin_t, False, hd ** -0.5 * LOG2E, "swa_qkv")
            q3, k3, v3 = (a.reshape(B, lay.SB, -1) for a in (q, k, vv))
            o_lat = _swa(lay, q3, k3, v3, swa_sink[j] * LOG2E)
            routed = _out_proj(lay, o_lat, None, swa_out[j].astype(bf), *tail)
        final = i == L - 1
        mod_next = mod if final else mod_all[i + 1]
        gain_next = norm_final if final else norm_mix[i + 1]
        h1, v, info, gates, cnt = routed
        res = _moe(lay, v, info, gates, cnt, h1, mod, mod_next, gain_next, i, exp_gate, exp_up, exp_down, final)
        if final:
            return res[0].reshape(B, S, D)
        h, u = res
```

```python
import functools

import jax
import jax.numpy as jnp
from jax import lax
from jax.experimental import pallas as pl
from jax.experimental.pallas import tpu as pltpu

GRID_W = 64
NORM_EPS = 1e-6
ROPE_THETA = 10000.0
NEG_INF = -1e30
GQA_HEADS, GQA_KV_HEADS = 16, 4
SWA_HEADS, SWA_KV_HEADS, SWA_WINDOW = 32, 8, 128
POOL_WINDOWS = (2, 4, 8, 16)
N_GROUPS, EXPERTS_PER_GROUP, TOP_K = 8, 4, 2
N_EXPERTS = N_GROUPS * EXPERTS_PER_GROUP
LOG2E = 1.4426950408889634

LANES = 128
SUBLANES_F32 = 8
SUBLANES_BF16 = 16
VMEM_LIMIT = 56 * 1024 * 1024
TM = 256
EXPERT_ROWS = 256
COMBINE_ROWS = 32
COMBINE_AHEAD = 2
DMA_PRIORITIES = 2
DISPATCH_SLOTS = 3
HALO = SUBLANES_BF16


def _cparams(sem):
    return pltpu.CompilerParams(dimension_semantics=sem, vmem_limit_bytes=VMEM_LIMIT)


class Layout:
    def __init__(self, B, S, C, D):
        assert S % TM == 0 and C % TM == 0
        self.B, self.S, self.C, self.D = B, S, C, D
        self.SB = S + C
        self.T = B * self.SB
        self.LT = S // TM
        self.TPB = self.SB // TM
        self.NT = B * self.TPB

    def split(self, i):
        return i // self.TPB, i % self.TPB

    def mod_row(self, i):
        b, w = self.split(i)
        return jnp.where(w >= self.LT, self.B, b)


def _mod_spec(lay, chunk, tile_of):
    return pl.BlockSpec((None, None, 1, lay.D),
                        lambda *g: (chunk, lay.mod_row(tile_of(*g)), 0, 0))


def _row_spec(D):
    return pl.BlockSpec((1, D), lambda *g: (0, 0))


def _rms(x, gain):
    ms = jnp.mean(x * x, axis=-1, keepdims=True)
    return (x * lax.rsqrt(ms + NORM_EPS)) * gain


def _dot(a, b):
    return jnp.dot(a, b, preferred_element_type=jnp.float32)


def _mod_kernel(c_ref, w_ref, b_ref, o_ref):
    c = c_ref[...]
    a = (c * (1.0 / (1.0 + jnp.exp(-c)))).astype(jnp.bfloat16)
    o_ref[...] = _dot(a, w_ref[...].astype(jnp.bfloat16)) + b_ref[...]


def _modulation(cvec, ada_w, ada_b):
    L, D, N = ada_w.shape
    R = cvec.shape[0]
    tn = 512
    return pl.pallas_call(
        _mod_kernel,
        out_shape=jax.ShapeDtypeStruct((L, R, N), jnp.float32),
        grid=(L, N // tn),
        in_specs=[pl.BlockSpec((R, D), lambda l, j: (0, 0)),
                  pl.BlockSpec((None, D, tn), lambda l, j: (l, 0, j)),
                  pl.BlockSpec((None, 1, tn), lambda l, j: (l, 0, j))],
        out_specs=pl.BlockSpec((None, R, tn), lambda l, j: (l, 0, j)),
        compiler_params=_cparams(("arbitrary", "arbitrary")),
        name="adaln_mod",
    )(cvec, ada_w, ada_b.reshape(L, 1, N))


def _prenorm_kernel(lay, x_ref, ctx_ref, gain_ref, sc_ref, sh_ref, h_ref, u_ref):
    _, w = lay.split(pl.program_id(0))
    h = jnp.where(w >= lay.LT, ctx_ref[...], x_ref[...])
    h_ref[...] = h
    u_ref[...] = (_rms(h, gain_ref[...]) * (1.0 + sc_ref[...]) + sh_ref[...]).astype(u_ref.dtype)


def _lat_ctx_specs(lay):
    def lat_idx(i):
        b, w = lay.split(i)
        return b, jnp.minimum(w, lay.LT - 1), 0

    def ctx_idx(i):
        b, w = lay.split(i)
        return b, jnp.clip(w - lay.LT, 0, lay.TPB - lay.LT - 1), 0

    return pl.BlockSpec((None, TM, lay.D), lat_idx), pl.BlockSpec((None, TM, lay.D), ctx_idx)


def _prenorm(lay, x, ctx, gain, mod):
    D = lay.D
    tile = lambda i: i
    row = pl.BlockSpec((TM, D), lambda i: (i, 0))
    return pl.pallas_call(
        functools.partial(_prenorm_kernel, lay),
        out_shape=(jax.ShapeDtypeStruct((lay.T, D), jnp.float32),
                   jax.ShapeDtypeStruct((lay.T, D), jnp.bfloat16)),
        grid=(lay.NT,),
        in_specs=[*_lat_ctx_specs(lay), _row_spec(D), _mod_spec(lay, 1, tile), _mod_spec(lay, 0, tile)],
        out_specs=(row, row),
        compiler_params=_cparams(("arbitrary",)),
        name="prenorm",
    )(x, ctx, gain.reshape(1, D), mod, mod)


def _pack_rows(x):
    w = x.shape[1] // 2
    hi = pltpu.bitcast(x[:, :w].astype(jnp.bfloat16).astype(jnp.float32), jnp.uint32)
    lo = pltpu.bitcast(x[:, w:].astype(jnp.bfloat16).astype(jnp.float32), jnp.uint32)
    return hi | (lo >> 16)


def _unpack_rows(words):
    hi = pltpu.bitcast(words & jnp.uint32(0xFFFF0000), jnp.float32)
    lo = pltpu.bitcast(words << 16, jnp.float32)
    return hi, lo


def _mixer_tail(y, h_ref, g1_ref, gain_ref, sc_ref, sh_ref, wr_ref, br_ref,
                h1_ref, v_ref, info_ref, gate_ref, cnt_ref, carry_ref):
    h1 = h_ref[...] + g1_ref[...] * y
    h1_ref[...] = h1
    v = _rms(h1, gain_ref[...]) * (1.0 + sc_ref[...]) + sh_ref[...]
    v_ref[...] = _pack_rows(v)
    logits = _dot(v.astype(jnp.bfloat16), wr_ref[...]) + br_ref[...]
    _route_tile(logits, info_ref, gate_ref, cnt_ref, carry_ref)


def _tail_in_specs(lay, tile):
    D = lay.D
    return [pl.BlockSpec((TM, D), lambda *g: (tile(*g), 0)),
            _mod_spec(lay, 2, tile),
            _row_spec(D),
            _mod_spec(lay, 4, tile), _mod_spec(lay, 3, tile),
            pl.BlockSpec((D, LANES), lambda *g: (0, 0)),
            pl.BlockSpec((1, LANES), lambda *g: (0, 0))]


def _tail_out(lay, tile):
    D = lay.D
    shapes = (jax.ShapeDtypeStruct((lay.T, D), jnp.float32),
              jax.ShapeDtypeStruct((lay.T, D // 2), jnp.uint32),
              jax.ShapeDtypeStruct((lay.T, LANES), jnp.int32),
              jax.ShapeDtypeStruct((lay.T, LANES), jnp.float32),
              jax.ShapeDtypeStruct((SUBLANES_F32, LANES), jnp.int32))
    specs = (pl.BlockSpec((TM, D), lambda *g: (tile(*g), 0)),
             pl.BlockSpec((TM, D // 2), lambda *g: (tile(*g), 0)),
             pl.BlockSpec((TM, LANES), lambda *g: (tile(*g), 0)),
             pl.BlockSpec((TM, LANES), lambda *g: (tile(*g), 0)),
             pl.BlockSpec((SUBLANES_F32, LANES), lambda *g: (0, 0)))
    return shapes, specs


TAIL_SCRATCH = [pltpu.VMEM((SUBLANES_F32, LANES), jnp.float32)]


def _seq_flags(lay, i):
    _, w = lay.split(i)
    has_prev = jnp.logical_and(w != 0, w != lay.LT)
    has_next = jnp.logical_and(w != lay.LT - 1, w != lay.TPB - 1)
    return has_prev, has_next


def _halo_specs(lay, width, col=lambda *g: 0, tile=lambda i: i):
    per = TM // HALO
    last = lay.T // HALO - 1
    prev = pl.BlockSpec((HALO, width), lambda *g: (jnp.maximum(tile(*g) * per - 1, 0), col(*g)))
    nxt = pl.BlockSpec((HALO, width), lambda *g: (jnp.minimum((tile(*g) + 1) * per, last), col(*g)))
    return prev, nxt


def _convin_kernel(u_ref, wb_ref, wc_ref, wx_ref, bg_ref, z_ref):
    x = u_ref[...]
    bg_ref[...] = _dot(x, wb_ref[...]).astype(bg_ref.dtype)
    z_ref[...] = (_dot(x, wc_ref[...]) * _dot(x, wx_ref[...])).astype(z_ref.dtype)


def _conv_in(lay, u, w_in):
    D = lay.D
    tn = 1024
    nb = D // tn
    out = jax.ShapeDtypeStruct((lay.T, D), jnp.bfloat16)
    return pl.pallas_call(
        _convin_kernel,
        out_shape=(out, out),
        grid=(nb, lay.NT),
        in_specs=[pl.BlockSpec((TM, D), lambda j, i: (i, 0)),
                  pl.BlockSpec((D, tn), lambda j, i: (0, j)),
                  pl.BlockSpec((D, tn), lambda j, i: (0, nb + j)),
                  pl.BlockSpec((D, tn), lambda j, i: (0, 2 * nb + j))],
        out_specs=(pl.BlockSpec((TM, tn), lambda j, i: (i, j)),
                   pl.BlockSpec((TM, tn), lambda j, i: (i, j))),
        compiler_params=_cparams(("arbitrary", "arbitrary")),
        name="conv_in",
    )(u, w_in, w_in, w_in)


def _convout_kernel(lay, z_ref, zp_ref, zn_ref, bg_ref, ck_ref, wo_ref, *rest):
    i = pl.program_id(0)
    has_prev, has_next = _seq_flags(lay, i)
    z = z_ref[...].astype(jnp.float32)
    row = lax.broadcasted_iota(jnp.int32, z.shape, 0)
    prev_row = jnp.where(has_prev, zp_ref[HALO - 1:HALO, :].astype(jnp.float32), 0.0)
    next_row = jnp.where(has_next, zn_ref[0:1, :].astype(jnp.float32), 0.0)
    z_m1 = jnp.where(row == 0, prev_row, pltpu.roll(z, 1, 0))
    z_p1 = jnp.where(row == TM - 1, next_row, pltpu.roll(z, TM - 1, 0))
    y = z_m1 * ck_ref[0:1, :] + z * ck_ref[1:2, :] + z_p1 * ck_ref[2:3, :]
    g = (bg_ref[...].astype(jnp.float32) * y).astype(jnp.bfloat16)
    _mixer_tail(_dot(g, wo_ref[...]), *rest)


def _conv_out(lay, z, bg, conv_k, w_out, h, mod, gain, wr, br):
    D = lay.D
    tile = lambda i: i
    prev, nxt = _halo_specs(lay, D)
    shapes, specs = _tail_out(lay, tile)
    return pl.pallas_call(
        functools.partial(_convout_kernel, lay),
        out_shape=shapes,
        grid=(lay.NT,),
        in_specs=[pl.BlockSpec((TM, D), lambda i: (i, 0)), prev, nxt,
                  pl.BlockSpec((TM, D), lambda i: (i, 0)),
                  pl.BlockSpec(conv_k.shape, lambda i: (0, 0)),
                  pl.BlockSpec((D, D), lambda i: (0, 0))] + _tail_in_specs(lay, tile),
        out_specs=specs,
        scratch_shapes=TAIL_SCRATCH,
        compiler_params=_cparams(("arbitrary",)),
        name="conv_out",
    )(z, z, z, bg, conv_k, w_out, h, mod, gain.reshape(1, D), mod, mod, wr, br)


def _qkv_kernel(head_dim, use_norm, q_scale, u_ref, w_ref, qg_ref, kg_ref, cos_ref, sin_ref, q_ref, k_ref, v_ref):
    y = _dot(u_ref[...], w_ref[...])
    nq, nk = q_ref.shape[1], k_ref.shape[1]
    lane = lax.broadcasted_iota(jnp.int32, (TM, LANES), 1)

    def rotary(yg, gain_ref):
        if use_norm:
            yg = _rms(yg, gain_ref[...])
        if head_dim == LANES:
            rot = pltpu.roll(yg, LANES // 2, 1)
        else:
            q = head_dim // 2
            rot = jnp.where(lane % head_dim < q, pltpu.roll(yg, LANES - q, 1), pltpu.roll(yg, q, 1))
        return yg * cos_ref[...] + rot * sin_ref[...]

    for g in range(nq // LANES):
        cols = slice(g * LANES, (g + 1) * LANES)
        q_ref[:, cols] = (rotary(y[:, cols], qg_ref) * q_scale).astype(q_ref.dtype)
    for g in range(nk // LANES):
        cols = slice(g * LANES, (g + 1) * LANES)
        k_ref[:, cols] = rotary(y[:, nq + g * LANES:nq + (g + 1) * LANES], kg_ref).astype(k_ref.dtype)
    v_ref[...] = y[:, nq + nk:].astype(v_ref.dtype)


def _qkv_project(lay, u, w, nq, nk, head_dim, q_gain, k_gain, cos_t, sin_t, use_norm, q_scale, name):
    D = lay.D
    row = lambda n: pl.BlockSpec((TM, n), lambda i: (i, 0))
    table = pl.BlockSpec((TM, LANES), lambda i: (i % lay.TPB, 0))
    gain = pl.BlockSpec((1, LANES), lambda i: (0, 0))
    bf = jnp.bfloat16
    return pl.pallas_call(
        functools.partial(_qkv_kernel, head_dim, use_norm, q_scale),
        out_shape=(jax.ShapeDtypeStruct((lay.T, nq), bf), jax.ShapeDtypeStruct((lay.T, nk), bf),
                   jax.ShapeDtypeStruct((lay.T, nk), bf)),
        grid=(lay.NT,),
        in_specs=[row(D), pl.BlockSpec((D, nq + 2 * nk), lambda i: (0, 0)), gain, gain, table, table],
        out_specs=(row(nq), row(nk), row(nk)),
        compiler_params=_cparams(("arbitrary",)),
        name=name,
    )(u, w, q_gain, k_gain, cos_t, sin_t)


def _rope_tables(lay, head_dim):
    quarter = head_dim // 4
    rows = lay.S // GRID_W
    row = jnp.repeat(jnp.arange(rows), GRID_W).astype(jnp.float32)
    col = jnp.tile(jnp.arange(GRID_W), rows).astype(jnp.float32)
    inv = ROPE_THETA ** (-jnp.arange(quarter, dtype=jnp.float32) / quarter)
    ang = jnp.concatenate([row[:, None] * inv, col[:, None] * inv], axis=-1)
    cos, sin = jnp.cos(ang), jnp.sin(ang)
    reps = LANES // head_dim
    cos_t = jnp.tile(jnp.concatenate([cos, cos], axis=-1), (1, reps))
    sin_t = jnp.tile(jnp.concatenate([-sin, sin], axis=-1), (1, reps))
    cos_t = jnp.concatenate([cos_t, jnp.ones((lay.C, LANES), jnp.float32)], axis=0)
    sin_t = jnp.concatenate([sin_t, jnp.zeros((lay.C, LANES), jnp.float32)], axis=0)
    return cos_t, sin_t


def _flash_kernel(n_rep, tk, q_ref, qn_ref, k_ref, v_ref, o_ref, vx_ref, kt_ref, qs_ref, qns_ref, s_ref,
                  m_ref, l_ref, acc_ref):
    hd = LANES
    nk = k_ref.shape[0] // tk
    first = pl.program_id(2) == 0
    ring = nk >= 3 and nk % 2 == 1

    def rows(j):
        start = j * tk
        return pl.ds(start if isinstance(start, int) else pl.multiple_of(start, tk), tk)

    tq = q_ref.shape[0]

    def stack_heads(src, dst):
        for h in range(n_rep):
            dst[h * tq:(h + 1) * tq, :] = src[:, h * hd:(h + 1) * hd]

    def scores(qstack, j, slot):
        s_ref[slot] = _dot(qstack[...], kt_ref[:, rows(j)])

    def step(j, rd, wr=None, qstack=None, jn=None):
        vx = vx_ref[rows(j), :]
        if wr is not None:
            scores(qstack, jn, wr)
        for h in range(n_rep):
            s = s_ref[rd, h * tq:(h + 1) * tq, :]
            m_prev = m_ref[h]
            m_next = jnp.maximum(m_prev, jnp.max(s, axis=1, keepdims=True))
            alpha = jnp.exp2(m_prev - m_next)
            p = jnp.exp2(s - m_next[:, :1]).astype(jnp.bfloat16)
            pv = _dot(p, vx)
            m_ref[h] = m_next
            l_ref[h] = alpha * l_ref[h] + pv[:, hd:]
            acc_ref[h] = alpha * acc_ref[h] + pv[:, :hd]

    @pl.when(first)
    def _():
        vx_ref[:, :hd] = v_ref[...]
        vx_ref[:, hd:] = jnp.ones((vx_ref.shape[0], hd), vx_ref.dtype)
        for c in range(nk):
            kt_ref[:, rows(c)] = k_ref[rows(c), :].astype(jnp.float32).T.astype(kt_ref.dtype)

    m_ref[...] = jnp.full(m_ref.shape, NEG_INF, jnp.float32)
    l_ref[...] = jnp.zeros(l_ref.shape, jnp.float32)
    acc_ref[...] = jnp.zeros(acc_ref.shape, jnp.float32)

    stack_heads(q_ref, qs_ref)
    if ring:
        stack_heads(qn_ref, qns_ref)

        @pl.when(first)
        def _():
            scores(qs_ref, 0, 2)

        step(0, 2, 1, qs_ref, 1)

        def body(t, _):
            step(2 * t + 1, 1, 0, qs_ref, 2 * t + 2)
            step(2 * t + 2, 0, 1, qs_ref, 2 * t + 3)
            return 0

        lax.fori_loop(0, (nk - 3) // 2, body, 0)
        step(nk - 2, 1, 0, qs_ref, nk - 1)
        step(nk - 1, 0, 2, qns_ref, 0)
    else:
        scores(qs_ref, 0, 0)
        for j in range(nk):
            step(j, j % 2, *((1 - j % 2, qs_ref, j + 1) if j + 1 < nk else ()))
    for h in range(n_rep):
        o_ref[:, h * hd:(h + 1) * hd] = (acc_ref[h] / l_ref[h]).astype(o_ref.dtype)


def _flash(lay, q, k, v, n_kv, n_rep, tq, tk, q_blk0, nq, kb, k_blk0, name):
    B = lay.B
    qw = n_rep * LANES
    return pl.pallas_call(
        functools.partial(_flash_kernel, n_rep, tk),
        out_shape=jax.ShapeDtypeStruct((B, nq * tq, n_kv * qw), jnp.bfloat16),
        grid=(B, n_kv, nq),
        in_specs=[pl.BlockSpec((None, tq, qw), lambda b, g, i: (b, q_blk0 + i, g)),
                  pl.BlockSpec((None, tq, qw), lambda b, g, i: (b, q_blk0 + jnp.minimum(i + 1, nq - 1), g)),
                  pl.BlockSpec((None, kb, LANES), lambda b, g, i: (b, k_blk0, g)),
                  pl.BlockSpec((None, kb, LANES), lambda b, g, i: (b, k_blk0, g))],
        out_specs=pl.BlockSpec((None, tq, qw), lambda b, g, i: (b, i, g)),
        scratch_shapes=[pltpu.VMEM((kb, 2 * LANES), jnp.bfloat16),
                        pltpu.VMEM((LANES, kb), jnp.bfloat16),
                        pltpu.VMEM((n_rep * tq, LANES), jnp.bfloat16),
                        pltpu.VMEM((n_rep * tq, LANES), jnp.bfloat16),
                        pltpu.VMEM((3, n_rep * tq, tk), jnp.float32),
                        pltpu.VMEM((n_rep, tq, LANES), jnp.float32),
                        pltpu.VMEM((n_rep, tq, LANES), jnp.float32),
                        pltpu.VMEM((n_rep, tq, LANES), jnp.float32)],
        compiler_params=_cparams(("arbitrary", "arbitrary", "arbitrary")),
        name=name,
    )(q, q, k, v)


def _swa_kernel(lay, tq, sink_ref, q_ref, kp_ref, kc_ref, kn_ref, kx_ref, vp_ref, vc_ref, vn_ref, vx_ref, o_ref,
                q_scr, p_scr):
    pair = pl.program_id(1)
    qb = pl.program_id(2)
    hd = LANES // 2
    n_rep = SWA_HEADS // SWA_KV_HEADS
    W = SWA_WINDOW
    kk = jnp.concatenate([kp_ref[...], kc_ref[...], kn_ref[...], kx_ref[...]], axis=0)
    vv = jnp.concatenate([vp_ref[...], vc_ref[...], vn_ref[...], vx_ref[...]], axis=0)
    nkeys = kk.shape[0]
    nwin = tq + 2 * W
    lane = lax.broadcasted_iota(jnp.int32, (nkeys, LANES), 1)
    low = lane < hd
    kk_sw = pltpu.roll(kk.astype(jnp.float32), hd, 1).astype(kk.dtype)
    vv_sw = pltpu.roll(vv.astype(jnp.float32), hd, 1).astype(vv.dtype)
    k_dup = (jnp.where(low, kk, kk_sw), jnp.where(low, kk_sw, kk))
    v_dup = (jnp.where(low, vv, vv_sw), jnp.where(low, vv_sw, vv))

    start = qb * tq
    qpos = start + lax.broadcasted_iota(jnp.int32, (tq, nkeys), 0)
    col = lax.broadcasted_iota(jnp.int32, (tq, nkeys), 1)
    kpos = start - W + col
    in_win = jnp.logical_and(jnp.abs(qpos - kpos) <= W, jnp.logical_and(kpos >= 0, kpos < lay.S))
    valid = jnp.logical_or(col >= nwin, in_win)

    qlane = lax.broadcasted_iota(jnp.int32, (tq, LANES), 1)
    qlow = qlane < hd
    for kvh in range(2):
        for j in range(n_rep):
            g = (kvh * n_rep + j) // 2
            qg = q_ref[:, g * LANES:(g + 1) * LANES]
            q_scr[j * tq:(j + 1) * tq, :] = jnp.where(qlow if j % 2 == 0 else jnp.logical_not(qlow),
                                                     qg, jnp.zeros_like(qg))
        s_all = lax.dot_general(q_scr[...], k_dup[kvh], (((1,), (1,)), ((), ())),
                                preferred_element_type=jnp.float32)
        denom = []
        for j in range(n_rep):
            sink = sink_ref[(pair * 2 + kvh) * n_rep + j]
            s = jnp.where(valid, s_all[j * tq:(j + 1) * tq, :], NEG_INF)
            m = jnp.maximum(jnp.max(s, axis=1, keepdims=True), sink)
            p = jnp.exp2(s - m)
            denom.append(jnp.sum(p, axis=1, keepdims=True) + jnp.exp2(sink - m))
            p_scr[j * tq:(j + 1) * tq, :] = p.astype(p_scr.dtype)
        pv = _dot(p_scr[...], v_dup[kvh])
        for jj in range(n_rep // 2):
            even = pv[(2 * jj) * tq:(2 * jj + 1) * tq, :] / denom[2 * jj]
            odd = pv[(2 * jj + 1) * tq:(2 * jj + 2) * tq, :] / denom[2 * jj + 1]
            g = (kvh * n_rep) // 2 + jj
            o_ref[:, g * LANES:(g + 1) * LANES] = jnp.where(qlow, even, odd).astype(o_ref.dtype)


def _swa(lay, q, k, v, sink2):
    B, S = lay.B, lay.S
    tq = TM
    W = SWA_WINDOW
    per = tq // W
    nq = S // tq
    n_rep = SWA_HEADS // SWA_KV_HEADS
    qw = 2 * n_rep * (LANES // 2)
    ctx_blk = S // lay.C
    kv_specs = [pl.BlockSpec((None, W, LANES), lambda b, p, i: (b, jnp.maximum(i * per - 1, 0), p)),
                pl.BlockSpec((None, tq, LANES), lambda b, p, i: (b, i, p)),
                pl.BlockSpec((None, W, LANES), lambda b, p, i: (b, (i + 1) * per, p)),
                pl.BlockSpec((None, lay.C, LANES), lambda b, p, i: (b, ctx_blk, p))]
    return pl.pallas_call(
        functools.partial(_swa_kernel, lay, tq),
        out_shape=jax.ShapeDtypeStruct((B, S, lay.D), jnp.bfloat16),
        grid=(B, SWA_KV_HEADS // 2, nq),
        in_specs=[pl.BlockSpec(memory_space=pltpu.SMEM),
                  pl.BlockSpec((None, tq, qw), lambda b, p, i: (b, i, p))] + kv_specs + kv_specs,
        out_specs=pl.BlockSpec((None, tq, qw), lambda b, p, i: (b, i, p)),
        scratch_shapes=[pltpu.VMEM((n_rep * tq, LANES), jnp.bfloat16),
                        pltpu.VMEM((n_rep * tq, tq + 2 * W + lay.C), jnp.bfloat16)],
        compiler_params=_cparams(("arbitrary", "arbitrary", "arbitrary")),
        name="swa",
    )(sink2, q, k, k, k, k, v, v, v, v)


def _outproj_kernel(lay, has_ctx, ol_ref, oc_ref, wo_ref, *rest):
    x = ol_ref[...]
    if has_ctx:
        _, w = lay.split(pl.program_id(0))
        x = jnp.where(w >= lay.LT, oc_ref[...], x)
    _mixer_tail(_dot(x, wo_ref[...]), *rest)


def _out_proj(lay, o_lat, o_ctx, w_out, h, mod, gain, wr, br):
    D = lay.D
    tile = lambda i: i
    has_ctx = o_ctx is not None
    if not has_ctx:
        o_ctx = o_lat
    shapes, specs = _tail_out(lay, tile)
    lat_spec, ctx_spec = _lat_ctx_specs(lay)
    return pl.pallas_call(
        functools.partial(_outproj_kernel, lay, has_ctx),
        out_shape=shapes,
        grid=(lay.NT,),
        in_specs=[lat_spec, ctx_spec if has_ctx else lat_spec,
                  pl.BlockSpec((D, D), lambda i: (0, 0))] + _tail_in_specs(lay, tile),
        out_specs=specs,
        scratch_shapes=TAIL_SCRATCH,
        compiler_params=_cparams(("arbitrary",)),
        name="attn_out",
    )(o_lat, o_ctx, w_out, h, mod, gain.reshape(1, D), mod, mod, wr, br)


def _pool_kernel(lay, u_ref, up_ref, un_ref, pw_ref, ps_ref, *rest):
    i = pl.program_id(0)
    _, w = lay.split(i)
    has_prev, has_next = _seq_flags(lay, i)
    in_ctx = w >= lay.LT
    seq_len = jnp.where(in_ctx, lay.C, lay.S)
    pos0 = jnp.where(in_ctx, w - lay.LT, w) * TM
    G = len(POOL_WINDOWS)
    gw = lay.D // G
    E_ROWS = TM + 2 * SUBLANES_F32
    pos = pos0 + lax.broadcasted_iota(jnp.int32, (TM, gw), 0)
    ys = []
    for g, win in enumerate(POOL_WINDOWS):
        sl = slice(g * gw, (g + 1) * gw)
        u = u_ref[:, sl].astype(jnp.float32)
        before = jnp.where(has_prev, up_ref[HALO - SUBLANES_F32:HALO, sl].astype(jnp.float32), 0.0)
        after = jnp.where(has_next, un_ref[0:SUBLANES_F32, sl].astype(jnp.float32), 0.0)
        e = jnp.concatenate([before, u, after], axis=0)
        left = win // 2
        right = win - 1 - left
        assert left == right + 1 and left & (left - 1) == 0
        acc = e
        span = 1
        while span < left:
            acc = acc + pltpu.roll(acc, E_ROWS - span, 0)
            span *= 2
        tot = pltpu.roll(acc, left, 0) + acc
        total = tot[SUBLANES_F32:SUBLANES_F32 + TM]
        cnt = jnp.minimum(pos + right, seq_len - 1) - jnp.maximum(pos - left, 0) + 1
        mean = total / cnt.astype(jnp.float32)
        ys.append(_dot((mean - u).astype(jnp.bfloat16), pw_ref[g]))
    y = jnp.concatenate(ys, axis=1) * ps_ref[...]
    _mixer_tail(y, *rest)


def _pool(lay, u, pool_w, pool_scale, h, mod, gain, wr, br):
    D = lay.D
    tile = lambda i: i
    prev, nxt = _halo_specs(lay, D)
    shapes, specs = _tail_out(lay, tile)
    return pl.pallas_call(
        functools.partial(_pool_kernel, lay),
        out_shape=shapes,
        grid=(lay.NT,),
        in_specs=[pl.BlockSpec((TM, D), lambda i: (i, 0)), prev, nxt,
                  pl.BlockSpec(pool_w.shape, lambda i: (0, 0, 0)),
                  _row_spec(D)] + _tail_in_specs(lay, tile),
        out_specs=specs,
        scratch_shapes=TAIL_SCRATCH,
        compiler_params=_cparams(("arbitrary",)),
        name="pool",
    )(u, u, u, pool_w, pool_scale.reshape(1, D), h, mod, gain.reshape(1, D), mod, mod, wr, br)


def _route_tile(lg, info_ref, gate_ref, cnt_ref, carry_ref):
    @pl.when(pl.program_id(0) == 0)
    def _():
        carry_ref[...] = jnp.zeros(carry_ref.shape, jnp.float32)

    lane = lax.broadcasted_iota(jnp.int32, lg.shape, 1)
    lane_f = lane.astype(jnp.float32)
    big = jnp.float32(4 * LANES)

    def first_lane(mask):
        return jnp.min(jnp.where(mask, lane_f, big), axis=1, keepdims=True).astype(jnp.int32)

    is_grp = lane < N_GROUPS
    gl = jnp.where(is_grp, lg, NEG_INF)
    gmax = jnp.max(gl, axis=1, keepdims=True)
    grp = first_lane(jnp.logical_and(is_grp, gl == gmax))
    p_grp = 1.0 / jnp.sum(jnp.where(is_grp, jnp.exp(gl - gmax), 0.0), axis=1, keepdims=True)
    eid = lane - N_GROUPS
    in_grp = jnp.logical_and(lane >= N_GROUPS + grp * EXPERTS_PER_GROUP,
                             lane < N_GROUPS + (grp + 1) * EXPERTS_PER_GROUP)
    el = jnp.where(in_grp, lg, NEG_INF)
    t1 = jnp.max(el, axis=1, keepdims=True)
    e1 = first_lane(jnp.logical_and(in_grp, el == t1)) - N_GROUPS
    rest = jnp.logical_and(in_grp, eid != e1)
    el2 = jnp.where(rest, lg, NEG_INF)
    t2 = jnp.max(el2, axis=1, keepdims=True)
    e2 = first_lane(jnp.logical_and(rest, el2 == t2)) - N_GROUPS
    d = jnp.exp(t2 - t1)
    g1 = p_grp / (1.0 + d)
    g2 = p_grp * d / (1.0 + d)

    oh1 = lane == e1
    oh2 = lane == e2
    oh = jnp.where(jnp.logical_or(oh1, oh2), 1.0, 0.0)
    r = lax.broadcasted_iota(jnp.int32, (TM, TM), 0)
    c = lax.broadcasted_iota(jnp.int32, (TM, TM), 1)
    tri = jnp.where(c < r, 1.0, 0.0).astype(jnp.bfloat16)
    before = _dot(tri, oh.astype(jnp.bfloat16)) + carry_ref[0:1, :]
    r1 = jnp.sum(jnp.where(oh1, before, 0.0), axis=1, keepdims=True)
    r2 = jnp.sum(jnp.where(oh2, before, 0.0), axis=1, keepdims=True)
    carry = carry_ref[0:1, :] + jnp.sum(oh, axis=0, keepdims=True)
    carry_ref[...] = jnp.broadcast_to(carry, carry_ref.shape)

    info = jnp.where(lane == 0, e1, jnp.where(lane == 1, e2, jnp.where(
        lane == 2, r1.astype(jnp.int32), jnp.where(lane == 3, r2.astype(jnp.int32), 0))))
    info_ref[...] = info
    gate_ref[...] = jnp.where(lane == 0, g1, jnp.where(lane == 1, g2, 0.0))
    cnt_ref[...] = jnp.broadcast_to(carry, cnt_ref.shape).astype(jnp.int32)


def _dispatch_kernel(n_blocks, plan_ref, dest_ref, v_hbm, xs_hbm, zbuf, vbuf, lsem, rsem, zsem):
    i = pl.program_id(0)
    n = pl.num_programs(0)

    def load(tile, slot):
        row0 = tile * TM
        rows = pl.ds(row0 if isinstance(row0, int) else pl.multiple_of(row0, TM), TM)
        return pltpu.make_async_copy(v_hbm.at[rows], vbuf.at[slot], lsem.at[slot])

    def wait_rows(slot):
        for k in range(TOP_K):
            pltpu.make_async_copy(vbuf.at[slot], xs_hbm.at[pl.ds(0, TM)], rsem.at[slot]).wait()

    @pl.when(i == 0)
    def _():
        zbuf[...] = jnp.zeros(zbuf.shape, zbuf.dtype)

        def zero_block(row0):
            rows = pl.ds(pl.multiple_of(row0, EXPERT_ROWS), EXPERT_ROWS)
            return pltpu.make_async_copy(zbuf, xs_hbm.at[rows], zsem)

        def for_each_zero_block(fn):
            def seg(e, _):
                @pl.when(plan_ref[N_EXPERTS + e] > 0)
                def _():
                    fn(zero_block(plan_ref[e] - EXPERT_ROWS))
                return 0

            def tail(b, _):
                fn(zero_block(b * EXPERT_ROWS))
                return 0

            lax.fori_loop(0, N_EXPERTS, seg, 0)
            lax.fori_loop(plan_ref[2 * N_EXPERTS], n_blocks, tail, 0)

        for_each_zero_block(lambda cp: cp.start())
        for_each_zero_block(lambda cp: cp.wait())
        load(0, 0).start()

    def step(slot):
        free = (slot + 1) % DISPATCH_SLOTS

        @pl.when(i >= DISPATCH_SLOTS - 1)
        def _():
            wait_rows(free)

        @pl.when(i + 1 < n)
        def _():
            load(i + 1, free).start()

        load(i, slot).wait()
        for r in range(TM):
            for k in range(TOP_K):
                pltpu.make_async_copy(vbuf.at[slot, pl.ds(r, 1)], xs_hbm.at[pl.ds(dest_ref[0, k, r], 1)],
                                      rsem.at[slot]).start(priority=k % DMA_PRIORITIES)

        @pl.when(i == n - 1)
        def _():
            @pl.when(i >= 1)
            def _():
                wait_rows((slot + DISPATCH_SLOTS - 1) % DISPATCH_SLOTS)
            wait_rows(slot)

    for slot in range(DISPATCH_SLOTS):
        pl.when(i % DISPATCH_SLOTS == slot)(functools.partial(step, slot))


def _dispatch(lay, v, dest, plan, n_blocks):
    W = v.shape[1]
    return pl.pallas_call(
        functools.partial(_dispatch_kernel, n_blocks),
        out_shape=jax.ShapeDtypeStruct((n_blocks * EXPERT_ROWS, W), v.dtype),
        grid=(lay.NT,),
        in_specs=[pl.BlockSpec(memory_space=pltpu.SMEM),
                  pl.BlockSpec((1, TOP_K, TM), lambda i: (i, 0, 0), memory_space=pltpu.SMEM),
                  pl.BlockSpec(memory_space=pl.ANY)],
        out_specs=pl.BlockSpec(memory_space=pl.ANY),
        scratch_shapes=[pltpu.VMEM((EXPERT_ROWS, W), v.dtype),
                        pltpu.VMEM((DISPATCH_SLOTS, TM, W), v.dtype),
                        pltpu.SemaphoreType.DMA((DISPATCH_SLOTS,)),
                        pltpu.SemaphoreType.DMA((DISPATCH_SLOTS,)), pltpu.SemaphoreType.DMA(())],
        compiler_params=pltpu.CompilerParams(dimension_semantics=("arbitrary",), has_side_effects=True),
        name="moe_dispatch",
    )(plan, dest, v)


def _expert_kernel(layer, be_ref, na_ref, nxt_ref, slot_ref, x_ref, wg_hbm, wu_hbm, wd_hbm, y_ref,
                   wg_buf, wu_buf, wd_buf, wgu_s, wd_s, sem):
    i = pl.program_id(0)
    de = wg_buf.shape[2]
    active = i < na_ref[0]
    e = be_ref[i]

    def fetch(expert, slot):
        return (pltpu.make_async_copy(wg_hbm.at[layer, expert], wg_buf.at[slot], sem.at[slot, 0]),
                pltpu.make_async_copy(wu_hbm.at[layer, expert], wu_buf.at[slot], sem.at[slot, 1]),
                pltpu.make_async_copy(wd_hbm.at[layer, expert], wd_buf.at[slot], sem.at[slot, 2]))

    @pl.when(i == 0)
    def _():
        for cp in fetch(e, slot_ref[e]):
            cp.start()

    @pl.when(jnp.logical_and(active, jnp.logical_or(i == 0, e != be_ref[jnp.maximum(i - 1, 0)])))
    def _():
        slot = slot_ref[e]
        for cp in fetch(e, slot):
            cp.wait()

        @pl.when(nxt_ref[e] >= 0)
        def _():
            for cp in fetch(nxt_ref[e], 1 - slot):
                cp.start()

        wgu_s[:, :de] = wg_buf[slot].astype(wgu_s.dtype)
        wgu_s[:, de:] = wu_buf[slot].astype(wgu_s.dtype)
        wd_s[...] = wd_buf[slot].astype(wd_s.dtype)

    @pl.when(active)
    def _():
        hi, lo = _unpack_rows(x_ref[...])
        x = jnp.concatenate([hi.astype(jnp.bfloat16), lo.astype(jnp.bfloat16)], axis=1)
        hgu = _dot(x, wgu_s[...])
        hg, hu = hgu[:, :de], hgu[:, de:]
        a = (hg * (1.0 / (1.0 + jnp.exp(-hg)))) * hu
        y_ref[...] = _pack_rows(_dot(a.astype(jnp.bfloat16), wd_s[...]))

    @pl.when(jnp.logical_not(active))
    def _():
        y_ref[...] = jnp.zeros(y_ref.shape, y_ref.dtype)


def _experts(lay, xs, blk_expert, n_active, nxt, slot, layer, wg, wu, wd):
    D = lay.D
    P, W = xs.shape
    DE = wg.shape[3]
    row_block = pl.BlockSpec((EXPERT_ROWS, W), lambda i, *_: (i, 0))
    hbm = pl.BlockSpec(memory_space=pl.ANY)
    grid_spec = pltpu.PrefetchScalarGridSpec(
        num_scalar_prefetch=4,
        grid=(P // EXPERT_ROWS,),
        in_specs=[row_block, hbm, hbm, hbm],
        out_specs=row_block,
        scratch_shapes=[pltpu.VMEM((2, D, DE), jnp.float32), pltpu.VMEM((2, D, DE), jnp.float32),
                        pltpu.VMEM((2, DE, D), jnp.float32),
                        pltpu.VMEM((D, 2 * DE), jnp.bfloat16), pltpu.VMEM((DE, D), jnp.bfloat16),
                        pltpu.SemaphoreType.DMA((2, 3))],
    )
    return pl.pallas_call(
        functools.partial(_expert_kernel, layer),
        out_shape=jax.ShapeDtypeStruct((P, W), xs.dtype),
        grid_spec=grid_spec,
        compiler_params=_cparams(("arbitrary",)),
        name="moe_experts",
    )(blk_expert, n_active, nxt, slot, xs, wg, wu, wd)


def _combine_kernel(final, *refs):
    dests, refs = refs[:COMBINE_AHEAD + 1], refs[COMBINE_AHEAD + 1:]
    ys_hbm, gate_ref, h_ref, g2_ref, gain_ref, sc_ref, sh_ref, *out_refs, ybuf, sem = refs
    i = pl.program_id(0)
    n_slots = COMBINE_AHEAD + 1

    def gather(dref, buf_slot, r0, r1):
        for r in range(r0, r1):
            for k in range(TOP_K):
                pltpu.make_async_copy(ys_hbm.at[pl.ds(dref[0, k, r], 1)], ybuf.at[buf_slot, k, pl.ds(r, 1)],
                                      sem.at[buf_slot]).start(priority=k % DMA_PRIORITIES)

    def wait_tile(buf_slot):
        for k in range(TOP_K):
            pltpu.make_async_copy(ys_hbm.at[pl.ds(0, TM)], ybuf.at[buf_slot, k], sem.at[buf_slot]).wait()

    @pl.when(i == 0)
    def _():
        for a in range(COMBINE_AHEAD):
            gather(dests[a], a, 0, TM)

    def step(slot):
        wait_tile(slot)
        for r0 in range(0, TM, COMBINE_ROWS):
            rows = pl.ds(r0, COMBINE_ROWS)
            gate = gate_ref[rows, :]
            hi0, lo0 = _unpack_rows(ybuf[slot, 0, rows, :])
            hi1, lo1 = _unpack_rows(ybuf[slot, 1, rows, :])
            g0, g1 = gate[:, 0:1], gate[:, 1:2]
            f = jnp.concatenate([g0 * hi0 + g1 * hi1, g0 * lo0 + g1 * lo1], axis=1)
            h2 = h_ref[rows, :] + g2_ref[...] * f
            if final:
                (out_ref,) = out_refs
                out_ref[rows, :] = _rms(h2, gain_ref[...])
            else:
                h2_ref, u_ref = out_refs
                h2_ref[rows, :] = h2
                u_ref[rows, :] = (_rms(h2, gain_ref[...]) * (1.0 + sc_ref[...]) + sh_ref[...]).astype(u_ref.dtype)
            gather(dests[COMBINE_AHEAD], (slot + COMBINE_AHEAD) % n_slots, r0, r0 + COMBINE_ROWS)

        @pl.when(i == pl.num_programs(0) - 1)
        def _():
            for a in range(1, n_slots):
                wait_tile((slot + a) % n_slots)

    for slot in range(n_slots):
        pl.when(i % n_slots == slot)(functools.partial(step, slot))


def _combine(lay, ys, dest, gates, h1, mod, mod_next, gain_next, final):
    D = lay.D
    row_tile = pl.BlockSpec((TM, D), lambda i: (i, 0))
    if final:
        n = lay.B * lay.LT
        tile = lambda i: (i // lay.LT) * lay.TPB + i % lay.LT
        out_shape = (jax.ShapeDtypeStruct((lay.B * lay.S, D), jnp.float32),)
        out_specs = (row_tile,)
    else:
        n = lay.NT
        tile = lambda i: i
        out_shape = (jax.ShapeDtypeStruct((lay.T, D), jnp.float32),
                     jax.ShapeDtypeStruct((lay.T, D), jnp.bfloat16))
        out_specs = (row_tile, row_tile)
    return pl.pallas_call(
        functools.partial(_combine_kernel, final),
        out_shape=out_shape,
        grid=(n,),
        in_specs=[pl.BlockSpec((1, TOP_K, TM), lambda i, a=a: (tile(jnp.minimum(i + a, n - 1)), 0, 0),
                               memory_space=pltpu.SMEM) for a in range(COMBINE_AHEAD + 1)] + [
                  pl.BlockSpec(memory_space=pl.ANY),
                  pl.BlockSpec((TM, LANES), lambda i: (tile(i), 0)),
                  pl.BlockSpec((TM, D), lambda i: (tile(i), 0)),
                  _mod_spec(lay, 5, tile),
                  _row_spec(D),
                  _mod_spec(lay, 1, tile), _mod_spec(lay, 0, tile)],
        out_specs=out_specs,
        scratch_shapes=[pltpu.VMEM((COMBINE_AHEAD + 1, TOP_K, TM, ys.shape[1]), ys.dtype),
                        pltpu.SemaphoreType.DMA((COMBINE_AHEAD + 1,))],
        compiler_params=_cparams(("arbitrary",)),
        name="moe_combine_final" if final else "moe_combine",
    )(*([dest] * (COMBINE_AHEAD + 1)), ys, gates, h1, mod, gain_next.reshape(1, D), mod_next, mod_next)


def _moe(lay, v, info, gates, cnt, h1, mod, mod_next, gain_next, layer, wg, wu, wd, final):
    T, NT = lay.T, lay.NT
    counts = cnt[0, :N_EXPERTS]
    padded = ((counts + EXPERT_ROWS - 1) // EXPERT_ROWS) * EXPERT_ROWS
    pad_end = jnp.cumsum(padded)
    pad_start = pad_end - padded
    expert = info[:, :TOP_K]
    onehot = expert[:, :, None] == jnp.arange(N_EXPERTS, dtype=jnp.int32)
    dest = info[:, TOP_K:2 * TOP_K] + jnp.sum(jnp.where(onehot, pad_start, 0), axis=-1)
    dest = dest.reshape(NT, TM, TOP_K).transpose(0, 2, 1).astype(jnp.int32)
    n_blocks = -(-(T * TOP_K + N_EXPERTS * (EXPERT_ROWS - 1)) // EXPERT_ROWS)
    blk_start = jnp.arange(n_blocks, dtype=jnp.int32) * EXPERT_ROWS
    blk_expert = jnp.minimum(jnp.sum(pad_end[None, :] <= blk_start[:, None], axis=1),
                             N_EXPERTS - 1).astype(jnp.int32)
    n_active = (pad_end[-1:] // EXPERT_ROWS).astype(jnp.int32)
    plan = jnp.concatenate([pad_end, padded, n_active]).astype(jnp.int32)
    xs = _dispatch(lay, v, dest, plan, n_blocks)
    owns = padded > 0
    ids = jnp.arange(N_EXPERTS, dtype=jnp.int32)
    later = jnp.where(owns[None, :] & (ids[None, :] > ids[:, None]), ids[None, :], N_EXPERTS)
    nxt = jnp.min(later, axis=1)
    nxt = jnp.where(nxt == N_EXPERTS, -1, nxt).astype(jnp.int32)
    slot = ((jnp.cumsum(owns.astype(jnp.int32)) - 1) % 2).astype(jnp.int32)
    ys = _experts(lay, xs, blk_expert, n_active, nxt, slot, layer, wg, wu, wd)
    return _combine(lay, ys, dest, gates, h1, mod, mod_next, gain_next, final)


def kernel(x, c, ctx, c_ctx, ada_w, ada_b, norm_mix, norm_ffn, norm_final, conv_in, conv_k, conv_out, gqa_qkv, gqa_q_gain, gqa_k_gain, gqa_out, pool_w, pool_scale, swa_qkv, swa_sink, swa_out, router_grp_w, router_grp_b, router_exp_w, router_exp_b, exp_gate, exp_up, exp_down):
    B, S, D = x.shape
    C = ctx.shape[1]
    L = ada_w.shape[0]
    lay = Layout(B, S, C, D)
    bf = jnp.bfloat16

    R = SUBLANES_BF16
    cvec = jnp.zeros((R, D), jnp.float32).at[:B].set(c).at[B].set(c_ctx)
    mod_all = _modulation(cvec, ada_w, ada_b)[:, :B + 1].reshape(L, B + 1, 6, 1, D).transpose(0, 2, 1, 3, 4)

    pad = LANES - N_GROUPS - N_EXPERTS
    wr_all = jnp.concatenate([router_grp_w, router_exp_w, jnp.zeros((L, D, pad), jnp.float32)], axis=-1).astype(bf)
    br_all = jnp.concatenate([router_grp_b, router_exp_b, jnp.zeros((L, pad), jnp.float32)], axis=-1)

    h, u = _prenorm(lay, x, ctx, norm_mix[0], mod_all[0])
    one = jnp.ones((1, LANES), jnp.float32)

    for i in range(L):
        m, j = i % 4, i // 4
        mod = mod_all[i]
        wr, br = wr_all[i], br_all[i].reshape(1, LANES)
        tail = (h, mod, norm_ffn[i], wr, br)
        if m == 0:
            bg, z = _conv_in(lay, u, conv_in[j].astype(bf))
            routed = _conv_out(lay, z, bg, conv_k[j], conv_out[j].astype(bf), *tail)
        elif m == 1:
            hd = D // GQA_HEADS
            w = gqa_qkv[j].astype(bf)
            cos_t, sin_t = _rope_tables(lay, hd)
            nq, nk = GQA_HEADS * hd, GQA_KV_HEADS * hd
            qg = gqa_q_gain[j].reshape(1, hd)
            kg = gqa_k_gain[j].reshape(1, hd)
            q, k, vv = _qkv_project(lay, u, w, nq, nk, hd, qg, kg, cos_t, sin_t, True, hd ** -0.5 * LOG2E, "gqa_qkv")
            q3, k3, v3 = (a.reshape(B, lay.SB, -1) for a in (q, k, vv))
            n_rep = GQA_HEADS // GQA_KV_HEADS
            tq = 512 if S % 512 == 0 else TM
            tk = 768 if lay.SB % 768 == 0 else TM
            o_lat = _flash(lay, q3, k3, v3, GQA_KV_HEADS, n_rep, tq, tk, 0, S // tq, lay.SB, 0, "gqa_flash")
            o_ctx = _flash(lay, q3, k3, v3, GQA_KV_HEADS, n_rep, C, C, S // C, 1, C, S // C, "gqa_flash_ctx")
            routed = _out_proj(lay, o_lat, o_ctx, gqa_out[j].astype(bf), *tail)
        elif m == 2:
            routed = _pool(lay, u, pool_w[j].astype(bf), pool_scale[j], *tail)
        else:
            hd = D // SWA_HEADS
            w = swa_qkv[j].astype(bf)
            cos_t, sin_t = _rope_tables(lay, hd)
            nq, nk = SWA_HEADS * hd, SWA_KV_HEADS * hd
            q, k, vv = _qkv_project(lay, u, w, nq, nk, hd, one, one, cos_t, sin_t, False, hd ** -0.5 * LOG2E, "swa_qkv")
            q3, k3, v3 = (a.reshape(B, lay.SB, -1) for a in (q, k, vv))
            o_lat = _swa(lay, q3, k3, v3, swa_sink[j] * LOG2E)
            routed = _out_proj(lay, o_lat, None, swa_out[j].astype(bf), *tail)
        final = i == L - 1
        mod_next = mod if final else mod_all[i + 1]
        gain_next = norm_final if final else norm_mix[i + 1]
        h1, v, info, gates, cnt = routed
        res = _moe(lay, v, info, gates, cnt, h1, mod, mod_next, gain_next, i, exp_gate, exp_up, exp_down, final)
        if final:
            return res[0].reshape(B, S, D)
        h, u = res
```
